```python
import jax, jax.numpy as jnp
from jax import lax
import numpy as np

D_MODEL = 2048
BATCH = 2
SEQ = 16384
DEPTH = 4
DEC_BATCH = 8
DEC_SEQ = 4096
PAST_LEN = 128

GRID_W = 64
MIX_W = D_MODEL
GLA_H = 4
GLA_VW = MIX_W // 2
GLA_KW = GLA_VW // 2
GLA_DK = GLA_KW // GLA_H
GLA_DV = GLA_VW // GLA_H
GLA_RANK = 16
GLA_NORMALIZER = 16.0
GLA_CHUNK = 64
NA_W = MIX_W - GLA_VW
NA_H = 8
NA_HD = NA_W // NA_H
WIN_ROWS = 8
WIN_COLS = 16
FF_DIM = 2 * D_MODEL
N_EXP = 8
TOP_K = 2
EXP_DIM = D_MODEL // 2
PLE_DIM = 256
EPS = 1e-6
NEG_INF = -1e30
IN_SIZES = (GLA_KW, GLA_KW, GLA_VW, GLA_VW, GLA_RANK, GLA_RANK, NA_W, NA_W, NA_W)
IN_SPLITS = tuple(int(s) for s in np.cumsum(IN_SIZES)[:-1])
N_IN = sum(IN_SIZES)
N_DENSE = (DEPTH + 1) // 2
N_MOE = DEPTH // 2

kernel_name = "hybrid_gla_natten_moe_encoder"


def rmsnorm(x, gain):
    xf = x.astype(jnp.float32)
    y = xf * lax.rsqrt(jnp.mean(xf * xf, axis=-1, keepdims=True) + EPS)
    return (y * gain.astype(jnp.float32)).astype(x.dtype)


def swiglu(u, w_gate, w_up, w_down):
    return (jax.nn.silu(u @ w_gate) * (u @ w_up)) @ w_down


def gla_causal(q, k, v, log_a):
    B, T, H, DK = q.shape
    DV = v.shape[-1]
    C = GLA_CHUNK
    n = T // C

    def chunks(t):
        return t.astype(jnp.float32).reshape(B, n, C, H, t.shape[-1]).transpose(1, 0, 3, 2, 4)

    qc, kc, vc, ac = chunks(q), chunks(k), chunks(v), chunks(log_a)
    b = jnp.cumsum(ac, axis=3)
    b_last = b[..., -1:, :]
    qt = qc * jnp.exp(b) * (DK ** -0.5)
    kt = kc * jnp.exp(-b)
    ke = kc * jnp.exp(b_last - b)
    dl = jnp.exp(b_last[..., 0, :])
    mask = jnp.tril(jnp.ones((C, C), dtype=bool))
    att = jnp.where(mask, jnp.einsum('nbhid,nbhjd->nbhij', qt, kt), 0.0)
    o_intra = jnp.einsum('nbhij,nbhje->nbhie', att, vc)

    def step(S, xs):
        qt_c, ke_c, v_c, dl_c = xs
        o_inter = jnp.einsum('bhcd,bhde->bhce', qt_c, S)
        S = dl_c[..., None] * S + jnp.einsum('bhcd,bhce->bhde', ke_c, v_c)
        return S, o_inter

    S0 = jnp.zeros((B, H, DK, DV), jnp.float32)
    _, o_inter = lax.scan(step, S0, (qt, ke, vc, dl))
    return (o_intra + o_inter).transpose(1, 0, 3, 2, 4).reshape(B, T, H, DV)


def gla_bidirectional(q, k, v, la_fwd, la_bwd):
    fwd = gla_causal(q, k, v, la_fwd)
    flip = lambda t: jnp.flip(t, axis=1)
    bwd = flip(gla_causal(flip(q), flip(k), flip(v), flip(la_bwd)))
    return fwd + bwd


def neighborhood_attention(q, k, v, rpb):
    B, T, H, Dh = q.shape
    rows = T // GRID_W
    kr = min(WIN_ROWS, rows)
    qg = q.reshape(B, rows, GRID_W, H, Dh)
    kg = k.reshape(B, rows, GRID_W, H, Dh)
    vg = v.reshape(B, rows, GRID_W, H, Dh)
    r = jnp.arange(rows, dtype=jnp.int32)
    r_start = jnp.clip(r - kr // 2, 0, rows - kr)
    dr = r_start[:, None] + jnp.arange(kr, dtype=jnp.int32)[None, :] - r[:, None] + (WIN_ROWS - 1)
    c = jnp.arange(GRID_W, dtype=jnp.int32)
    c_start = jnp.clip(c - WIN_COLS // 2, 0, GRID_W - WIN_COLS)
    valid = (c[None, :] >= c_start[:, None]) & (c[None, :] < c_start[:, None] + WIN_COLS)
    dc = jnp.clip(c[None, :] - c[:, None], -(WIN_COLS - 1), WIN_COLS - 1) + (WIN_COLS - 1)
    bias_c = jnp.where(valid, rpb.astype(jnp.float32)[:, :, dc], NEG_INF)
    scale = Dh ** -0.5

    def row_block(args):
        q_row, rs, dr_row = args
        k_blk = lax.dynamic_slice_in_dim(kg, rs, kr, axis=1)
        v_blk = lax.dynamic_slice_in_dim(vg, rs, kr, axis=1)
        bias = jnp.take(bias_c, dr_row, axis=1).transpose(0, 2, 1, 3)
        s = jnp.einsum('bqhd,bikhd->bhqik', q_row, k_blk).astype(jnp.float32) * scale + bias[None]
        p = jax.nn.softmax(s.reshape(B, H, GRID_W, kr * GRID_W), axis=-1).reshape(s.shape)
        return jnp.einsum('bhqik,bikhd->bqhd', p.astype(v_blk.dtype), v_blk)

    out = lax.map(row_block, (qg.transpose(1, 0, 2, 3, 4), r_start, dr))
    return out.transpose(1, 0, 2, 3, 4).reshape(B, T, H, Dh)


def token_mixer(u, w_in, w_gate_f, b_gate_f, w_gate_b, b_gate_b, gla_norm, q_norm, k_norm, rpb, attn_norm, w_o):
    B, T, _ = u.shape
    z = u @ w_in
    gq, gk, gv, gg, lrf, lrb, aq, ak, av = jnp.split(z, IN_SPLITS, axis=-1)
    la_f = jax.nn.log_sigmoid((lrf @ w_gate_f + b_gate_f).astype(jnp.float32)) / GLA_NORMALIZER
    la_b = jax.nn.log_sigmoid((lrb @ w_gate_b + b_gate_b).astype(jnp.float32)) / GLA_NORMALIZER
    hk = lambda t, d: t.reshape(B, T, GLA_H, d)
    o_gla = gla_bidirectional(hk(gq, GLA_DK), hk(gk, GLA_DK), hk(gv, GLA_DV), hk(la_f, GLA_DK), hk(la_b, GLA_DK))
    o_gla = rmsnorm(o_gla, gla_norm.reshape(GLA_H, GLA_DV)).reshape(B, T, GLA_VW).astype(u.dtype)
    o_gla = o_gla * jax.nn.silu(gg)
    qa = rmsnorm(aq.reshape(B, T, NA_H, NA_HD), q_norm)
    ka = rmsnorm(ak.reshape(B, T, NA_H, NA_HD), k_norm)
    va = av.reshape(B, T, NA_H, NA_HD)
    o_na = rmsnorm(neighborhood_attention(qa, ka, va, rpb).reshape(B, T, NA_W), attn_norm)
    return jnp.concatenate([o_gla, o_na], axis=-1) @ w_o


def moe_ffn(u, w_router, w_gate, w_up, w_down):
    logits = (u @ w_router).astype(jnp.float32)
    top_v, top_i = lax.top_k(logits, TOP_K)
    top_w = jax.nn.softmax(top_v, axis=-1)
    gates = jnp.sum(jax.nn.one_hot(top_i, N_EXP, dtype=jnp.float32) * top_w[..., None], axis=-2)
    hg = jnp.einsum('btd,edf->btef', u, w_gate)
    hu = jnp.einsum('btd,edf->btef', u, w_up)
    hidden = jax.nn.silu(hg) * hu * gates[..., None].astype(u.dtype)
    return jnp.einsum('btef,efd->btd', hidden, w_down)


def setup_inputs(seed: int = 0) -> dict:
    key = jax.random.key(seed)
    ks = jax.random.split(key, 28)
    f32 = jnp.float32

    def nrm(k, shape, scale):
        return jax.random.normal(k, shape, f32) * scale

    def gain(k, shape):
        return 1.0 + 0.02 * jax.random.normal(k, shape, f32)

    return {
        "x_prompt": nrm(ks[0], (BATCH, SEQ, D_MODEL), 1.0),
        "x_sample": nrm(ks[1], (DEC_BATCH, DEC_SEQ, D_MODEL), 1.0),
        "p_prompt": nrm(ks[2], (DEPTH, BATCH, SEQ, PLE_DIM), 1.0),
        "p_sample": nrm(ks[3], (DEPTH, DEC_BATCH, DEC_SEQ, PLE_DIM), 1.0),
        "g_mix": gain(ks[4], (DEPTH, D_MODEL)),
        "w_in": nrm(ks[5], (DEPTH, D_MODEL, N_IN), D_MODEL ** -0.5),
        "w_gate_f": nrm(ks[6], (DEPTH, GLA_RANK, GLA_KW), GLA_RANK ** -0.5),
        "b_gate_f": nrm(ks[7], (DEPTH, GLA_KW), 0.1),
        "w_gate_b": nrm(ks[8], (DEPTH, GLA_RANK, GLA_KW), GLA_RANK ** -0.5),
        "b_gate_b": nrm(ks[9], (DEPTH, GLA_KW), 0.1),
        "gla_norm": gain(ks[10], (DEPTH, GLA_VW)),
        "q_norm": gain(ks[11], (DEPTH, NA_HD)),
        "k_norm": gain(ks[12], (DEPTH, NA_HD)),
        "rpb": nrm(ks[13], (DEPTH, NA_H, 2 * WIN_ROWS - 1, 2 * WIN_COLS - 1), 0.5),
        "attn_norm": gain(ks[14], (DEPTH, NA_W)),
        "w_o": nrm(ks[15], (DEPTH, MIX_W, D_MODEL), MIX_W ** -0.5),
        "g_ffn": gain(ks[16], (DEPTH, D_MODEL)),
        "w_ff_gate": nrm(ks[17], (N_DENSE, D_MODEL, FF_DIM), D_MODEL ** -0.5),
        "w_ff_up": nrm(ks[18], (N_DENSE, D_MODEL, FF_DIM), D_MODEL ** -0.5),
        "w_ff_down": nrm(ks[19], (N_DENSE, FF_DIM, D_MODEL), FF_DIM ** -0.5),
        "w_router": nrm(ks[20], (N_MOE, D_MODEL, N_EXP), D_MODEL ** -0.5),
        "w_exp_gate": nrm(ks[21], (N_MOE, N_EXP, D_MODEL, EXP_DIM), D_MODEL ** -0.5),
        "w_exp_up": nrm(ks[22], (N_MOE, N_EXP, D_MODEL, EXP_DIM), D_MODEL ** -0.5),
        "w_exp_down": nrm(ks[23], (N_MOE, N_EXP, EXP_DIM, D_MODEL), EXP_DIM ** -0.5),
        "g_ple_in": gain(ks[24], (DEPTH, D_MODEL)),
        "w_ple_gate": nrm(ks[25], (DEPTH, D_MODEL, D_MODEL), D_MODEL ** -0.5),
        "w_ple": nrm(ks[26], (DEPTH, PLE_DIM, D_MODEL), PLE_DIM ** -0.5),
        "g_ple_out": gain(ks[27], (DEPTH, D_MODEL)),
    }


def reference(x_prompt, x_sample, p_prompt, p_sample, g_mix, w_in, w_gate_f, b_gate_f, w_gate_b, b_gate_b,
              gla_norm, q_norm, k_norm, rpb, attn_norm, w_o, g_ffn, w_ff_gate, w_ff_up, w_ff_down,
              w_router, w_exp_gate, w_exp_up, w_exp_down, g_ple_in, w_ple_gate, w_ple, g_ple_out):
    def trunk(x, p):
        h = x
        for i in range(DEPTH):
            u = rmsnorm(h, g_mix[i])
            h = h + token_mixer(u, w_in[i], w_gate_f[i], b_gate_f[i], w_gate_b[i], b_gate_b[i], gla_norm[i],
                                q_norm[i], k_norm[i], rpb[i], attn_norm[i], w_o[i])
            u = rmsnorm(h, g_ffn[i])
            j = i // 2
            if i % 2 == 0:
                f = swiglu(u, w_ff_gate[j], w_ff_up[j], w_ff_down[j])
            else:
                f = moe_ffn(u, w_router[j], w_exp_gate[j], w_exp_up[j], w_exp_down[j])
            h = h + f
            e = rmsnorm(p[i].astype(h.dtype) @ w_ple[i], g_ple_out[i])
            gate = jax.nn.sigmoid(rmsnorm(h, g_ple_in[i]) @ w_ple_gate[i])
            h = h + gate * e
        return h

    y_prompt = trunk(x_prompt, p_prompt)
    y_sample = trunk(x_sample, p_sample)
    return (y_prompt, y_sample)
```

```python
import functools

import jax
import jax.numpy as jnp
from jax import lax
from jax.experimental import pallas as pl
from jax.experimental.pallas import tpu as pltpu

F32 = jnp.float32
BF16 = jnp.bfloat16

D_MODEL = 2048
DEPTH = 4
GRID_W = 64
GLA_H = 4
GLA_VW = 1024
GLA_KW = 512
GLA_DK = 128
GLA_DV = 256
GLA_RANK = 16
GLA_INV_NORMALIZER = 1.0 / 16.0
GLA_CHUNK = 64
NA_W = 1024
NA_H = 8
NA_HD = 128
WIN_ROWS = 8
WIN_COLS = 16
FF_DIM = 4096
N_EXP = 8
EXP_DIM = 1024
PLE_DIM = 256
EPS = 1e-6
NEG_INF = -1e30

Z_GQ, Z_GK, Z_GV, Z_GG = 0, 512, 1024, 2048
Z_AQ, Z_AK, Z_AV, Z_LR = 3072, 4096, 5120, 6144
LR_W = 256
NZ = Z_LR + LR_W

V7X_VMEM_BYTES = 64 * 1024 * 1024
LANES = 128


def _vmem(nbytes):
    return int(min(nbytes * 3 // 2 + (4 << 20), V7X_VMEM_BYTES - (6 << 20)))


def _rms(x, g):
    ms = jnp.mean(x * x, axis=-1, keepdims=True)
    return x * lax.rsqrt(ms + EPS) * g


def _silu(x):
    return x * jax.nn.sigmoid(x)


def _inproj_kernel(h_ref, g_ref, w_ref, z_ref, u_scr):
    @pl.when(pl.program_id(1) == 0)
    def _():
        u_scr[...] = _rms(h_ref[...], g_ref[...]).astype(BF16)

    z_ref[...] = jnp.dot(u_scr[...], w_ref[...], preferred_element_type=F32).astype(z_ref.dtype)


def inproj(h, g, w, *, tm=512, tn=1280):
    n = h.shape[0]
    est = 2 * tm * D_MODEL * 4 + 2 * D_MODEL * tn * 2 + 2 * tm * tn * 2 + tm * D_MODEL * 2 + tm * tn * 4
    return pl.pallas_call(
        _inproj_kernel,
        grid=(n // tm, NZ // tn),
        in_specs=[
            pl.BlockSpec((tm, D_MODEL), lambda i, j: (i, 0)),
            pl.BlockSpec((1, D_MODEL), lambda i, j: (0, 0)),
            pl.BlockSpec((D_MODEL, tn), lambda i, j: (0, j)),
        ],
        out_specs=pl.BlockSpec((tm, tn), lambda i, j: (i, j)),
        out_shape=jax.ShapeDtypeStruct((n, NZ), BF16),
        scratch_shapes=[pltpu.VMEM((tm, D_MODEL), BF16)],
        compiler_params=pltpu.CompilerParams(
            dimension_semantics=("parallel", "arbitrary"), vmem_limit_bytes=_vmem(est)),
        name="inproj",
    )(h, g, w)


def _cumsum_rows(x):
    n = x.shape[0]
    idx = lax.broadcasted_iota(jnp.int32, x.shape, 0)
    s = 1
    while s < n:
        x = x + jnp.where(idx >= s, pltpu.roll(x, s, 0), 0.0)
        s *= 2
    return x


def _log_sigmoid(x):
    return jnp.minimum(x, 0.0) - jnp.log1p(jnp.exp(-jnp.abs(x)))


def _gla_kernel(*refs, reverse, final, tb):
    if final:
        q_ref, k_ref, v_ref, lr_ref, wg_ref, bg_ref, of_ref, gg_ref, gn_ref, o_ref, s_scr, la_scr = refs
    else:
        q_ref, k_ref, v_ref, lr_ref, wg_ref, bg_ref, o_ref, s_scr, la_scr = refs
    C = GLA_CHUNK
    nchunk = tb // C

    @pl.when(pl.program_id(2) == 0)
    def _():
        s_scr[...] = jnp.zeros_like(s_scr)

    x = jnp.dot(lr_ref[...], wg_ref[...], preferred_element_type=F32) + bg_ref[...]
    la_scr[...] = _log_sigmoid(x) * GLA_INV_NORMALIZER

    row = lax.broadcasted_iota(jnp.int32, (C, C), 0)
    col = lax.broadcasted_iota(jnp.int32, (C, C), 1)
    mask = (col >= row) if reverse else (col <= row)
    scale = GLA_DK ** -0.5

    def chunk(ci, carry):
        c = (nchunk - 1 - ci) if reverse else ci
        off = pl.multiple_of(c * C, C)
        la = la_scr[pl.ds(off, C), :]
        b = _cumsum_rows(la)
        blast = b[C - 1:C, :]
        if reverse:
            b = blast - b + la
        q = q_ref[pl.ds(off, C), :].astype(F32)
        k = k_ref[pl.ds(off, C), :].astype(F32)
        v = v_ref[pl.ds(off, C), :]
        qt = (q * jnp.exp(b) * scale).astype(BF16)
        kt = (k * jnp.exp(-b)).astype(BF16)
        ke = k * jnp.exp(blast - b)
        dl = jnp.exp(blast)
        att = lax.dot_general(qt, kt, (((1,), (1,)), ((), ())), preferred_element_type=F32)
        att = jnp.where(mask, att, 0.0).astype(BF16)
        s_old = s_scr[...]
        o = (jnp.dot(att, v, preferred_element_type=F32)
             + jnp.dot(qt, s_old.astype(BF16), preferred_element_type=F32))
        ke_t = ke.T.astype(BF16)
        dl_col = jnp.broadcast_to(dl, (8, GLA_DK)).T[:, 0:1]
        s_scr[...] = dl_col * s_old + jnp.dot(ke_t, v, preferred_element_type=F32)
        if final:
            ot = o + of_ref[pl.ds(off, C), :]
            y = _rms(ot, gn_ref[...])
            gg = gg_ref[pl.ds(off, C), :].astype(F32)
            o_ref[pl.ds(off, C), :] = (y * _silu(gg)).astype(o_ref.dtype)
        else:
            o_ref[pl.ds(off, C), :] = o
        return carry

    lax.fori_loop(0, nchunk, chunk, 0)


def gla_direction(z3, wg, bg, *, reverse, o_other=None, gn=None, tb=256):
    B, T, _ = z3.shape
    nb = T // tb
    final = o_other is not None
    blk = (lambda j: nb - 1 - j) if reverse else (lambda j: j)
    qw, vw = GLA_DK, GLA_DV
    in_specs = [
        pl.BlockSpec((None, tb, qw), lambda b, h, j: (b, blk(j), Z_GQ // qw + h)),
        pl.BlockSpec((None, tb, qw), lambda b, h, j: (b, blk(j), Z_GK // qw + h)),
        pl.BlockSpec((None, tb, vw), lambda b, h, j: (b, blk(j), Z_GV // vw + h)),
        pl.BlockSpec((None, tb, LR_W), lambda b, h, j: (b, blk(j), Z_LR // LR_W)),
        pl.BlockSpec((LR_W, qw), lambda b, h, j: (0, h)),
        pl.BlockSpec((1, qw), lambda b, h, j: (0, h)),
    ]
    args = [z3, z3, z3, z3, wg, bg]
    if final:
        in_specs += [
            pl.BlockSpec((None, tb, vw), lambda b, h, j: (b, blk(j), h)),
            pl.BlockSpec((None, tb, vw), lambda b, h, j: (b, blk(j), Z_GG // vw + h)),
            pl.BlockSpec((None, 1, vw), lambda b, h, j: (h, 0, 0)),
        ]
        args += [o_other, z3, gn]
    out_dtype = BF16 if final else F32
    return pl.pallas_call(
        functools.partial(_gla_kernel, reverse=reverse, final=final, tb=tb),
        grid=(B, GLA_H, nb),
        in_specs=in_specs,
        out_specs=pl.BlockSpec((None, tb, vw), lambda b, h, j: (b, blk(j), h)),
        out_shape=jax.ShapeDtypeStruct((B, T, GLA_VW), out_dtype),
        scratch_shapes=[pltpu.VMEM((GLA_DK, GLA_DV), F32), pltpu.VMEM((tb, GLA_DK), F32)],
        compiler_params=pltpu.CompilerParams(
            dimension_semantics=("parallel", "parallel", "arbitrary"), vmem_limit_bytes=_vmem(8 << 20)),
        name="gla_bwd" if reverse else "gla_fwd",
    )(*args)


def _natten_kernel(q_ref, k_ref, v_ref, bias_ref, qg_ref, kg_ref, o_ref, kn_scr, *, rows, rq):
    T = rows * GRID_W
    KB = 512
    WK = WIN_ROWS * GRID_W
    j = pl.program_id(2)

    @pl.when(j == 0)
    def _():
        def norm_block(i, carry):
            off = pl.multiple_of(i * KB, KB)
            kk = k_ref[pl.ds(off, KB), :].astype(F32)
            kn_scr[pl.ds(off, KB), :] = _rms(kk, kg_ref[...]).astype(BF16)
            return carry

        lax.fori_loop(0, T // KB, norm_block, 0)

    scale = NA_HD ** -0.5

    def row(i, carry):
        r = j * rq + i
        rs = jnp.clip(r - WIN_ROWS // 2, 0, rows - WIN_ROWS)
        dti = rs - r + (WIN_ROWS - 1)
        qoff = pl.multiple_of(i * GRID_W, GRID_W)
        koff = pl.multiple_of(rs * GRID_W, GRID_W)
        qn = _rms(q_ref[pl.ds(qoff, GRID_W), :].astype(F32), qg_ref[...]).astype(BF16)
        kw = kn_scr[pl.ds(koff, WK), :]
        vw = v_ref[pl.ds(koff, WK), :]
        s = lax.dot_general(qn, kw, (((1,), (1,)), ((), ())), preferred_element_type=F32)
        s = s * scale + bias_ref[dti]
        m = jnp.max(s, axis=-1, keepdims=True)
        p = jnp.exp(s - m)
        l = jnp.sum(p, axis=-1, keepdims=True)
        o = jnp.dot(p.astype(BF16), vw, preferred_element_type=F32) / l
        o_ref[pl.ds(qoff, GRID_W), :] = o.astype(o_ref.dtype)
        return carry

    lax.fori_loop(0, rq, row, 0)


def natten(z3, bias_tbl, qg, kg, *, rq=8):
    B, T, _ = z3.shape
    rows = T // GRID_W
    assert rows >= WIN_ROWS and rows % rq == 0
    tq = rq * GRID_W
    hd = NA_HD
    est = 2 * 2 * T * hd * 2 + T * hd * 2 + 2 * WIN_ROWS * GRID_W * WIN_ROWS * GRID_W * 4 + 4 * tq * hd * 2
    return pl.pallas_call(
        functools.partial(_natten_kernel, rows=rows, rq=rq),
        grid=(B, NA_H, rows // rq),
        in_specs=[
            pl.BlockSpec((None, tq, hd), lambda b, h, j: (b, j, Z_AQ // hd + h)),
            pl.BlockSpec((None, T, hd), lambda b, h, j: (b, 0, Z_AK // hd + h)),
            pl.BlockSpec((None, T, hd), lambda b, h, j: (b, 0, Z_AV // hd + h)),
            pl.BlockSpec((None, WIN_ROWS, GRID_W, WIN_ROWS * GRID_W), lambda b, h, j: (h, 0, 0, 0)),
            pl.BlockSpec((1, hd), lambda b, h, j: (0, 0)),
            pl.BlockSpec((1, hd), lambda b, h, j: (0, 0)),
        ],
        out_specs=pl.BlockSpec((None, tq, hd), lambda b, h, j: (b, j, h)),
        out_shape=jax.ShapeDtypeStruct((B, T, NA_W), BF16),
        scratch_shapes=[pltpu.VMEM((T, hd), BF16)],
        compiler_params=pltpu.CompilerParams(
            dimension_semantics=("parallel", "parallel", "arbitrary"), vmem_limit_bytes=_vmem(est)),
        name="natten",
    )(z3, z3, z3, bias_tbl, qg, kg)


def natten_bias_table(rpb):
    c = jnp.arange(GRID_W, dtype=jnp.int32)
    c_start = jnp.clip(c - WIN_COLS // 2, 0, GRID_W - WIN_COLS)
    valid = (c[None, :] >= c_start[:, None]) & (c[None, :] < c_start[:, None] + WIN_COLS)
    dc = jnp.clip(c[None, :] - c[:, None], -(WIN_COLS - 1), WIN_COLS - 1) + (WIN_COLS - 1)
    bias_c = jnp.where(valid, rpb.astype(F32)[:, :, dc], NEG_INF)
    dr = jnp.arange(WIN_ROWS)[:, None] + jnp.arange(WIN_ROWS)[None, :]
    tbl = bias_c[:, dr]
    return tbl.transpose(0, 1, 3, 2, 4).reshape(NA_H, WIN_ROWS, GRID_W, WIN_ROWS * GRID_W)


def _outproj_kernel(og_ref, on_ref, an_ref, w_ref, h_ref, o_ref):
    na = _rms(on_ref[...].astype(F32), an_ref[...]).astype(BF16)
    acc = jnp.dot(og_ref[...], w_ref[0:GLA_VW, :], preferred_element_type=F32)
    acc += jnp.dot(na, w_ref[GLA_VW:GLA_VW + NA_W, :], preferred_element_type=F32)
    o_ref[...] = h_ref[...] + acc


def outproj(o_gla, o_na, an, w_o, h, *, tm=512):
    n = h.shape[0]
    est = 2 * 2 * tm * 1024 * 2 + 2 * D_MODEL * D_MODEL * 2 + 4 * tm * D_MODEL * 4 + tm * D_MODEL * 4
    return pl.pallas_call(
        _outproj_kernel,
        grid=(n // tm,),
        in_specs=[
            pl.BlockSpec((tm, GLA_VW), lambda i: (i, 0)),
            pl.BlockSpec((tm, NA_W), lambda i: (i, 0)),
            pl.BlockSpec((1, NA_W), lambda i: (0, 0)),
            pl.BlockSpec((D_MODEL, D_MODEL), lambda i: (0, 0)),
            pl.BlockSpec((tm, D_MODEL), lambda i: (i, 0)),
        ],
        out_specs=pl.BlockSpec((tm, D_MODEL), lambda i: (i, 0)),
        out_shape=jax.ShapeDtypeStruct((n, D_MODEL), F32),
        compiler_params=pltpu.CompilerParams(
            dimension_semantics=("parallel",), vmem_limit_bytes=_vmem(est)),
        name="outproj",
    )(o_gla, o_na, an, w_o, h)


def _ffn_kernel(h_ref, g_ref, wg_ref, wu_ref, wd_ref, o_ref, u_scr, acc_scr):
    f = pl.program_id(1)

    @pl.when(f == 0)
    def _():
        u_scr[...] = _rms(h_ref[...], g_ref[...]).astype(BF16)
        acc_scr[...] = jnp.zeros_like(acc_scr)

    u = u_scr[...]
    a = jnp.dot(u, wg_ref[...], preferred_element_type=F32)
    b = jnp.dot(u, wu_ref[...], preferred_element_type=F32)
    hid = (_silu(a) * b).astype(BF16)
    acc_scr[...] += jnp.dot(hid, wd_ref[...], preferred_element_type=F32)

    @pl.when(f == pl.num_programs(1) - 1)
    def _():
        o_ref[...] = h_ref[...] + acc_scr[...]


def ffn_dense(h, g, wg, wu, wd, *, tm=512, tf=512):
    n = h.shape[0]
    est = 4 * tm * D_MODEL * 4 + tm * D_MODEL * 6 + 2 * 3 * D_MODEL * tf * 2 + 3 * tm * tf * 4
    return pl.pallas_call(
        _ffn_kernel,
        grid=(n // tm, FF_DIM // tf),
        in_specs=[
            pl.BlockSpec((tm, D_MODEL), lambda i, f: (i, 0)),
            pl.BlockSpec((1, D_MODEL), lambda i, f: (0, 0)),
            pl.BlockSpec((D_MODEL, tf), lambda i, f: (0, f)),
            pl.BlockSpec((D_MODEL, tf), lambda i, f: (0, f)),
            pl.BlockSpec((tf, D_MODEL), lambda i, f: (f, 0)),
        ],
        out_specs=pl.BlockSpec((tm, D_MODEL), lambda i, f: (i, 0)),
        out_shape=jax.ShapeDtypeStruct((n, D_MODEL), F32),
        scratch_shapes=[pltpu.VMEM((tm, D_MODEL), BF16), pltpu.VMEM((tm, D_MODEL), F32)],
        compiler_params=pltpu.CompilerParams(
            dimension_semantics=("parallel", "arbitrary"), vmem_limit_bytes=_vmem(est)),
        name="ffn_dense",
    )(h, g, wg, wu, wd)


def _moe_kernel(h_ref, g_ref, wr_ref, wg_ref, wu_ref, wd_ref, o_ref, u_scr, acc_scr, gate_scr, *, steps_per_expert):
    f = pl.program_id(1)
    tm = u_scr.shape[0]
    lane = lax.broadcasted_iota(jnp.int32, (tm, LANES), 1)

    @pl.when(f == 0)
    def _():
        u = _rms(h_ref[...], g_ref[...]).astype(BF16)
        u_scr[...] = u
        acc_scr[...] = jnp.zeros_like(acc_scr)
        logits = jnp.dot(u, wr_ref[...], preferred_element_type=F32)
        lg = jnp.where(lane < N_EXP, logits, -jnp.inf)
        v1 = jnp.max(lg, axis=-1, keepdims=True)
        i1 = jnp.min(jnp.where(lg == v1, lane, LANES), axis=-1, keepdims=True)
        lg2 = jnp.where(lane == i1, -jnp.inf, lg)
        v2 = jnp.max(lg2, axis=-1, keepdims=True)
        i2 = jnp.min(jnp.where(lg2 == v2, lane, LANES), axis=-1, keepdims=True)
        e2 = jnp.exp(v2 - v1)
        w1 = 1.0 / (1.0 + e2)
        w2 = e2 / (1.0 + e2)
        gate_scr[...] = jnp.where(lane == i1, w1, 0.0) + jnp.where(lane == i2, w2, 0.0)

    e = f // steps_per_expert
    gcol = jnp.sum(jnp.where(lane == e, gate_scr[...], 0.0), axis=-1, keepdims=True)
    u = u_scr[...]
    a = jnp.dot(u, wg_ref[...], preferred_element_type=F32)
    b = jnp.dot(u, wu_ref[...], preferred_element_type=F32)
    hid = (_silu(a) * b * gcol).astype(BF16)
    acc_scr[...] += jnp.dot(hid, wd_ref[...], preferred_element_type=F32)

    @pl.when(f == pl.num_programs(1) - 1)
    def _():
        o_ref[...] = h_ref[...] + acc_scr[...]


def ffn_moe(h, g, wr, wg, wu, wd, *, tm=512, tf=512):
    n = h.shape[0]
    spe = EXP_DIM // tf
    est = 4 * tm * D_MODEL * 4 + tm * D_MODEL * 6 + 2 * 3 * D_MODEL * tf * 2 + 3 * tm * tf * 4 + D_MODEL * LANES * 4
    return pl.pallas_call(
        functools.partial(_moe_kernel, steps_per_expert=spe),
        grid=(n // tm, N_EXP * spe),
        in_specs=[
            pl.BlockSpec((tm, D_MODEL), lambda i, f: (i, 0)),
            pl.BlockSpec((1, D_MODEL), lambda i, f: (0, 0)),
            pl.BlockSpec((D_MODEL, LANES), lambda i, f: (0, 0)),
            pl.BlockSpec((None, D_MODEL, tf), lambda i, f: (f // spe, 0, f % spe)),
            pl.BlockSpec((None, D_MODEL, tf), lambda i, f: (f // spe, 0, f % spe)),
            pl.BlockSpec((None, tf, D_MODEL), lambda i, f: (f // spe, f % spe, 0)),
        ],
        out_specs=pl.BlockSpec((tm, D_MODEL), lambda i, f: (i, 0)),
        out_shape=jax.ShapeDtypeStruct((n, D_MODEL), F32),
        scratch_shapes=[pltpu.VMEM((tm, D_MODEL), BF16), pltpu.VMEM((tm, D_MODEL), F32),
                        pltpu.VMEM((tm, LANES), F32)],
        compiler_params=pltpu.CompilerParams(
            dimension_semantics=("parallel", "arbitrary"), vmem_limit_bytes=_vmem(est)),
        name="ffn_moe",
    )(h, g, wr, wg, wu, wd)


def _ple_kernel(h_ref, p_ref, gi_ref, wgate_ref, wp_ref, go_ref, o_ref):
    h = h_ref[...]
    hn = _rms(h, gi_ref[...]).astype(BF16)
    gate = jax.nn.sigmoid(jnp.dot(hn, wgate_ref[...], preferred_element_type=F32))
    e = jnp.dot(p_ref[...].astype(BF16), wp_ref[...], preferred_element_type=F32)
    o_ref[...] = h + gate * _rms(e, go_ref[...])


def ple(h, p_all, layer, gi, wgate, wp, go, *, tm=512):
    n = h.shape[0]
    est = 4 * tm * D_MODEL * 4 + 2 * D_MODEL * D_MODEL * 2 + 2 * PLE_DIM * D_MODEL * 2 + 3 * tm * D_MODEL * 4
    return pl.pallas_call(
        _ple_kernel,
        grid=(n // tm,),
        in_specs=[
            pl.BlockSpec((tm, D_MODEL), lambda i: (i, 0)),
            pl.BlockSpec((None, tm, PLE_DIM), lambda i: (layer, i, 0)),
            pl.BlockSpec((1, D_MODEL), lambda i: (0, 0)),
            pl.BlockSpec((D_MODEL, D_MODEL), lambda i: (0, 0)),
            pl.BlockSpec((PLE_DIM, D_MODEL), lambda i: (0, 0)),
            pl.BlockSpec((1, D_MODEL), lambda i: (0, 0)),
        ],
        out_specs=pl.BlockSpec((tm, D_MODEL), lambda i: (i, 0)),
        out_shape=jax.ShapeDtypeStruct((n, D_MODEL), F32),
        compiler_params=pltpu.CompilerParams(
            dimension_semantics=("parallel",), vmem_limit_bytes=_vmem(est)),
        name="ple",
    )(h, p_all, gi, wgate, wp, go)


def _prep_layer(i, P):
    w_in = P["w_in"][i]
    lr0 = Z_AQ
    w_in_r = jnp.concatenate(
        [w_in[:, :lr0], w_in[:, lr0 + 2 * GLA_RANK:], w_in[:, lr0:lr0 + 2 * GLA_RANK],
         jnp.zeros((D_MODEL, LR_W - 2 * GLA_RANK), w_in.dtype)], axis=1).astype(BF16)
    zpad = jnp.zeros((LR_W, GLA_KW), F32)
    wgf = zpad.at[0:GLA_RANK].set(P["w_gate_f"][i]).astype(BF16)
    wgb = zpad.at[GLA_RANK:2 * GLA_RANK].set(P["w_gate_b"][i]).astype(BF16)
    L = dict(
        g_mix=P["g_mix"][i][None], w_in=w_in_r, wgf=wgf, wgb=wgb,
        bgf=P["b_gate_f"][i][None], bgb=P["b_gate_b"][i][None],
        gla_norm=P["gla_norm"][i].reshape(GLA_H, 1, GLA_DV),
        q_norm=P["q_norm"][i][None], k_norm=P["k_norm"][i][None],
        bias_tbl=natten_bias_table(P["rpb"][i]), attn_norm=P["attn_norm"][i][None],
        w_o=P["w_o"][i].astype(BF16), g_ffn=P["g_ffn"][i][None],
        g_ple_in=P["g_ple_in"][i][None], w_ple_gate=P["w_ple_gate"][i].astype(BF16),
        w_ple=P["w_ple"][i].astype(BF16), g_ple_out=P["g_ple_out"][i][None],
    )
    j = i // 2
    if i % 2 == 0:
        L.update(wg=P["w_ff_gate"][j].astype(BF16), wu=P["w_ff_up"][j].astype(BF16),
                 wd=P["w_ff_down"][j].astype(BF16))
    else:
        wr = jnp.zeros((D_MODEL, LANES), F32).at[:, :N_EXP].set(P["w_router"][j]).astype(BF16)
        L.update(wr=wr, wg=P["w_exp_gate"][j].astype(BF16), wu=P["w_exp_up"][j].astype(BF16),
                 wd=P["w_exp_down"][j].astype(BF16))
    return L


def _trunk(x, p, layers):
    B, T, _ = x.shape
    n = B * T
    h = x.reshape(n, D_MODEL)
    p_all = p.reshape(DEPTH, n, PLE_DIM)
    for i, L in enumerate(layers):
        z = inproj(h, L["g_mix"], L["w_in"])
        z3 = z.reshape(B, T, NZ)
        o_f = gla_direction(z3, L["wgf"], L["bgf"], reverse=False)
        o_gla = gla_direction(z3, L["wgb"], L["bgb"], reverse=True, o_other=o_f, gn=L["gla_norm"])
        o_na = natten(z3, L["bias_tbl"], L["q_norm"], L["k_norm"])
        h = outproj(o_gla.reshape(n, GLA_VW), o_na.reshape(n, NA_W), L["attn_norm"], L["w_o"], h)
        if i % 2 == 0:
            h = ffn_dense(h, L["g_ffn"], L["wg"], L["wu"], L["wd"])
        else:
            h = ffn_moe(h, L["g_ffn"], L["wr"], L["wg"], L["wu"], L["wd"])
        h = ple(h, p_all, i, L["g_ple_in"], L["w_ple_gate"], L["w_ple"], L["g_ple_out"])
    return h.reshape(B, T, D_MODEL)


def kernel(x_prompt, x_sample, p_prompt, p_sample, g_mix, w_in, w_gate_f, b_gate_f, w_gate_b, b_gate_b, gla_norm, q_norm, k_norm, rpb, attn_norm, w_o, g_ffn, w_ff_gate, w_ff_up, w_ff_down, w_router, w_exp_gate, w_exp_up, w_exp_down, g_ple_in, w_ple_gate, w_ple, g_ple_out):
    P = dict(g_mix=g_mix, w_in=w_in, w_gate_f=w_gate_f, b_gate_f=b_gate_f, w_gate_b=w_gate_b,
             b_gate_b=b_gate_b, gla_norm=gla_norm, q_norm=q_norm, k_norm=k_norm, rpb=rpb,
             attn_norm=attn_norm, w_o=w_o, g_ffn=g_ffn, w_ff_gate=w_ff_gate, w_ff_up=w_ff_up,
             w_ff_down=w_ff_down, w_router=w_router, w_exp_gate=w_exp_gate, w_exp_up=w_exp_up,
             w_exp_down=w_exp_down, g_ple_in=g_ple_in, w_ple_gate=w_ple_gate, w_ple=w_ple,
             g_ple_out=g_ple_out)
    layers = [_prep_layer(i, P) for i in range(DEPTH)]
    return (_trunk(x_prompt, p_prompt, layers), _trunk(x_sample, p_sample, layers))
```

```python
import functools

import jax
import jax.numpy as jnp
from jax import lax
from jax.experimental import pallas as pl
from jax.experimental.pallas import tpu as pltpu

F32 = jnp.float32
BF16 = jnp.bfloat16

D_MODEL = 2048
DEPTH = 4
GRID_W = 64
GLA_H = 4
GLA_VW = 1024
GLA_KW = 512
GLA_DK = 128
GLA_DV = 256
GLA_RANK = 16
GLA_INV_NORMALIZER = 1.0 / 16.0
GLA_CHUNK = 64
NA_W = 1024
NA_H = 8
NA_HD = 128
WIN_ROWS = 8
WIN_COLS = 16
FF_DIM = 4096
N_EXP = 8
EXP_DIM = 1024
PLE_DIM = 256
EPS = 1e-6
NEG_INF = -1e30

Z_GQ, Z_GK, Z_GV, Z_GG = 0, 512, 1024, 2048
Z_AQ, Z_AK, Z_AV, Z_LR = 3072, 4096, 5120, 6144
LR_W = 256
NZ = Z_LR + LR_W

V7X_VMEM_BYTES = 64 * 1024 * 1024
LANES = 128


def _vmem(nbytes):
    return int(min(nbytes * 3 // 2 + (4 << 20), V7X_VMEM_BYTES - (6 << 20)))


def _rms(x, g):
    ms = jnp.mean(x * x, axis=-1, keepdims=True)
    return x * lax.rsqrt(ms + EPS) * g


def _silu(x):
    return x * jax.nn.sigmoid(x)


def _inproj_kernel(h_ref, g_ref, w_ref, z_ref, u_scr):
    @pl.when(pl.program_id(1) == 0)
    def _():
        u_scr[...] = _rms(h_ref[...], g_ref[...]).astype(BF16)

    z_ref[...] = jnp.dot(u_scr[...], w_ref[...], preferred_element_type=F32).astype(z_ref.dtype)


def inproj(h, g, w, *, tm=1024, tn=1280):
    n = h.shape[0]
    est = 2 * tm * D_MODEL * 4 + 2 * D_MODEL * tn * 2 + 2 * tm * tn * 2 + tm * D_MODEL * 2 + tm * tn * 4
    return pl.pallas_call(
        _inproj_kernel,
        grid=(n // tm, NZ // tn),
        in_specs=[
            pl.BlockSpec((tm, D_MODEL), lambda i, j: (i, 0)),
            pl.BlockSpec((1, D_MODEL), lambda i, j: (0, 0)),
            pl.BlockSpec((D_MODEL, tn), lambda i, j: (0, j)),
        ],
        out_specs=pl.BlockSpec((tm, tn), lambda i, j: (i, j)),
        out_shape=jax.ShapeDtypeStruct((n, NZ), BF16),
        scratch_shapes=[pltpu.VMEM((tm, D_MODEL), BF16)],
        compiler_params=pltpu.CompilerParams(
            dimension_semantics=("parallel", "arbitrary"), vmem_limit_bytes=_vmem(est)),
        name="inproj",
    )(h, g, w)


def _cumsum_rows(x, seg):
    idx = lax.broadcasted_iota(jnp.int32, x.shape, 0) & (seg - 1)
    s = 1
    while s < seg:
        x = x + jnp.where(idx >= s, pltpu.roll(x, s, 0), 0.0)
        s *= 2
    return x


def _log_sigmoid(x):
    return jnp.minimum(x, 0.0) - jnp.log1p(jnp.exp(-jnp.abs(x)))


def _gla_kernel(*refs, reverse, final, tb):
    if final:
        q_ref, k_ref, v_ref, lr_ref, wg_ref, bg_ref, of_ref, gg_ref, gn_ref, o_ref, s_scr = refs
    else:
        q_ref, k_ref, v_ref, lr_ref, wg_ref, bg_ref, o_ref, s_scr = refs
    C = GLA_CHUNK
    nchunk = tb // C

    @pl.when(pl.program_id(2) == 0)
    def _():
        s_scr[...] = jnp.zeros_like(s_scr)

    x = jnp.dot(lr_ref[...], wg_ref[...], preferred_element_type=F32) + bg_ref[...]
    la_blk = _log_sigmoid(x) * GLA_INV_NORMALIZER
    b_blk = _cumsum_rows(la_blk, C)

    row = lax.broadcasted_iota(jnp.int32, (C, C), 0)
    col = lax.broadcasted_iota(jnp.int32, (C, C), 1)
    mask = (col >= row) if reverse else (col <= row)
    scale = GLA_DK ** -0.5

    order = [(nchunk - 1 - ci) if reverse else ci for ci in range(nchunk)]
    sl = [slice(c * C, (c + 1) * C) for c in range(nchunk)]
    qts, atts, incs, dls = {}, {}, {}, {}
    for c in order:
        b = b_blk[sl[c]]
        blast = b[C - 1:C, :]
        if reverse:
            b = blast - b + la_blk[sl[c]]
        q = q_ref[sl[c], :].astype(F32)
        k = k_ref[sl[c], :].astype(F32)
        qts[c] = (q * jnp.exp(b) * scale).astype(BF16)
        kt = (k * jnp.exp(-b)).astype(BF16)
        ke_t = (k * jnp.exp(blast - b)).T.astype(BF16)
        dls[c] = jnp.broadcast_to(jnp.exp(blast), (8, GLA_DK)).T[:, 0:1]
        att = lax.dot_general(qts[c], kt, (((1,), (1,)), ((), ())), preferred_element_type=F32)
        atts[c] = jnp.where(mask, att, 0.0).astype(BF16)
        incs[c] = jnp.dot(ke_t, v_ref[sl[c], :], preferred_element_type=F32)
    state = s_scr[...]
    states = {}
    for c in order:
        states[c] = state.astype(BF16)
        state = dls[c] * state + incs[c]
    s_scr[...] = state
    for c in order:
        o = (jnp.dot(atts[c], v_ref[sl[c], :], preferred_element_type=F32)
             + jnp.dot(qts[c], states[c], preferred_element_type=F32))
        if final:
            y = _rms(o + of_ref[sl[c], :], gn_ref[...])
            gg = gg_ref[sl[c], :].astype(F32)
            o_ref[sl[c], :] = (y * _silu(gg)).astype(o_ref.dtype)
        else:
            o_ref[sl[c], :] = o


def gla_direction(z3, wg, bg, *, reverse, o_other=None, gn=None, tb=1024):
    B, T, _ = z3.shape
    nb = T // tb
    final = o_other is not None
    blk = (lambda j: nb - 1 - j) if reverse else (lambda j: j)
    qw, vw = GLA_DK, GLA_DV
    in_specs = [
        pl.BlockSpec((None, tb, qw), lambda b, h, j: (b, blk(j), Z_GQ // qw + h)),
        pl.BlockSpec((None, tb, qw), lambda b, h, j: (b, blk(j), Z_GK // qw + h)),
        pl.BlockSpec((None, tb, vw), lambda b, h, j: (b, blk(j), Z_GV // vw + h)),
        pl.BlockSpec((None, tb, LR_W), lambda b, h, j: (b, blk(j), Z_LR // LR_W)),
        pl.BlockSpec((LR_W, qw), lambda b, h, j: (0, h)),
        pl.BlockSpec((1, qw), lambda b, h, j: (0, h)),
    ]
    args = [z3, z3, z3, z3, wg, bg]
    if final:
        in_specs += [
            pl.BlockSpec((None, tb, vw), lambda b, h, j: (b, blk(j), h)),
            pl.BlockSpec((None, tb, vw), lambda b, h, j: (b, blk(j), Z_GG // vw + h)),
            pl.BlockSpec((None, 1, vw), lambda b, h, j: (h, 0, 0)),
        ]
        args += [o_other, z3, gn]
    out_dtype = BF16 if final else F32
    return pl.pallas_call(
        functools.partial(_gla_kernel, reverse=reverse, final=final, tb=tb),
        grid=(B, GLA_H, nb),
        in_specs=in_specs,
        out_specs=pl.BlockSpec((None, tb, vw), lambda b, h, j: (b, blk(j), h)),
        out_shape=jax.ShapeDtypeStruct((B, T, GLA_VW), out_dtype),
        scratch_shapes=[pltpu.VMEM((GLA_DK, GLA_DV), F32)],
        compiler_params=pltpu.CompilerParams(
            dimension_semantics=("parallel", "parallel", "arbitrary"), vmem_limit_bytes=_vmem(8 << 20)),
        name="gla_bwd" if reverse else "gla_fwd",
    )(*args)


def _natten_kernel(q_ref, k_ref, v_ref, bias_ref, qg_ref, kg_ref, o_ref, kn_scr, *, rows, rq):
    T = rows * GRID_W
    KB = 512
    WK = WIN_ROWS * GRID_W
    j = pl.program_id(2)

    @pl.when(j == 0)
    def _():
        def norm_block(i, carry):
            off = pl.multiple_of(i * KB, KB)
            kk = k_ref[pl.ds(off, KB), :].astype(F32)
            kn_scr[pl.ds(off, KB), :] = _rms(kk, kg_ref[...]).astype(BF16)
            return carry

        lax.fori_loop(0, T // KB, norm_block, 0)

    qn_blk = (_rms(q_ref[...].astype(F32), qg_ref[...]) * (NA_HD ** -0.5)).astype(BF16)
    koffs, dtis, scores = [], [], []
    for i in range(rq):
        r = j * rq + i
        rs = jnp.clip(r - WIN_ROWS // 2, 0, rows - WIN_ROWS)
        dtis.append(rs - r + (WIN_ROWS - 1))
        koffs.append(pl.multiple_of(rs * GRID_W, GRID_W))
        kw = kn_scr[pl.ds(koffs[i], WK), :]
        scores.append(lax.dot_general(qn_blk[i * GRID_W:(i + 1) * GRID_W], kw, (((1,), (1,)), ((), ())),
                                      preferred_element_type=F32))
    probs, sums = [], []
    for i in range(rq):
        s = scores[i] + bias_ref[dtis[i]]
        p = jnp.exp(s - jnp.max(s, axis=-1, keepdims=True))
        sums.append(jnp.sum(p, axis=-1, keepdims=True))
        probs.append(p.astype(BF16))
    for i in range(rq):
        vw = v_ref[pl.ds(koffs[i], WK), :]
        o = jnp.dot(probs[i], vw, preferred_element_type=F32) / sums[i]
        o_ref[i * GRID_W:(i + 1) * GRID_W, :] = o.astype(o_ref.dtype)


def natten(z3, bias_tbl, qg, kg, *, rq=32):
    B, T, _ = z3.shape
    rows = T // GRID_W
    assert rows >= WIN_ROWS and rows % rq == 0
    tq = rq * GRID_W
    hd = NA_HD
    est = 2 * 2 * T * hd * 2 + T * hd * 2 + 2 * WIN_ROWS * GRID_W * WIN_ROWS * GRID_W * 4 + 4 * tq * hd * 2
    return pl.pallas_call(
        functools.partial(_natten_kernel, rows=rows, rq=rq),
        grid=(B, NA_H, rows // rq),
        in_specs=[
            pl.BlockSpec((None, tq, hd), lambda b, h, j: (b, j, Z_AQ // hd + h)),
            pl.BlockSpec((None, T, hd), lambda b, h, j: (b, 0, Z_AK // hd + h)),
            pl.BlockSpec((None, T, hd), lambda b, h, j: (b, 0, Z_AV // hd + h)),
            pl.BlockSpec((None, WIN_ROWS, GRID_W, WIN_ROWS * GRID_W), lambda b, h, j: (h, 0, 0, 0)),
            pl.BlockSpec((1, hd), lambda b, h, j: (0, 0)),
            pl.BlockSpec((1, hd), lambda b, h, j: (0, 0)),
        ],
        out_specs=pl.BlockSpec((None, tq, hd), lambda b, h, j: (b, j, h)),
        out_shape=jax.ShapeDtypeStruct((B, T, NA_W), BF16),
        scratch_shapes=[pltpu.VMEM((T, hd), BF16)],
        compiler_params=pltpu.CompilerParams(
            dimension_semantics=("parallel", "parallel", "arbitrary"), vmem_limit_bytes=_vmem(est)),
        name="natten",
    )(z3, z3, z3, bias_tbl, qg, kg)


def natten_bias_table(rpb):
    c = jnp.arange(GRID_W, dtype=jnp.int32)
    c_start = jnp.clip(c - WIN_COLS // 2, 0, GRID_W - WIN_COLS)
    valid = (c[None, :] >= c_start[:, None]) & (c[None, :] < c_start[:, None] + WIN_COLS)
    dc = jnp.clip(c[None, :] - c[:, None], -(WIN_COLS - 1), WIN_COLS - 1) + (WIN_COLS - 1)
    bias_c = jnp.where(valid, rpb.astype(F32)[:, :, dc], NEG_INF)
    dr = jnp.arange(WIN_ROWS)[:, None] + jnp.arange(WIN_ROWS)[None, :]
    tbl = bias_c[:, dr]
    return tbl.transpose(0, 1, 3, 2, 4).reshape(NA_H, WIN_ROWS, GRID_W, WIN_ROWS * GRID_W)


def _outproj_kernel(og_ref, on_ref, an_ref, w_ref, h_ref, o_ref):
    na = _rms(on_ref[...].astype(F32), an_ref[...]).astype(BF16)
    acc = jnp.dot(og_ref[...], w_ref[0:GLA_VW, :], preferred_element_type=F32)
    acc += jnp.dot(na, w_ref[GLA_VW:GLA_VW + NA_W, :], preferred_element_type=F32)
    o_ref[...] = h_ref[...] + acc


def outproj(o_gla, o_na, an, w_o, h, *, tm=512):
    n = h.shape[0]
    est = 2 * 2 * tm * 1024 * 2 + 2 * D_MODEL * D_MODEL * 2 + 4 * tm * D_MODEL * 4 + tm * D_MODEL * 4
    return pl.pallas_call(
        _outproj_kernel,
        grid=(n // tm,),
        in_specs=[
            pl.BlockSpec((tm, GLA_VW), lambda i: (i, 0)),
            pl.BlockSpec((tm, NA_W), lambda i: (i, 0)),
            pl.BlockSpec((1, NA_W), lambda i: (0, 0)),
            pl.BlockSpec((D_MODEL, D_MODEL), lambda i: (0, 0)),
            pl.BlockSpec((tm, D_MODEL), lambda i: (i, 0)),
        ],
        out_specs=pl.BlockSpec((tm, D_MODEL), lambda i: (i, 0)),
        out_shape=jax.ShapeDtypeStruct((n, D_MODEL), F32),
        compiler_params=pltpu.CompilerParams(
            dimension_semantics=("parallel",), vmem_limit_bytes=_vmem(est)),
        name="outproj",
    )(o_gla, o_na, an, w_o, h)


def _ffn_kernel(h_ref, g_ref, wg_ref, wu_ref, wd_ref, o_ref, u_scr):
    @pl.when(pl.program_id(1) == 0)
    def _():
        h = h_ref[...]
        u_scr[...] = _rms(h, g_ref[...]).astype(BF16)
        o_ref[...] = h

    u = u_scr[...]
    a = jnp.dot(u, wg_ref[...], preferred_element_type=F32)
    b = jnp.dot(u, wu_ref[...], preferred_element_type=F32)
    hid = (_silu(a) * b).astype(BF16)
    o_ref[...] += jnp.dot(hid, wd_ref[...], preferred_element_type=F32)


def ffn_dense(h, g, wg, wu, wd, *, tm=1024, tf=256):
    n = h.shape[0]
    est = 4 * tm * D_MODEL * 4 + tm * D_MODEL * 2 + 2 * 3 * D_MODEL * tf * 2 + 3 * tm * tf * 4
    return pl.pallas_call(
        _ffn_kernel,
        grid=(n // tm, FF_DIM // tf),
        in_specs=[
            pl.BlockSpec((tm, D_MODEL), lambda i, f: (i, 0)),
            pl.BlockSpec((1, D_MODEL), lambda i, f: (0, 0)),
            pl.BlockSpec((D_MODEL, tf), lambda i, f: (0, f)),
            pl.BlockSpec((D_MODEL, tf), lambda i, f: (0, f)),
            pl.BlockSpec((tf, D_MODEL), lambda i, f: (f, 0)),
        ],
        out_specs=pl.BlockSpec((tm, D_MODEL), lambda i, f: (i, 0)),
        out_shape=jax.ShapeDtypeStruct((n, D_MODEL), F32),
        scratch_shapes=[pltpu.VMEM((tm, D_MODEL), BF16)],
        compiler_params=pltpu.CompilerParams(
            dimension_semantics=("parallel", "arbitrary"), vmem_limit_bytes=_vmem(est)),
        name="ffn_dense",
    )(h, g, wg, wu, wd)


def _moe_kernel(h_ref, g_ref, wr_ref, wg_ref, wu_ref, wd_ref, o_ref, u_scr, acc_scr, gate_scr, *, steps_per_expert):
    f = pl.program_id(1)
    tm = u_scr.shape[0]
    lane = lax.broadcasted_iota(jnp.int32, (tm, LANES), 1)

    @pl.when(f == 0)
    def _():
        u = _rms(h_ref[...], g_ref[...]).astype(BF16)
        u_scr[...] = u
        acc_scr[...] = jnp.zeros_like(acc_scr)
        logits = jnp.dot(u, wr_ref[...], preferred_element_type=F32)
        lg = jnp.where(lane < N_EXP, logits, -jnp.inf)
        v1 = jnp.max(lg, axis=-1, keepdims=True)
        i1 = jnp.min(jnp.where(lg == v1, lane, LANES), axis=-1, keepdims=True)
        lg2 = jnp.where(lane == i1, -jnp.inf, lg)
        v2 = jnp.max(lg2, axis=-1, keepdims=True)
        i2 = jnp.min(jnp.where(lg2 == v2, lane, LANES), axis=-1, keepdims=True)
        e2 = jnp.exp(v2 - v1)
        w1 = 1.0 / (1.0 + e2)
        w2 = e2 / (1.0 + e2)
        gate_scr[...] = jnp.where(lane == i1, w1, 0.0) + jnp.where(lane == i2, w2, 0.0)

    e = f // steps_per_expert
    gcol = jnp.sum(jnp.where(lane == e, gate_scr[...], 0.0), axis=-1, keepdims=True)
    u = u_scr[...]
    a = jnp.dot(u, wg_ref[...], preferred_element_type=F32)
    b = jnp.dot(u, wu_ref[...], preferred_element_type=F32)
    hid = (_silu(a) * b * gcol).astype(BF16)
    acc_scr[...] += jnp.dot(hid, wd_ref[...], preferred_element_type=F32)

    @pl.when(f == pl.num_programs(1) - 1)
    def _():
        o_ref[...] = h_ref[...] + acc_scr[...]


def ffn_moe(h, g, wr, wg, wu, wd, *, tm=512, tf=512):
    n = h.shape[0]
    spe = EXP_DIM // tf
    est = 4 * tm * D_MODEL * 4 + tm * D_MODEL * 6 + 2 * 3 * D_MODEL * tf * 2 + 3 * tm * tf * 4 + D_MODEL * LANES * 4
    return pl.pallas_call(
        functools.partial(_moe_kernel, steps_per_expert=spe),
        grid=(n // tm, N_EXP * spe),
        in_specs=[
            pl.BlockSpec((tm, D_MODEL), lambda i, f: (i, 0)),
            pl.BlockSpec((1, D_MODEL), lambda i, f: (0, 0)),
            pl.BlockSpec((D_MODEL, LANES), lambda i, f: (0, 0)),
            pl.BlockSpec((None, D_MODEL, tf), lambda i, f: (f // spe, 0, f % spe)),
            pl.BlockSpec((None, D_MODEL, tf), lambda i, f: (f // spe, 0, f % spe)),
            pl.BlockSpec((None, tf, D_MODEL), lambda i, f: (f // spe, f % spe, 0)),
        ],
        out_specs=pl.BlockSpec((tm, D_MODEL), lambda i, f: (i, 0)),
        out_shape=jax.ShapeDtypeStruct((n, D_MODEL), F32),
        scratch_shapes=[pltpu.VMEM((tm, D_MODEL), BF16), pltpu.VMEM((tm, D_MODEL), F32),
                        pltpu.VMEM((tm, LANES), F32)],
        compiler_params=pltpu.CompilerParams(
            dimension_semantics=("parallel", "arbitrary"), vmem_limit_bytes=_vmem(est)),
        name="ffn_moe",
    )(h, g, wr, wg, wu, wd)


def _ple_kernel(h_ref, p_ref, gi_ref, wgate_ref, wp_ref, go_ref, o_ref):
    h = h_ref[...]
    hn = _rms(h, gi_ref[...]).astype(BF16)
    gate = jax.nn.sigmoid(jnp.dot(hn, wgate_ref[...], preferred_element_type=F32))
    e = jnp.dot(p_ref[...].astype(BF16), wp_ref[...], preferred_element_type=F32)
    o_ref[...] = h + gate * _rms(e, go_ref[...])


def ple(h, p_all, layer, gi, wgate, wp, go, *, tm=512):
    n = h.shape[0]
    est = 4 * tm * D_MODEL * 4 + 2 * D_MODEL * D_MODEL * 2 + 2 * PLE_DIM * D_MODEL * 2 + 3 * tm * D_MODEL * 4
    return pl.pallas_call(
        _ple_kernel,
        grid=(n // tm,),
        in_specs=[
            pl.BlockSpec((tm, D_MODEL), lambda i: (i, 0)),
            pl.BlockSpec((None, tm, PLE_DIM), lambda i: (layer, i, 0)),
            pl.BlockSpec((1, D_MODEL), lambda i: (0, 0)),
            pl.BlockSpec((D_MODEL, D_MODEL), lambda i: (0, 0)),
            pl.BlockSpec((PLE_DIM, D_MODEL), lambda i: (0, 0)),
            pl.BlockSpec((1, D_MODEL), lambda i: (0, 0)),
        ],
        out_specs=pl.BlockSpec((tm, D_MODEL), lambda i: (i, 0)),
        out_shape=jax.ShapeDtypeStruct((n, D_MODEL), F32),
        compiler_params=pltpu.CompilerParams(
            dimension_semantics=("parallel",), vmem_limit_bytes=_vmem(est)),
        name="ple",
    )(h, p_all, gi, wgate, wp, go)


def _prep_layer(i, P):
    w_in = P["w_in"][i]
    lr0 = Z_AQ
    w_in_r = jnp.concatenate(
        [w_in[:, :lr0], w_in[:, lr0 + 2 * GLA_RANK:], w_in[:, lr0:lr0 + 2 * GLA_RANK],
         jnp.zeros((D_MODEL, LR_W - 2 * GLA_RANK), w_in.dtype)], axis=1).astype(BF16)
    zpad = jnp.zeros((LR_W, GLA_KW), F32)
    wgf = zpad.at[0:GLA_RANK].set(P["w_gate_f"][i]).astype(BF16)
    wgb = zpad.at[GLA_RANK:2 * GLA_RANK].set(P["w_gate_b"][i]).astype(BF16)
    L = dict(
        g_mix=P["g_mix"][i][None], w_in=w_in_r, wgf=wgf, wgb=wgb,
        bgf=P["b_gate_f"][i][None], bgb=P["b_gate_b"][i][None],
        gla_norm=P["gla_norm"][i].reshape(GLA_H, 1, GLA_DV),
        q_norm=P["q_norm"][i][None], k_norm=P["k_norm"][i][None],
        bias_tbl=natten_bias_table(P["rpb"][i]), attn_norm=P["attn_norm"][i][None],
        w_o=P["w_o"][i].astype(BF16), g_ffn=P["g_ffn"][i][None],
        g_ple_in=P["g_ple_in"][i][None], w_ple_gate=P["w_ple_gate"][i].astype(BF16),
        w_ple=P["w_ple"][i].astype(BF16), g_ple_out=P["g_ple_out"][i][None],
    )
    j = i // 2
    if i % 2 == 0:
        L.update(wg=P["w_ff_gate"][j].astype(BF16), wu=P["w_ff_up"][j].astype(BF16),
                 wd=P["w_ff_down"][j].astype(BF16))
    else:
        wr = jnp.zeros((D_MODEL, LANES), F32).at[:, :N_EXP].set(P["w_router"][j]).astype(BF16)
        L.update(wr=wr, wg=P["w_exp_gate"][j].astype(BF16), wu=P["w_exp_up"][j].astype(BF16),
                 wd=P["w_exp_down"][j].astype(BF16))
    return L


def _trunk(x, p, layers):
    B, T, _ = x.shape
    n = B * T
    h = x.reshape(n, D_MODEL)
    p_all = p.reshape(DEPTH, n, PLE_DIM)
    for i, L in enumerate(layers):
        z = inproj(h, L["g_mix"], L["w_in"])
        z3 = z.reshape(B, T, NZ)
        o_f = gla_direction(z3, L["wgf"], L["bgf"], reverse=False)
        o_gla = gla_direction(z3, L["wgb"], L["bgb"], reverse=True, o_other=o_f, gn=L["gla_norm"])
        o_na = natten(z3, L["bias_tbl"], L["q_norm"], L["k_norm"])
        h = outproj(o_gla.reshape(n, GLA_VW), o_na.reshape(n, NA_W), L["attn_norm"], L["w_o"], h)
        if i % 2 == 0:
            h = ffn_dense(h, L["g_ffn"], L["wg"], L["wu"], L["wd"])
        else:
            h = ffn_moe(h, L["g_ffn"], L["wr"], L["wg"], L["wu"], L["wd"])
        h = ple(h, p_all, i, L["g_ple_in"], L["w_ple_gate"], L["w_ple"], L["g_ple_out"])
    return h.reshape(B, T, D_MODEL)


def kernel(x_prompt, x_sample, p_prompt, p_sample, g_mix, w_in, w_gate_f, b_gate_f, w_gate_b, b_gate_b, gla_norm, q_norm, k_norm, rpb, attn_norm, w_o, g_ffn, w_ff_gate, w_ff_up, w_ff_down, w_router, w_exp_gate, w_exp_up, w_exp_down, g_ple_in, w_ple_gate, w_ple, g_ple_out):
    P = dict(g_mix=g_mix, w_in=w_in, w_gate_f=w_gate_f, b_gate_f=b_gate_f, w_gate_b=w_gate_b,
             b_gate_b=b_gate_b, gla_norm=gla_norm, q_norm=q_norm, k_norm=k_norm, rpb=rpb,
             attn_norm=attn_norm, w_o=w_o, g_ffn=g_ffn, w_ff_gate=w_ff_gate, w_ff_up=w_ff_up,
             w_ff_down=w_ff_down, w_router=w_router, w_exp_gate=w_exp_gate, w_exp_up=w_exp_up,
             w_exp_down=w_exp_down, g_ple_in=g_ple_in, w_ple_gate=w_ple_gate, w_ple=w_ple,
             g_ple_out=g_ple_out)
    layers = [_prep_layer(i, P) for i in range(DEPTH)]
    return (_trunk(x_prompt, p_prompt, layers), _trunk(x_sample, p_sample, layers))
```

```python
import functools

import jax
import jax.numpy as jnp
from jax import lax
from jax.experimental import pallas as pl
from jax.experimental.pallas import tpu as pltpu

F32 = jnp.float32
BF16 = jnp.bfloat16

D_MODEL = 2048
DEPTH = 4
GRID_W = 64
GLA_H = 4
GLA_VW = 1024
GLA_KW = 512
GLA_DK = 128
GLA_DV = 256
GLA_RANK = 16
GLA_INV_NORMALIZER = 1.0 / 16.0
GLA_CHUNK = 64
NA_W = 1024
NA_H = 8
NA_HD = 128
WIN_ROWS = 8
WIN_COLS = 16
FF_DIM = 4096
N_EXP = 8
EXP_DIM = 1024
PLE_DIM = 256
EPS = 1e-6
NEG_INF = -1e30

Z_GQ, Z_GK, Z_GV, Z_GG = 0, 512, 1024, 2048
Z_AQ, Z_AK, Z_AV, Z_LR = 3072, 4096, 5120, 6144
LR_W = 256
NZ = Z_LR + LR_W

V7X_VMEM_BYTES = 64 * 1024 * 1024
LANES = 128


def _vmem(nbytes):
    return int(min(nbytes * 3 // 2 + (4 << 20), V7X_VMEM_BYTES - (6 << 20)))


def _rms(x, g):
    ms = jnp.mean(x * x, axis=-1, keepdims=True)
    return x * lax.rsqrt(ms + EPS) * g


def _silu(x):
    return x * jax.nn.sigmoid(x)


def _inproj_kernel(h_ref, g_ref, w_ref, z_ref, u_scr):
    @pl.when(pl.program_id(1) == 0)
    def _():
        u_scr[...] = _rms(h_ref[...], g_ref[...]).astype(BF16)

    z_ref[...] = jnp.dot(u_scr[...], w_ref[...], preferred_element_type=F32).astype(z_ref.dtype)


def inproj(h, g, w, *, tm=1024, tn=1280):
    n = h.shape[0]
    est = 2 * tm * D_MODEL * 4 + 2 * D_MODEL * tn * 2 + 2 * tm * tn * 2 + tm * D_MODEL * 2 + tm * tn * 4
    return pl.pallas_call(
        _inproj_kernel,
        grid=(n // tm, NZ // tn),
        in_specs=[
            pl.BlockSpec((tm, D_MODEL), lambda i, j: (i, 0)),
            pl.BlockSpec((1, D_MODEL), lambda i, j: (0, 0)),
            pl.BlockSpec((D_MODEL, tn), lambda i, j: (0, j)),
        ],
        out_specs=pl.BlockSpec((tm, tn), lambda i, j: (i, j)),
        out_shape=jax.ShapeDtypeStruct((n, NZ), BF16),
        scratch_shapes=[pltpu.VMEM((tm, D_MODEL), BF16)],
        compiler_params=pltpu.CompilerParams(
            dimension_semantics=("parallel", "arbitrary"), vmem_limit_bytes=_vmem(est)),
        name="inproj",
    )(h, g, w)


def _cumsum_rows(x, seg):
    idx = lax.broadcasted_iota(jnp.int32, x.shape, 0) & (seg - 1)
    s = 1
    while s < seg:
        x = x + jnp.where(idx >= s, pltpu.roll(x, s, 0), 0.0)
        s *= 2
    return x


def _log_sigmoid(x):
    return jnp.minimum(x, 0.0) - jnp.log1p(jnp.exp(-jnp.abs(x)))


def _gla_kernel(*refs, reverse, final, tb):
    if final:
        q_ref, k_ref, v_ref, lr_ref, wg_ref, bg_ref, of_ref, gg_ref, gn_ref, o_ref, s_scr = refs
    else:
        q_ref, k_ref, v_ref, lr_ref, wg_ref, bg_ref, o_ref, s_scr = refs
    C = GLA_CHUNK
    nchunk = tb // C

    @pl.when(pl.program_id(2) == 0)
    def _():
        s_scr[...] = jnp.zeros_like(s_scr)

    x = jnp.dot(lr_ref[...], wg_ref[...], preferred_element_type=F32) + bg_ref[...]
    la_blk = _log_sigmoid(x) * GLA_INV_NORMALIZER
    b_blk = _cumsum_rows(la_blk, C)

    row = lax.broadcasted_iota(jnp.int32, (C, C), 0)
    col = lax.broadcasted_iota(jnp.int32, (C, C), 1)
    mask = (col >= row) if reverse else (col <= row)
    scale = GLA_DK ** -0.5

    order = [(nchunk - 1 - ci) if reverse else ci for ci in range(nchunk)]
    sl = [slice(c * C, (c + 1) * C) for c in range(nchunk)]
    qts, atts, incs, dls = {}, {}, {}, {}
    for c in order:
        b = b_blk[sl[c]]
        blast = b[C - 1:C, :]
        if reverse:
            b = blast - b + la_blk[sl[c]]
        q = q_ref[sl[c], :].astype(F32)
        k = k_ref[sl[c], :].astype(F32)
        qts[c] = (q * jnp.exp(b) * scale).astype(BF16)
        kt = (k * jnp.exp(-b)).astype(BF16)
        ke_t = (k * jnp.exp(blast - b)).T.astype(BF16)
        dls[c] = jnp.broadcast_to(jnp.exp(blast), (8, GLA_DK)).T[:, 0:1]
        att = lax.dot_general(qts[c], kt, (((1,), (1,)), ((), ())), preferred_element_type=F32)
        atts[c] = jnp.where(mask, att, 0.0).astype(BF16)
        incs[c] = jnp.dot(ke_t, v_ref[sl[c], :], preferred_element_type=F32)
    state = s_scr[...]
    states = {}
    for c in order:
        states[c] = state.astype(BF16)
        state = dls[c] * state + incs[c]
    s_scr[...] = state
    for c in order:
        o = (jnp.dot(atts[c], v_ref[sl[c], :], preferred_element_type=F32)
             + jnp.dot(qts[c], states[c], preferred_element_type=F32))
        if final:
            y = _rms(o + of_ref[sl[c], :], gn_ref[...])
            gg = gg_ref[sl[c], :].astype(F32)
            o_ref[sl[c], :] = (y * _silu(gg)).astype(o_ref.dtype)
        else:
            o_ref[sl[c], :] = o


def gla_direction(z3, wg, bg, *, reverse, o_other=None, gn=None, tb=1024):
    B, T, _ = z3.shape
    nb = T // tb
    final = o_other is not None
    blk = (lambda j: nb - 1 - j) if reverse else (lambda j: j)
    qw, vw = GLA_DK, GLA_DV
    in_specs = [
        pl.BlockSpec((None, tb, qw), lambda b, h, j: (b, blk(j), Z_GQ // qw + h)),
        pl.BlockSpec((None, tb, qw), lambda b, h, j: (b, blk(j), Z_GK // qw + h)),
        pl.BlockSpec((None, tb, vw), lambda b, h, j: (b, blk(j), Z_GV // vw + h)),
        pl.BlockSpec((None, tb, LR_W), lambda b, h, j: (b, blk(j), Z_LR // LR_W)),
        pl.BlockSpec((LR_W, qw), lambda b, h, j: (0, h)),
        pl.BlockSpec((1, qw), lambda b, h, j: (0, h)),
    ]
    args = [z3, z3, z3, z3, wg, bg]
    if final:
        in_specs += [
            pl.BlockSpec((None, tb, vw), lambda b, h, j: (b, blk(j), h)),
            pl.BlockSpec((None, tb, vw), lambda b, h, j: (b, blk(j), Z_GG // vw + h)),
            pl.BlockSpec((None, 1, vw), lambda b, h, j: (h, 0, 0)),
        ]
        args += [o_other, z3, gn]
    out_dtype = BF16 if final else F32
    return pl.pallas_call(
        functools.partial(_gla_kernel, reverse=reverse, final=final, tb=tb),
        grid=(B, GLA_H, nb),
        in_specs=in_specs,
        out_specs=pl.BlockSpec((None, tb, vw), lambda b, h, j: (b, blk(j), h)),
        out_shape=jax.ShapeDtypeStruct((B, T, GLA_VW), out_dtype),
        scratch_shapes=[pltpu.VMEM((GLA_DK, GLA_DV), F32)],
        compiler_params=pltpu.CompilerParams(
            dimension_semantics=("parallel", "parallel", "arbitrary"), vmem_limit_bytes=_vmem(8 << 20)),
        name="gla_bwd" if reverse else "gla_fwd",
    )(*args)


def _natten_kernel(q_ref, k_ref, v_ref, bias_ref, qg_ref, kg_ref, o_ref, kn_scr, *, rows, rq):
    T = rows * GRID_W
    KB = 512
    WK = WIN_ROWS * GRID_W
    j = pl.program_id(2)

    @pl.when(j == 0)
    def _():
        def norm_block(i, carry):
            off = pl.multiple_of(i * KB, KB)
            kk = k_ref[pl.ds(off, KB), :].astype(F32)
            kn_scr[pl.ds(off, KB), :] = _rms(kk, kg_ref[...]).astype(BF16)
            return carry

        lax.fori_loop(0, T // KB, norm_block, 0)

    qn_blk = (_rms(q_ref[...].astype(F32), qg_ref[...]) * (NA_HD ** -0.5)).astype(BF16)
    koffs, dtis, scores = [], [], []
    for i in range(rq):
        r = j * rq + i
        rs = jnp.clip(r - WIN_ROWS // 2, 0, rows - WIN_ROWS)
        dtis.append(rs - r + (WIN_ROWS - 1))
        koffs.append(pl.multiple_of(rs * GRID_W, GRID_W))
        kw = kn_scr[pl.ds(koffs[i], WK), :]
        scores.append(lax.dot_general(qn_blk[i * GRID_W:(i + 1) * GRID_W], kw, (((1,), (1,)), ((), ())),
                                      preferred_element_type=F32))
    probs, sums = [], []
    for i in range(rq):
        s = scores[i] + bias_ref[dtis[i]]
        p = jnp.exp(s - jnp.max(s, axis=-1, keepdims=True))
        sums.append(jnp.sum(p, axis=-1, keepdims=True))
        probs.append(p.astype(BF16))
    for i in range(rq):
        vw = v_ref[pl.ds(koffs[i], WK), :]
        o = jnp.dot(probs[i], vw, preferred_element_type=F32) / sums[i]
        o_ref[i * GRID_W:(i + 1) * GRID_W, :] = o.astype(o_ref.dtype)


def natten(z3, bias_tbl, qg, kg, *, rq=32):
    B, T, _ = z3.shape
    rows = T // GRID_W
    assert rows >= WIN_ROWS and rows % rq == 0
    tq = rq * GRID_W
    hd = NA_HD
    est = 2 * 2 * T * hd * 2 + T * hd * 2 + 2 * WIN_ROWS * GRID_W * WIN_ROWS * GRID_W * 4 + 4 * tq * hd * 2
    return pl.pallas_call(
        functools.partial(_natten_kernel, rows=rows, rq=rq),
        grid=(B, NA_H, rows // rq),
        in_specs=[
            pl.BlockSpec((None, tq, hd), lambda b, h, j: (b, j, Z_AQ // hd + h)),
            pl.BlockSpec((None, T, hd), lambda b, h, j: (b, 0, Z_AK // hd + h)),
            pl.BlockSpec((None, T, hd), lambda b, h, j: (b, 0, Z_AV // hd + h)),
            pl.BlockSpec((None, WIN_ROWS, GRID_W, WIN_ROWS * GRID_W), lambda b, h, j: (h, 0, 0, 0)),
            pl.BlockSpec((1, hd), lambda b, h, j: (0, 0)),
            pl.BlockSpec((1, hd), lambda b, h, j: (0, 0)),
        ],
        out_specs=pl.BlockSpec((None, tq, hd), lambda b, h, j: (b, j, h)),
        out_shape=jax.ShapeDtypeStruct((B, T, NA_W), BF16),
        scratch_shapes=[pltpu.VMEM((T, hd), BF16)],
        compiler_params=pltpu.CompilerParams(
            dimension_semantics=("parallel", "parallel", "arbitrary"), vmem_limit_bytes=_vmem(est)),
        name="natten",
    )(z3, z3, z3, bias_tbl, qg, kg)


def natten_bias_table(rpb):
    c = jnp.arange(GRID_W, dtype=jnp.int32)
    c_start = jnp.clip(c - WIN_COLS // 2, 0, GRID_W - WIN_COLS)
    valid = (c[None, :] >= c_start[:, None]) & (c[None, :] < c_start[:, None] + WIN_COLS)
    dc = jnp.clip(c[None, :] - c[:, None], -(WIN_COLS - 1), WIN_COLS - 1) + (WIN_COLS - 1)
    bias_c = jnp.where(valid, rpb.astype(F32)[:, :, dc], NEG_INF)
    dr = jnp.arange(WIN_ROWS)[:, None] + jnp.arange(WIN_ROWS)[None, :]
    tbl = bias_c[:, dr]
    return tbl.transpose(0, 1, 3, 2, 4).reshape(NA_H, WIN_ROWS, GRID_W, WIN_ROWS * GRID_W)


def _outproj_kernel(og_ref, on_ref, an_ref, w_ref, h_ref, o_ref):
    na = _rms(on_ref[...].astype(F32), an_ref[...]).astype(BF16)
    acc = jnp.dot(og_ref[...], w_ref[0:GLA_VW, :], preferred_element_type=F32)
    acc += jnp.dot(na, w_ref[GLA_VW:GLA_VW + NA_W, :], preferred_element_type=F32)
    o_ref[...] = h_ref[...] + acc


def outproj(o_gla, o_na, an, w_o, h, *, tm=512):
    n = h.shape[0]
    est = 2 * 2 * tm * 1024 * 2 + 2 * D_MODEL * D_MODEL * 2 + 4 * tm * D_MODEL * 4 + tm * D_MODEL * 4
    return pl.pallas_call(
        _outproj_kernel,
        grid=(n // tm,),
        in_specs=[
            pl.BlockSpec((tm, GLA_VW), lambda i: (i, 0)),
            pl.BlockSpec((tm, NA_W), lambda i: (i, 0)),
            pl.BlockSpec((1, NA_W), lambda i: (0, 0)),
            pl.BlockSpec((D_MODEL, D_MODEL), lambda i: (0, 0)),
            pl.BlockSpec((tm, D_MODEL), lambda i: (i, 0)),
        ],
        out_specs=pl.BlockSpec((tm, D_MODEL), lambda i: (i, 0)),
        out_shape=jax.ShapeDtypeStruct((n, D_MODEL), F32),
        compiler_params=pltpu.CompilerParams(
            dimension_semantics=("parallel",), vmem_limit_bytes=_vmem(est)),
        name="outproj",
    )(o_gla, o_na, an, w_o, h)


def _ffn_kernel(h_ref, g_ref, wg_ref, wu_ref, wd_ref, o_ref, u_scr):
    @pl.when(pl.program_id(1) == 0)
    def _():
        h = h_ref[...]
        u_scr[...] = _rms(h, g_ref[...]).astype(BF16)
        o_ref[...] = h

    u = u_scr[...]
    a = jnp.dot(u, wg_ref[...], preferred_element_type=F32)
    b = jnp.dot(u, wu_ref[...], preferred_element_type=F32)
    hid = (_silu(a) * b).astype(BF16)
    o_ref[...] += jnp.dot(hid, wd_ref[...], preferred_element_type=F32)


def ffn_dense(h, g, wg, wu, wd, *, tm=1024, tf=256):
    n = h.shape[0]
    est = 4 * tm * D_MODEL * 4 + tm * D_MODEL * 2 + 2 * 3 * D_MODEL * tf * 2 + 3 * tm * tf * 4
    return pl.pallas_call(
        _ffn_kernel,
        grid=(n // tm, FF_DIM // tf),
        in_specs=[
            pl.BlockSpec((tm, D_MODEL), lambda i, f: (i, 0)),
            pl.BlockSpec((1, D_MODEL), lambda i, f: (0, 0)),
            pl.BlockSpec((D_MODEL, tf), lambda i, f: (0, f)),
            pl.BlockSpec((D_MODEL, tf), lambda i, f: (0, f)),
            pl.BlockSpec((tf, D_MODEL), lambda i, f: (f, 0)),
        ],
        out_specs=pl.BlockSpec((tm, D_MODEL), lambda i, f: (i, 0)),
        out_shape=jax.ShapeDtypeStruct((n, D_MODEL), F32),
        scratch_shapes=[pltpu.VMEM((tm, D_MODEL), BF16)],
        compiler_params=pltpu.CompilerParams(
            dimension_semantics=("parallel", "arbitrary"), vmem_limit_bytes=_vmem(est)),
        name="ffn_dense",
    )(h, g, wg, wu, wd)


REC_E, REC_W, REC_RANK = 0, 2, 4
MOE_TOP_K = 2


def _route_kernel(h_ref, g_ref, wr_ref, rec_ref, cnt_ref, carry_scr):
    tm = h_ref.shape[0]
    lane = lax.broadcasted_iota(jnp.int32, (tm, LANES), 1)

    @pl.when(pl.program_id(0) == 0)
    def _():
        carry_scr[...] = jnp.zeros_like(carry_scr)

    u = _rms(h_ref[...], g_ref[...]).astype(BF16)
    logits = jnp.dot(u, wr_ref[...], preferred_element_type=F32)
    lg = jnp.where(lane < N_EXP, logits, -jnp.inf)
    v1 = jnp.max(lg, axis=-1, keepdims=True)
    i1 = jnp.min(jnp.where(lg == v1, lane, LANES), axis=-1, keepdims=True)
    lg2 = jnp.where(lane == i1, -jnp.inf, lg)
    v2 = jnp.max(lg2, axis=-1, keepdims=True)
    i2 = jnp.min(jnp.where(lg2 == v2, lane, LANES), axis=-1, keepdims=True)
    e2 = jnp.exp(v2 - v1)
    w1 = 1.0 / (1.0 + e2)
    w2 = e2 / (1.0 + e2)
    onehot = jnp.where(lane == i1, 1.0, 0.0) + jnp.where(lane == i2, 1.0, 0.0)
    incl = _cumsum_rows(onehot, tm)
    before = incl - onehot + carry_scr[0:1, :]
    r1 = jnp.sum(jnp.where(lane == i1, before, 0.0), axis=-1, keepdims=True)
    r2 = jnp.sum(jnp.where(lane == i2, before, 0.0), axis=-1, keepdims=True)
    rec = jnp.where(lane == REC_E, i1.astype(F32), 0.0)
    rec = jnp.where(lane == REC_E + 1, i2.astype(F32), rec)
    rec = jnp.where(lane == REC_W, w1, rec)
    rec = jnp.where(lane == REC_W + 1, w2, rec)
    rec = jnp.where(lane == REC_RANK, r1, rec)
    rec = jnp.where(lane == REC_RANK + 1, r2, rec)
    rec_ref[...] = rec
    carry_scr[...] = carry_scr[...] + incl[tm - 1:tm, :]
    cnt_ref[...] = carry_scr[...]


def moe_route(h, g, wr, *, tm=512):
    n = h.shape[0]
    est = 2 * tm * D_MODEL * 4 + D_MODEL * LANES * 4 + 8 * tm * LANES * 4
    return pl.pallas_call(
        _route_kernel,
        grid=(n // tm,),
        in_specs=[
            pl.BlockSpec((tm, D_MODEL), lambda i: (i, 0)),
            pl.BlockSpec((1, D_MODEL), lambda i: (0, 0)),
            pl.BlockSpec((D_MODEL, LANES), lambda i: (0, 0)),
        ],
        out_specs=[pl.BlockSpec((tm, LANES), lambda i: (i, 0)), pl.BlockSpec((8, LANES), lambda i: (0, 0))],
        out_shape=[jax.ShapeDtypeStruct((n, LANES), F32), jax.ShapeDtypeStruct((8, LANES), F32)],
        scratch_shapes=[pltpu.VMEM((8, LANES), F32)],
        compiler_params=pltpu.CompilerParams(
            dimension_semantics=("arbitrary",), vmem_limit_bytes=_vmem(est)),
        name="moe_route",
    )(h, g, wr)


def _idx_copy(idx_hbm, idx_smem, sem, step, slot):
    return pltpu.make_async_copy(idx_hbm.at[pl.ds(step, 1)], idx_smem.at[pl.ds(slot, 1)], sem.at[slot])


def _dispatch_kernel(pad_ref, h_ref, g_ref, idx_hbm, xs_ref, ubuf, zrow, idx_smem, row_sem, idx_sem, pad_sem):
    tm = h_ref.shape[0]
    i = pl.program_id(0)
    nsteps = pl.num_programs(0)
    slot = i % 2

    def row_copy(s, r, dst_row):
        return pltpu.make_async_copy(ubuf.at[s, pl.ds(r, 1)], xs_ref.at[pl.ds(dst_row, 1)], row_sem.at[s])

    def wait_rows(s):
        def body(r, c):
            for _ in range(MOE_TOP_K):
                row_copy(s, 0, 0).wait()
            return c
        lax.fori_loop(0, tm, body, 0, unroll=16)

    @pl.when(i == 0)
    def _():
        _idx_copy(idx_hbm, idx_smem, idx_sem, 0, 0).start()

    _idx_copy(idx_hbm, idx_smem, idx_sem, i, slot).wait()

    @pl.when(i + 1 < nsteps)
    def _():
        _idx_copy(idx_hbm, idx_smem, idx_sem, i + 1, 1 - slot).start()

    @pl.when(i >= 2)
    def _():
        wait_rows(slot)

    ubuf[slot] = _rms(h_ref[...], g_ref[...])

    def issue(r, c):
        for k in range(MOE_TOP_K):
            row_copy(slot, r, idx_smem[slot, MOE_TOP_K * r + k]).start()
        return c
    lax.fori_loop(0, tm, issue, 0, unroll=8)

    @pl.when(i == nsteps - 1)
    def _():
        zrow[...] = jnp.zeros_like(zrow)
        for e in range(N_EXP):
            def pad_copy(j, e=e):
                return pltpu.make_async_copy(zrow.at[pl.ds(0, 1)], xs_ref.at[pl.ds(pad_ref[e] + j, 1)], pad_sem.at[0])

            def start(j, c, pad_copy=pad_copy):
                pad_copy(j).start()
                return c

            def wait(j, c, pad_copy=pad_copy):
                pad_copy(j).wait()
                return c
            lax.fori_loop(0, pad_ref[N_EXP + e], start, 0)
            lax.fori_loop(0, pad_ref[N_EXP + e], wait, 0)
        wait_rows(slot)

        @pl.when(nsteps >= 2)
        def _():
            wait_rows(1 - slot)


def moe_dispatch(h, g, idx, padinfo, n_sorted, *, tm):
    n = h.shape[0]
    est = 2 * tm * D_MODEL * 4 + 2 * tm * D_MODEL * 4 + 8 * D_MODEL * 4
    return pl.pallas_call(
        _dispatch_kernel,
        grid_spec=pltpu.PrefetchScalarGridSpec(
            num_scalar_prefetch=1,
            grid=(n // tm,),
            in_specs=[
                pl.BlockSpec((tm, D_MODEL), lambda i, pad: (i, 0)),
                pl.BlockSpec((1, D_MODEL), lambda i, pad: (0, 0)),
                pl.BlockSpec(memory_space=pl.ANY),
            ],
            out_specs=pl.BlockSpec(memory_space=pl.ANY),
            scratch_shapes=[
                pltpu.VMEM((2, tm, D_MODEL), F32),
                pltpu.VMEM((8, D_MODEL), F32),
                pltpu.SMEM((2, MOE_TOP_K * tm), jnp.int32),
                pltpu.SemaphoreType.DMA((2,)),
                pltpu.SemaphoreType.DMA((2,)),
                pltpu.SemaphoreType.DMA((1,)),
            ],
        ),
        out_shape=jax.ShapeDtypeStruct((n_sorted, D_MODEL), F32),
        compiler_params=pltpu.CompilerParams(
            dimension_semantics=("arbitrary",), vmem_limit_bytes=_vmem(est)),
        name="moe_dispatch",
    )(padinfo, h, g, idx)


def _experts_kernel(be_ref, nu_ref, x_ref, wg_ref, wu_ref, wd_ref, y_ref):
    del be_ref
    used = pl.program_id(0) < nu_ref[0]

    @pl.when(used)
    def _():
        x = x_ref[...].astype(BF16)
        a = jnp.dot(x, wg_ref[...], preferred_element_type=F32)
        b = jnp.dot(x, wu_ref[...], preferred_element_type=F32)
        hid = (_silu(a) * b).astype(BF16)
        y_ref[...] = jnp.dot(hid, wd_ref[...], preferred_element_type=F32)

    @pl.when(jnp.logical_not(used))
    def _():
        y_ref[...] = jnp.zeros_like(y_ref)


def moe_experts(block_expert, n_used, xs, wg, wu, wd, *, bm):
    n_sorted = xs.shape[0]
    est = 4 * bm * D_MODEL * 4 + 2 * 3 * D_MODEL * EXP_DIM * 2 + 3 * bm * EXP_DIM * 4 + bm * D_MODEL * 2
    row_blk = lambda i, be, nu: (jnp.where(i < nu[0], i, 0), 0)
    return pl.pallas_call(
        _experts_kernel,
        grid_spec=pltpu.PrefetchScalarGridSpec(
            num_scalar_prefetch=2,
            grid=(n_sorted // bm,),
            in_specs=[
                pl.BlockSpec((bm, D_MODEL), row_blk),
                pl.BlockSpec((None, D_MODEL, EXP_DIM), lambda i, be, nu: (be[i], 0, 0)),
                pl.BlockSpec((None, D_MODEL, EXP_DIM), lambda i, be, nu: (be[i], 0, 0)),
                pl.BlockSpec((None, EXP_DIM, D_MODEL), lambda i, be, nu: (be[i], 0, 0)),
            ],
            out_specs=pl.BlockSpec((bm, D_MODEL), lambda i, be, nu: (i, 0)),
        ),
        out_shape=jax.ShapeDtypeStruct((n_sorted, D_MODEL), F32),
        compiler_params=pltpu.CompilerParams(
            dimension_semantics=("arbitrary",), vmem_limit_bytes=_vmem(est)),
        name="moe_experts",
    )(block_expert, n_used, xs, wg, wu, wd)


def _combine_kernel(h_ref, rec_ref, idx_hbm, ys_ref, o_ref, ybuf, idx_smem, row_sem, idx_sem):
    tm = h_ref.shape[0]
    i = pl.program_id(0)
    nsteps = pl.num_programs(0)
    slot = i % 2

    def row_copy(s, k, r, src_row):
        return pltpu.make_async_copy(ys_ref.at[pl.ds(src_row, 1)], ybuf.at[s, k, pl.ds(r, 1)], row_sem.at[s])

    def issue_rows(s):
        def body(r, c):
            for k in range(MOE_TOP_K):
                row_copy(s, k, r, idx_smem[s, MOE_TOP_K * r + k]).start()
            return c
        lax.fori_loop(0, tm, body, 0, unroll=8)

    @pl.when(i == 0)
    def _():
        _idx_copy(idx_hbm, idx_smem, idx_sem, 0, 0).start()
        _idx_copy(idx_hbm, idx_smem, idx_sem, 0, 0).wait()
        issue_rows(0)

        @pl.when(nsteps >= 2)
        def _():
            _idx_copy(idx_hbm, idx_smem, idx_sem, 1, 1).start()

    @pl.when(i + 1 < nsteps)
    def _():
        _idx_copy(idx_hbm, idx_smem, idx_sem, i + 1, 1 - slot).wait()
        issue_rows(1 - slot)

    def wait_body(r, c):
        for k in range(MOE_TOP_K):
            row_copy(slot, k, 0, 0).wait()
        return c
    lax.fori_loop(0, tm, wait_body, 0, unroll=16)

    @pl.when(i + 2 < nsteps)
    def _():
        _idx_copy(idx_hbm, idx_smem, idx_sem, i + 2, slot).start()

    rec = rec_ref[...]
    w1 = rec[:, REC_W:REC_W + 1]
    w2 = rec[:, REC_W + 1:REC_W + 2]
    o_ref[...] = h_ref[...] + w1 * ybuf[slot, 0] + w2 * ybuf[slot, 1]


def moe_combine(h, rec, idx, ys, *, tm):
    n = h.shape[0]
    est = 4 * tm * D_MODEL * 4 + 2 * tm * LANES * 4 + 2 * MOE_TOP_K * tm * D_MODEL * 4
    return pl.pallas_call(
        _combine_kernel,
        grid=(n // tm,),
        in_specs=[
            pl.BlockSpec((tm, D_MODEL), lambda i: (i, 0)),
            pl.BlockSpec((tm, LANES), lambda i: (i, 0)),
            pl.BlockSpec(memory_space=pl.ANY),
            pl.BlockSpec(memory_space=pl.ANY),
        ],
        out_specs=pl.BlockSpec((tm, D_MODEL), lambda i: (i, 0)),
        out_shape=jax.ShapeDtypeStruct((n, D_MODEL), F32),
        scratch_shapes=[
            pltpu.VMEM((2, MOE_TOP_K, tm, D_MODEL), F32),
            pltpu.SMEM((2, MOE_TOP_K * tm), jnp.int32),
            pltpu.SemaphoreType.DMA((2,)),
            pltpu.SemaphoreType.DMA((2,)),
        ],
        compiler_params=pltpu.CompilerParams(
            dimension_semantics=("arbitrary",), vmem_limit_bytes=_vmem(est)),
        name="moe_combine",
    )(h, rec, idx, ys)


def ffn_moe(h, g, wr, wg, wu, wd, *, bm=512, tm=256):
    n = h.shape[0]
    n_sorted = MOE_TOP_K * n + N_EXP * bm
    rec, cnt = moe_route(h, g, wr)
    expert = rec[:, REC_E:REC_E + MOE_TOP_K].astype(jnp.int32)
    rank = rec[:, REC_RANK:REC_RANK + MOE_TOP_K].astype(jnp.int32)
    counts = cnt[0, :N_EXP].astype(jnp.int32)
    padded = (counts + bm - 1) // bm * bm
    pend = jnp.cumsum(padded)
    pstart = pend - padded
    dest = jnp.sum(jnp.where(expert[..., None] == jnp.arange(N_EXP), pstart, 0), axis=-1) + rank
    idx = dest.reshape(n // tm, MOE_TOP_K * tm)
    padinfo = jnp.concatenate([pstart + counts, padded - counts]).astype(jnp.int32)
    n_used = (pend[-1:] // bm).astype(jnp.int32)
    blk_start = jnp.arange(n_sorted // bm, dtype=jnp.int32) * bm
    block_expert = jnp.minimum(jnp.sum(blk_start[:, None] >= pend[None, :], axis=1), N_EXP - 1).astype(jnp.int32)
    xs = moe_dispatch(h, g, idx, padinfo, n_sorted, tm=tm)
    ys = moe_experts(block_expert, n_used, xs, wg, wu, wd, bm=bm)
    return moe_combine(h, rec, idx, ys, tm=tm)


def _ple_kernel(h_ref, p_ref, gi_ref, wgate_ref, wp_ref, go_ref, o_ref):
    h = h_ref[...]
    hn = _rms(h, gi_ref[...]).astype(BF16)
    gate = jax.nn.sigmoid(jnp.dot(hn, wgate_ref[...], preferred_element_type=F32))
    e = jnp.dot(p_ref[...].astype(BF16), wp_ref[...], preferred_element_type=F32)
    o_ref[...] = h + gate * _rms(e, go_ref[...])


def ple(h, p_all, layer, gi, wgate, wp, go, *, tm=512):
    n = h.shape[0]
    est = 4 * tm * D_MODEL * 4 + 2 * D_MODEL * D_MODEL * 2 + 2 * PLE_DIM * D_MODEL * 2 + 3 * tm * D_MODEL * 4
    return pl.pallas_call(
        _ple_kernel,
        grid=(n // tm,),
        in_specs=[
            pl.BlockSpec((tm, D_MODEL), lambda i: (i, 0)),
            pl.BlockSpec((None, tm, PLE_DIM), lambda i: (layer, i, 0)),
            pl.BlockSpec((1, D_MODEL), lambda i: (0, 0)),
            pl.BlockSpec((D_MODEL, D_MODEL), lambda i: (0, 0)),
            pl.BlockSpec((PLE_DIM, D_MODEL), lambda i: (0, 0)),
            pl.BlockSpec((1, D_MODEL), lambda i: (0, 0)),
        ],
        out_specs=pl.BlockSpec((tm, D_MODEL), lambda i: (i, 0)),
        out_shape=jax.ShapeDtypeStruct((n, D_MODEL), F32),
        compiler_params=pltpu.CompilerParams(
            dimension_semantics=("parallel",), vmem_limit_bytes=_vmem(est)),
        name="ple",
    )(h, p_all, gi, wgate, wp, go)


def _prep_layer(i, P):
    w_in = P["w_in"][i]
    lr0 = Z_AQ
    w_in_r = jnp.concatenate(
        [w_in[:, :lr0], w_in[:, lr0 + 2 * GLA_RANK:], w_in[:, lr0:lr0 + 2 * GLA_RANK],
         jnp.zeros((D_MODEL, LR_W - 2 * GLA_RANK), w_in.dtype)], axis=1).astype(BF16)
    zpad = jnp.zeros((LR_W, GLA_KW), F32)
    wgf = zpad.at[0:GLA_RANK].set(P["w_gate_f"][i]).astype(BF16)
    wgb = zpad.at[GLA_RANK:2 * GLA_RANK].set(P["w_gate_b"][i]).astype(BF16)
    L = dict(
        g_mix=P["g_mix"][i][None], w_in=w_in_r, wgf=wgf, wgb=wgb,
        bgf=P["b_gate_f"][i][None], bgb=P["b_gate_b"][i][None],
        gla_norm=P["gla_norm"][i].reshape(GLA_H, 1, GLA_DV),
        q_norm=P["q_norm"][i][None], k_norm=P["k_norm"][i][None],
        bias_tbl=natten_bias_table(P["rpb"][i]), attn_norm=P["attn_norm"][i][None],
        w_o=P["w_o"][i].astype(BF16), g_ffn=P["g_ffn"][i][None],
        g_ple_in=P["g_ple_in"][i][None], w_ple_gate=P["w_ple_gate"][i].astype(BF16),
        w_ple=P["w_ple"][i].astype(BF16), g_ple_out=P["g_ple_out"][i][None],
    )
    j = i // 2
    if i % 2 == 0:
        L.update(wg=P["w_ff_gate"][j].astype(BF16), wu=P["w_ff_up"][j].astype(BF16),
                 wd=P["w_ff_down"][j].astype(BF16))
    else:
        wr = jnp.zeros((D_MODEL, LANES), F32).at[:, :N_EXP].set(P["w_router"][j]).astype(BF16)
        L.update(wr=wr, wg=P["w_exp_gate"][j].astype(BF16), wu=P["w_exp_up"][j].astype(BF16),
                 wd=P["w_exp_down"][j].astype(BF16))
    return L


def _trunk(x, p, layers):
    B, T, _ = x.shape
    n = B * T
    h = x.reshape(n, D_MODEL)
    p_all = p.reshape(DEPTH, n, PLE_DIM)
    for i, L in enumerate(layers):
        z = inproj(h, L["g_mix"], L["w_in"])
        z3 = z.reshape(B, T, NZ)
        o_f = gla_direction(z3, L["wgf"], L["bgf"], reverse=False)
        o_gla = gla_direction(z3, L["wgb"], L["bgb"], reverse=True, o_other=o_f, gn=L["gla_norm"])
        o_na = natten(z3, L["bias_tbl"], L["q_norm"], L["k_norm"])
        h = outproj(o_gla.reshape(n, GLA_VW), o_na.reshape(n, NA_W), L["attn_norm"], L["w_o"], h)
        if i % 2 == 0:
            h = ffn_dense(h, L["g_ffn"], L["wg"], L["wu"], L["wd"])
        else:
            h = ffn_moe(h, L["g_ffn"], L["wr"], L["wg"], L["wu"], L["wd"])
        h = ple(h, p_all, i, L["g_ple_in"], L["w_ple_gate"], L["w_ple"], L["g_ple_out"])
    return h.reshape(B, T, D_MODEL)


def kernel(x_prompt, x_sample, p_prompt, p_sample, g_mix, w_in, w_gate_f, b_gate_f, w_gate_b, b_gate_b, gla_norm, q_norm, k_norm, rpb, attn_norm, w_o, g_ffn, w_ff_gate, w_ff_up, w_ff_down, w_router, w_exp_gate, w_exp_up, w_exp_down, g_ple_in, w_ple_gate, w_ple, g_ple_out):
    P = dict(g_mix=g_mix, w_in=w_in, w_gate_f=w_gate_f, b_gate_f=b_gate_f, w_gate_b=w_gate_b,
             b_gate_b=b_gate_b, gla_norm=gla_norm, q_norm=q_norm, k_norm=k_norm, rpb=rpb,
             attn_norm=attn_norm, w_o=w_o, g_ffn=g_ffn, w_ff_gate=w_ff_gate, w_ff_up=w_ff_up,
             w_ff_down=w_ff_down, w_router=w_router, w_exp_gate=w_exp_gate, w_exp_up=w_exp_up,
             w_exp_down=w_exp_down, g_ple_in=g_ple_in, w_ple_gate=w_ple_gate, w_ple=w_ple,
             g_ple_out=g_ple_out)
    layers = [_prep_layer(i, P) for i in range(DEPTH)]
    return (_trunk(x_prompt, p_prompt, layers), _trunk(x_sample, p_sample, layers))
```

```python
import functools

import jax
import jax.numpy as jnp
from jax import lax
from jax.experimental import pallas as pl
from jax.experimental.pallas import tpu as pltpu

F32 = jnp.float32
BF16 = jnp.bfloat16

D_MODEL = 2048
DEPTH = 4
GRID_W = 64
GLA_H = 4
GLA_VW = 1024
GLA_KW = 512
GLA_DK = 128
GLA_DV = 256
GLA_RANK = 16
GLA_INV_NORMALIZER = 1.0 / 16.0
GLA_CHUNK = 64
NA_W = 1024
NA_H = 8
NA_HD = 128
WIN_ROWS = 8
WIN_COLS = 16
FF_DIM = 4096
N_EXP = 8
EXP_DIM = 1024
PLE_DIM = 256
EPS = 1e-6
NEG_INF = -1e30

Z_GQ, Z_GK, Z_GV, Z_GG = 0, 512, 1024, 2048
Z_AQ, Z_AK, Z_AV, NZ = 3072, 4096, 5120, 6144
LR_W = 256

V7X_VMEM_BYTES = 64 * 1024 * 1024
LANES = 128


def _vmem(nbytes):
    return int(min(nbytes * 3 // 2 + (4 << 20), V7X_VMEM_BYTES - (6 << 20)))


def _rms(x, g):
    ms = jnp.mean(x * x, axis=-1, keepdims=True)
    return x * lax.rsqrt(ms + EPS) * g


def _silu(x):
    return x * jax.nn.sigmoid(x)


def _inproj_kernel(h_ref, g_ref, w_ref, wl_ref, z_ref, zl_ref, u_scr):
    j = pl.program_id(1)
    last = pl.num_programs(1) - 1

    @pl.when(j == 0)
    def _():
        u_scr[...] = _rms(h_ref[...], g_ref[...]).astype(BF16)

    @pl.when(j < last)
    def _():
        z_ref[...] = jnp.dot(u_scr[...], w_ref[...], preferred_element_type=F32).astype(z_ref.dtype)

    @pl.when(j == last)
    def _():
        zl_ref[...] = jnp.dot(u_scr[...], wl_ref[...], preferred_element_type=F32).astype(zl_ref.dtype)


def inproj(h, g, w, wl, *, tm=1024, tn=1536):
    n = h.shape[0]
    nmain = NZ // tn
    est = (2 * tm * D_MODEL * 4 + 2 * D_MODEL * tn * 2 + 2 * tm * tn * 2 + tm * D_MODEL * 2 + tm * tn * 4
           + 2 * D_MODEL * LR_W * 2 + 2 * tm * LR_W * 2)
    main = lambda j: jnp.minimum(j, nmain - 1)
    return pl.pallas_call(
        _inproj_kernel,
        grid=(n // tm, nmain + 1),
        in_specs=[
            pl.BlockSpec((tm, D_MODEL), lambda i, j: (i, 0)),
            pl.BlockSpec((1, D_MODEL), lambda i, j: (0, 0)),
            pl.BlockSpec((D_MODEL, tn), lambda i, j: (0, main(j))),
            pl.BlockSpec((D_MODEL, LR_W), lambda i, j: (0, 0)),
        ],
        out_specs=[pl.BlockSpec((tm, tn), lambda i, j: (i, main(j))),
                   pl.BlockSpec((tm, LR_W), lambda i, j: (i, 0))],
        out_shape=[jax.ShapeDtypeStruct((n, NZ), BF16), jax.ShapeDtypeStruct((n, LR_W), BF16)],
        scratch_shapes=[pltpu.VMEM((tm, D_MODEL), BF16)],
        compiler_params=pltpu.CompilerParams(
            dimension_semantics=("parallel", "arbitrary"), vmem_limit_bytes=_vmem(est)),
        name="inproj",
    )(h, g, w, wl)


def _cumsum_rows(x, seg):
    idx = lax.broadcasted_iota(jnp.int32, x.shape, 0) & (seg - 1)
    s = 1
    while s < seg:
        x = x + jnp.where(idx >= s, pltpu.roll(x, s, 0), 0.0)
        s *= 2
    return x


def _log_sigmoid(x):
    return jnp.minimum(x, 0.0) - jnp.log(1.0 + jnp.exp(-jnp.abs(x)))


def _split_bf16(x):
    hi = x.astype(BF16)
    r1 = x - hi.astype(F32)
    mid = r1.astype(BF16)
    lo = (r1 - mid.astype(F32)).astype(BF16)
    return hi, mid, lo


def _gla_kernel(*refs, reverse, final, tb):
    if final:
        q_ref, k_ref, v_ref, lr_ref, wg_ref, bg_ref, of_ref, gg_ref, gn_ref, o_ref, s_scr = refs
    else:
        q_ref, k_ref, v_ref, lr_ref, wg_ref, bg_ref, o_ref, s_scr = refs
    C = GLA_CHUNK
    nchunk = tb // C

    @pl.when(pl.program_id(2) == 0)
    def _():
        s_scr[...] = jnp.zeros_like(s_scr)

    x = jnp.dot(lr_ref[...], wg_ref[...], preferred_element_type=F32) + bg_ref[...]
    la_parts = _split_bf16(_log_sigmoid(x) * GLA_INV_NORMALIZER)

    row = lax.broadcasted_iota(jnp.int32, (C, C), 0)
    col = lax.broadcasted_iota(jnp.int32, (C, C), 1)
    mask = (col >= row) if reverse else (col <= row)
    tri = jnp.where(mask, 1.0, 0.0).astype(BF16)
    scale = GLA_DK ** -0.5

    order = [(nchunk - 1 - ci) if reverse else ci for ci in range(nchunk)]
    sl = [slice(c * C, (c + 1) * C) for c in range(nchunk)]
    qts, atts, incs, dls = {}, {}, {}, {}
    bs = {c: sum(jnp.dot(tri, part[sl[c]], preferred_element_type=F32) for part in la_parts) for c in order}
    for c in order:
        b = bs[c]
        blast = b[0:1, :] if reverse else b[C - 1:C, :]
        q = q_ref[sl[c], :].astype(F32)
        k = k_ref[sl[c], :].astype(F32)
        qts[c] = (q * jnp.exp(b) * scale).astype(BF16)
        kt = (k * jnp.exp(-b)).astype(BF16)
        ke_t = (k * jnp.exp(blast - b)).T.astype(BF16)
        dls[c] = jnp.broadcast_to(jnp.exp(blast), (8, GLA_DK)).T[:, 0:1]
        att = lax.dot_general(qts[c], kt, (((1,), (1,)), ((), ())), preferred_element_type=F32)
        atts[c] = jnp.where(mask, att, 0.0).astype(BF16)
        incs[c] = jnp.dot(ke_t, v_ref[sl[c], :], preferred_element_type=F32)
    state = s_scr[...]
    states = {}
    for c in order:
        states[c] = state.astype(BF16)
        state = dls[c] * state + incs[c]
    s_scr[...] = state
    for c in order:
        o = (jnp.dot(atts[c], v_ref[sl[c], :], preferred_element_type=F32)
             + jnp.dot(qts[c], states[c], preferred_element_type=F32))
        if final:
            y = _rms(o + of_ref[sl[c], :], gn_ref[...])
            gg = gg_ref[sl[c], :].astype(F32)
            o_ref[sl[c], :] = (y * _silu(gg)).astype(o_ref.dtype)
        else:
            o_ref[sl[c], :] = o


def gla_direction(z3, zl3, wg, bg, *, reverse, o_other=None, gn=None, tb=1024):
    B, T, _ = z3.shape
    nb = T // tb
    final = o_other is not None
    blk = (lambda j: nb - 1 - j) if reverse else (lambda j: j)
    qw, vw = GLA_DK, GLA_DV
    in_specs = [
        pl.BlockSpec((None, tb, qw), lambda b, h, j: (b, blk(j), Z_GQ // qw + h)),
        pl.BlockSpec((None, tb, qw), lambda b, h, j: (b, blk(j), Z_GK // qw + h)),
        pl.BlockSpec((None, tb, vw), lambda b, h, j: (b, blk(j), Z_GV // vw + h)),
        pl.BlockSpec((None, tb, LR_W), lambda b, h, j: (b, blk(j), 0)),
        pl.BlockSpec((LR_W, qw), lambda b, h, j: (0, h)),
        pl.BlockSpec((1, qw), lambda b, h, j: (0, h)),
    ]
    args = [z3, z3, z3, zl3, wg, bg]
    if final:
        in_specs += [
            pl.BlockSpec((None, tb, vw), lambda b, h, j: (b, blk(j), h)),
            pl.BlockSpec((None, tb, vw), lambda b, h, j: (b, blk(j), Z_GG // vw + h)),
            pl.BlockSpec((None, 1, vw), lambda b, h, j: (h, 0, 0)),
        ]
        args += [o_other, z3, gn]
    out_dtype = BF16 if final else F32
    return pl.pallas_call(
        functools.partial(_gla_kernel, reverse=reverse, final=final, tb=tb),
        grid=(B, GLA_H, nb),
        in_specs=in_specs,
        out_specs=pl.BlockSpec((None, tb, vw), lambda b, h, j: (b, blk(j), h)),
        out_shape=jax.ShapeDtypeStruct((B, T, GLA_VW), out_dtype),
        scratch_shapes=[pltpu.VMEM((GLA_DK, GLA_DV), F32)],
        compiler_params=pltpu.CompilerParams(
            dimension_semantics=("parallel", "parallel", "arbitrary"), vmem_limit_bytes=_vmem(8 << 20)),
        name="gla_bwd" if reverse else "gla_fwd",
    )(*args)


def _natten_kernel(q_ref, k_ref, v_ref, bias_ref, qg_ref, kg_ref, o_ref, kn_scr, *, rows, rq):
    T = rows * GRID_W
    KB = 2048
    WK = WIN_ROWS * GRID_W
    j = pl.program_id(2)

    @pl.when(j == 0)
    def _():
        def norm_block(i, carry):
            off = pl.multiple_of(i * KB, KB)
            kk = k_ref[pl.ds(off, KB), :].astype(F32)
            kn_scr[pl.ds(off, KB), :] = _rms(kk, kg_ref[...]).astype(BF16)
            return carry

        lax.fori_loop(0, T // KB, norm_block, 0)

    qn_blk = (_rms(q_ref[...].astype(F32), qg_ref[...]) * (NA_HD ** -0.5)).astype(BF16)
    koffs, dtis, scores = [], [], []
    for i in range(rq):
        r = j * rq + i
        rs = jnp.clip(r - WIN_ROWS // 2, 0, rows - WIN_ROWS)
        dtis.append(rs - r + (WIN_ROWS - 1))
        koffs.append(pl.multiple_of(rs * GRID_W, GRID_W))
        kw = kn_scr[pl.ds(koffs[i], WK), :]
        scores.append(lax.dot_general(qn_blk[i * GRID_W:(i + 1) * GRID_W], kw, (((1,), (1,)), ((), ())),
                                      preferred_element_type=F32))
    probs, sums = [], []
    for i in range(rq):
        s = scores[i] + bias_ref[dtis[i]]
        p = jnp.exp(s - jnp.max(s, axis=-1, keepdims=True))
        sums.append(jnp.sum(p, axis=-1, keepdims=True))
        probs.append(p.astype(BF16))
    for i in range(rq):
        vw = v_ref[pl.ds(koffs[i], WK), :]
        o = jnp.dot(probs[i], vw, preferred_element_type=F32) / sums[i]
        o_ref[i * GRID_W:(i + 1) * GRID_W, :] = o.astype(o_ref.dtype)


def natten(z3, bias_tbl, qg, kg, *, rq=32):
    B, T, _ = z3.shape
    rows = T // GRID_W
    assert rows >= WIN_ROWS and rows % rq == 0
    tq = rq * GRID_W
    hd = NA_HD
    est = 2 * 2 * T * hd * 2 + T * hd * 2 + 2 * WIN_ROWS * GRID_W * WIN_ROWS * GRID_W * 4 + 4 * tq * hd * 2
    return pl.pallas_call(
        functools.partial(_natten_kernel, rows=rows, rq=rq),
        grid=(B, NA_H, rows // rq),
        in_specs=[
            pl.BlockSpec((None, tq, hd), lambda b, h, j: (b, j, Z_AQ // hd + h)),
            pl.BlockSpec((None, T, hd), lambda b, h, j: (b, 0, Z_AK // hd + h)),
            pl.BlockSpec((None, T, hd), lambda b, h, j: (b, 0, Z_AV // hd + h)),
            pl.BlockSpec((None, WIN_ROWS, GRID_W, WIN_ROWS * GRID_W), lambda b, h, j: (h, 0, 0, 0)),
            pl.BlockSpec((1, hd), lambda b, h, j: (0, 0)),
            pl.BlockSpec((1, hd), lambda b, h, j: (0, 0)),
        ],
        out_specs=pl.BlockSpec((None, tq, hd), lambda b, h, j: (b, j, h)),
        out_shape=jax.ShapeDtypeStruct((B, T, NA_W), BF16),
        scratch_shapes=[pltpu.VMEM((T, hd), BF16)],
        compiler_params=pltpu.CompilerParams(
            dimension_semantics=("parallel", "parallel", "arbitrary"), vmem_limit_bytes=_vmem(est)),
        name="natten",
    )(z3, z3, z3, bias_tbl, qg, kg)


def natten_bias_table(rpb):
    c = jnp.arange(GRID_W, dtype=jnp.int32)
    c_start = jnp.clip(c - WIN_COLS // 2, 0, GRID_W - WIN_COLS)
    valid = (c[None, :] >= c_start[:, None]) & (c[None, :] < c_start[:, None] + WIN_COLS)
    dc = jnp.clip(c[None, :] - c[:, None], -(WIN_COLS - 1), WIN_COLS - 1) + (WIN_COLS - 1)
    bias_c = jnp.where(valid, rpb.astype(F32)[:, :, dc], NEG_INF)
    dr = jnp.arange(WIN_ROWS)[:, None] + jnp.arange(WIN_ROWS)[None, :]
    tbl = bias_c[:, dr]
    return tbl.transpose(0, 1, 3, 2, 4).reshape(NA_H, WIN_ROWS, GRID_W, WIN_ROWS * GRID_W)


def _outproj_kernel(og_ref, on_ref, an_ref, w_ref, h_ref, o_ref):
    na = _rms(on_ref[...].astype(F32), an_ref[...]).astype(BF16)
    acc = jnp.dot(og_ref[...], w_ref[0:GLA_VW, :], preferred_element_type=F32)
    acc += jnp.dot(na, w_ref[GLA_VW:GLA_VW + NA_W, :], preferred_element_type=F32)
    o_ref[...] = h_ref[...] + acc


def outproj(o_gla, o_na, an, w_o, h, *, tm=512):
    n = h.shape[0]
    est = 2 * 2 * tm * 1024 * 2 + 2 * D_MODEL * D_MODEL * 2 + 4 * tm * D_MODEL * 4 + tm * D_MODEL * 4
    return pl.pallas_call(
        _outproj_kernel,
        grid=(n // tm,),
        in_specs=[
            pl.BlockSpec((tm, GLA_VW), lambda i: (i, 0)),
            pl.BlockSpec((tm, NA_W), lambda i: (i, 0)),
            pl.BlockSpec((1, NA_W), lambda i: (0, 0)),
            pl.BlockSpec((D_MODEL, D_MODEL), lambda i: (0, 0)),
            pl.BlockSpec((tm, D_MODEL), lambda i: (i, 0)),
        ],
        out_specs=pl.BlockSpec((tm, D_MODEL), lambda i: (i, 0)),
        out_shape=jax.ShapeDtypeStruct((n, D_MODEL), F32),
        compiler_params=pltpu.CompilerParams(
            dimension_semantics=("parallel",), vmem_limit_bytes=_vmem(est)),
        name="outproj",
    )(o_gla, o_na, an, w_o, h)


FFN_TF = 512


def _ffn_kernel(h_ref, g_ref, wgu_ref, wd_ref, o_ref, u_scr):
    @pl.when(pl.program_id(1) == 0)
    def _():
        h = h_ref[...]
        u_scr[...] = _rms(h, g_ref[...]).astype(BF16)
        o_ref[...] = h

    tf = wd_ref.shape[0]
    ab = jnp.dot(u_scr[...], wgu_ref[...], preferred_element_type=F32)
    hid = (_silu(ab[:, :tf]) * ab[:, tf:]).astype(BF16)
    o_ref[...] += jnp.dot(hid, wd_ref[...], preferred_element_type=F32)


def ffn_gate_up_tiles(wg, wu, tf=FFN_TF):
    d, f = wg.shape
    return jnp.stack([wg.reshape(d, f // tf, tf), wu.reshape(d, f // tf, tf)], axis=2).reshape(d, 2 * f)


def ffn_dense(h, g, wgu, wd, *, tm=1024, tf=FFN_TF):
    n = h.shape[0]
    est = 4 * tm * D_MODEL * 4 + tm * D_MODEL * 2 + 2 * 3 * D_MODEL * tf * 2 + 3 * tm * tf * 4
    return pl.pallas_call(
        _ffn_kernel,
        grid=(n // tm, FF_DIM // tf),
        in_specs=[
            pl.BlockSpec((tm, D_MODEL), lambda i, f: (i, 0)),
            pl.BlockSpec((1, D_MODEL), lambda i, f: (0, 0)),
            pl.BlockSpec((D_MODEL, 2 * tf), lambda i, f: (0, f)),
            pl.BlockSpec((tf, D_MODEL), lambda i, f: (f, 0)),
        ],
        out_specs=pl.BlockSpec((tm, D_MODEL), lambda i, f: (i, 0)),
        out_shape=jax.ShapeDtypeStruct((n, D_MODEL), F32),
        scratch_shapes=[pltpu.VMEM((tm, D_MODEL), BF16)],
        compiler_params=pltpu.CompilerParams(
            dimension_semantics=("parallel", "arbitrary"), vmem_limit_bytes=_vmem(est)),
        name="ffn_dense",
    )(h, g, wgu, wd)


REC_E, REC_W, REC_RANK = 0, 2, 4
MOE_TOP_K = 2


def _route_kernel(h_ref, g_ref, wr_ref, rec_ref, cnt_ref, carry_scr):
    tm = h_ref.shape[0]
    lane = lax.broadcasted_iota(jnp.int32, (tm, LANES), 1)

    @pl.when(pl.program_id(0) == 0)
    def _():
        carry_scr[...] = jnp.zeros_like(carry_scr)

    u = _rms(h_ref[...], g_ref[...]).astype(BF16)
    logits = jnp.dot(u, wr_ref[...], preferred_element_type=F32)
    lg = jnp.where(lane < N_EXP, logits, -jnp.inf)
    v1 = jnp.max(lg, axis=-1, keepdims=True)
    i1 = jnp.min(jnp.where(lg == v1, lane, LANES), axis=-1, keepdims=True)
    lg2 = jnp.where(lane == i1, -jnp.inf, lg)
    v2 = jnp.max(lg2, axis=-1, keepdims=True)
    i2 = jnp.min(jnp.where(lg2 == v2, lane, LANES), axis=-1, keepdims=True)
    e2 = jnp.exp(v2 - v1)
    w1 = 1.0 / (1.0 + e2)
    w2 = e2 / (1.0 + e2)
    onehot = jnp.where(lane == i1, 1.0, 0.0) + jnp.where(lane == i2, 1.0, 0.0)
    incl = _cumsum_rows(onehot, tm)
    before = incl - onehot + carry_scr[0:1, :]
    r1 = jnp.sum(jnp.where(lane == i1, before, 0.0), axis=-1, keepdims=True)
    r2 = jnp.sum(jnp.where(lane == i2, before, 0.0), axis=-1, keepdims=True)
    rec = jnp.where(lane == REC_E, i1.astype(F32), 0.0)
    rec = jnp.where(lane == REC_E + 1, i2.astype(F32), rec)
    rec = jnp.where(lane == REC_W, w1, rec)
    rec = jnp.where(lane == REC_W + 1, w2, rec)
    rec = jnp.where(lane == REC_RANK, r1, rec)
    rec = jnp.where(lane == REC_RANK + 1, r2, rec)
    rec_ref[...] = rec
    carry_scr[...] = carry_scr[...] + incl[tm - 1:tm, :]
    cnt_ref[...] = carry_scr[...]


def moe_route(h, g, wr, *, tm=512):
    n = h.shape[0]
    est = 2 * tm * D_MODEL * 4 + D_MODEL * LANES * 4 + 8 * tm * LANES * 4
    return pl.pallas_call(
        _route_kernel,
        grid=(n // tm,),
        in_specs=[
            pl.BlockSpec((tm, D_MODEL), lambda i: (i, 0)),
            pl.BlockSpec((1, D_MODEL), lambda i: (0, 0)),
            pl.BlockSpec((D_MODEL, LANES), lambda i: (0, 0)),
        ],
        out_specs=[pl.BlockSpec((tm, LANES), lambda i: (i, 0)), pl.BlockSpec((8, LANES), lambda i: (0, 0))],
        out_shape=[jax.ShapeDtypeStruct((n, LANES), F32), jax.ShapeDtypeStruct((8, LANES), F32)],
        scratch_shapes=[pltpu.VMEM((8, LANES), F32)],
        compiler_params=pltpu.CompilerParams(
            dimension_semantics=("arbitrary",), vmem_limit_bytes=_vmem(est)),
        name="moe_route",
    )(h, g, wr)


def _idx_copy(idx_hbm, idx_smem, sem, step, slot):
    return pltpu.make_async_copy(idx_hbm.at[pl.ds(step, 1)], idx_smem.at[pl.ds(slot, 1)], sem.at[slot])


def _dispatch_kernel(pad_ref, h_ref, g_ref, idx_hbm, xs_ref, ubuf, zrow, idx_smem, row_sem, idx_sem, pad_sem):
    tm = h_ref.shape[0]
    i = pl.program_id(0)
    nsteps = pl.num_programs(0)
    slot = i % 2

    def row_copy(s, r, dst_row):
        return pltpu.make_async_copy(ubuf.at[s, pl.ds(r, 1)], xs_ref.at[pl.ds(dst_row, 1)], row_sem.at[s])

    def wait_rows(s):
        def body(r, c):
            for _ in range(MOE_TOP_K):
                row_copy(s, 0, 0).wait()
            return c
        lax.fori_loop(0, tm, body, 0, unroll=16)

    @pl.when(i == 0)
    def _():
        _idx_copy(idx_hbm, idx_smem, idx_sem, 0, 0).start()

    _idx_copy(idx_hbm, idx_smem, idx_sem, i, slot).wait()

    @pl.when(i + 1 < nsteps)
    def _():
        _idx_copy(idx_hbm, idx_smem, idx_sem, i + 1, 1 - slot).start()

    @pl.when(i >= 2)
    def _():
        wait_rows(slot)

    ubuf[slot] = _rms(h_ref[...], g_ref[...])

    def issue(r, c):
        for k in range(MOE_TOP_K):
            row_copy(slot, r, idx_smem[slot, MOE_TOP_K * r + k]).start()
        return c
    lax.fori_loop(0, tm, issue, 0, unroll=8)

    @pl.when(i == nsteps - 1)
    def _():
        zrow[...] = jnp.zeros_like(zrow)
        for e in range(N_EXP):
            def pad_copy(j, e=e):
                return pltpu.make_async_copy(zrow.at[pl.ds(0, 1)], xs_ref.at[pl.ds(pad_ref[e] + j, 1)], pad_sem.at[0])

            def start(j, c, pad_copy=pad_copy):
                pad_copy(j).start()
                return c

            def wait(j, c, pad_copy=pad_copy):
                pad_copy(j).wait()
                return c
            lax.fori_loop(0, pad_ref[N_EXP + e], start, 0)
            lax.fori_loop(0, pad_ref[N_EXP + e], wait, 0)
        wait_rows(slot)

        @pl.when(nsteps >= 2)
        def _():
            wait_rows(1 - slot)


def moe_dispatch(h, g, idx, padinfo, n_sorted, *, tm):
    n = h.shape[0]
    est = 2 * tm * D_MODEL * 4 + 2 * tm * D_MODEL * 4 + 8 * D_MODEL * 4
    return pl.pallas_call(
        _dispatch_kernel,
        grid_spec=pltpu.PrefetchScalarGridSpec(
            num_scalar_prefetch=1,
            grid=(n // tm,),
            in_specs=[
                pl.BlockSpec((tm, D_MODEL), lambda i, pad: (i, 0)),
                pl.BlockSpec((1, D_MODEL), lambda i, pad: (0, 0)),
                pl.BlockSpec(memory_space=pl.ANY),
            ],
            out_specs=pl.BlockSpec(memory_space=pl.ANY),
            scratch_shapes=[
                pltpu.VMEM((2, tm, D_MODEL), F32),
                pltpu.VMEM((8, D_MODEL), F32),
                pltpu.SMEM((2, MOE_TOP_K * tm), jnp.int32),
                pltpu.SemaphoreType.DMA((2,)),
                pltpu.SemaphoreType.DMA((2,)),
                pltpu.SemaphoreType.DMA((1,)),
            ],
        ),
        out_shape=jax.ShapeDtypeStruct((n_sorted, D_MODEL), F32),
        compiler_params=pltpu.CompilerParams(
            dimension_semantics=("arbitrary",), vmem_limit_bytes=_vmem(est)),
        name="moe_dispatch",
    )(padinfo, h, g, idx)


def _experts_kernel(be_ref, nu_ref, x_ref, wg_ref, wu_ref, wd_ref, y_ref):
    del be_ref
    used = pl.program_id(0) < nu_ref[0]

    @pl.when(used)
    def _():
        x = x_ref[...].astype(BF16)
        a = jnp.dot(x, wg_ref[...], preferred_element_type=F32)
        b = jnp.dot(x, wu_ref[...], preferred_element_type=F32)
        hid = (_silu(a) * b).astype(BF16)
        y_ref[...] = jnp.dot(hid, wd_ref[...], preferred_element_type=F32)

    @pl.when(jnp.logical_not(used))
    def _():
        y_ref[...] = jnp.zeros_like(y_ref)


def moe_experts(block_expert, n_used, xs, wg, wu, wd, *, bm):
    n_sorted = xs.shape[0]
    est = 4 * bm * D_MODEL * 4 + 2 * 3 * D_MODEL * EXP_DIM * 2 + 3 * bm * EXP_DIM * 4 + bm * D_MODEL * 2
    row_blk = lambda i, be, nu: (jnp.where(i < nu[0], i, 0), 0)
    return pl.pallas_call(
        _experts_kernel,
        grid_spec=pltpu.PrefetchScalarGridSpec(
            num_scalar_prefetch=2,
            grid=(n_sorted // bm,),
            in_specs=[
                pl.BlockSpec((bm, D_MODEL), row_blk),
                pl.BlockSpec((None, D_MODEL, EXP_DIM), lambda i, be, nu: (be[i], 0, 0)),
                pl.BlockSpec((None, D_MODEL, EXP_DIM), lambda i, be, nu: (be[i], 0, 0)),
                pl.BlockSpec((None, EXP_DIM, D_MODEL), lambda i, be, nu: (be[i], 0, 0)),
            ],
            out_specs=pl.BlockSpec((bm, D_MODEL), lambda i, be, nu: (i, 0)),
        ),
        out_shape=jax.ShapeDtypeStruct((n_sorted, D_MODEL), F32),
        compiler_params=pltpu.CompilerParams(
            dimension_semantics=("arbitrary",), vmem_limit_bytes=_vmem(est)),
        name="moe_experts",
    )(block_expert, n_used, xs, wg, wu, wd)


def _combine_kernel(h_ref, rec_ref, idx_hbm, ys_ref, o_ref, ybuf, idx_smem, row_sem, idx_sem):
    tm = h_ref.shape[0]
    i = pl.program_id(0)
    nsteps = pl.num_programs(0)
    slot = i % 2

    def row_copy(s, k, r, src_row):
        return pltpu.make_async_copy(ys_ref.at[pl.ds(src_row, 1)], ybuf.at[s, k, pl.ds(r, 1)], row_sem.at[s])

    def issue_rows(s):
        def body(r, c):
            for k in range(MOE_TOP_K):
                row_copy(s, k, r, idx_smem[s, MOE_TOP_K * r + k]).start()
            return c
        lax.fori_loop(0, tm, body, 0, unroll=8)

    @pl.when(i == 0)
    def _():
        _idx_copy(idx_hbm, idx_smem, idx_sem, 0, 0).start()
        _idx_copy(idx_hbm, idx_smem, idx_sem, 0, 0).wait()
        issue_rows(0)

        @pl.when(nsteps >= 2)
        def _():
            _idx_copy(idx_hbm, idx_smem, idx_sem, 1, 1).start()

    @pl.when(i + 1 < nsteps)
    def _():
        _idx_copy(idx_hbm, idx_smem, idx_sem, i + 1, 1 - slot).wait()
        issue_rows(1 - slot)

    def wait_body(r, c):
        for k in range(MOE_TOP_K):
            row_copy(slot, k, 0, 0).wait()
        return c
    lax.fori_loop(0, tm, wait_body, 0, unroll=16)

    @pl.when(i + 2 < nsteps)
    def _():
        _idx_copy(idx_hbm, idx_smem, idx_sem, i + 2, slot).start()

    rec = rec_ref[...]
    w1 = rec[:, REC_W:REC_W + 1]
    w2 = rec[:, REC_W + 1:REC_W + 2]
    o_ref[...] = h_ref[...] + w1 * ybuf[slot, 0] + w2 * ybuf[slot, 1]


def moe_combine(h, rec, idx, ys, *, tm):
    n = h.shape[0]
    est = 4 * tm * D_MODEL * 4 + 2 * tm * LANES * 4 + 2 * MOE_TOP_K * tm * D_MODEL * 4
    return pl.pallas_call(
        _combine_kernel,
        grid=(n // tm,),
        in_specs=[
            pl.BlockSpec((tm, D_MODEL), lambda i: (i, 0)),
            pl.BlockSpec((tm, LANES), lambda i: (i, 0)),
            pl.BlockSpec(memory_space=pl.ANY),
            pl.BlockSpec(memory_space=pl.ANY),
        ],
        out_specs=pl.BlockSpec((tm, D_MODEL), lambda i: (i, 0)),
        out_shape=jax.ShapeDtypeStruct((n, D_MODEL), F32),
        scratch_shapes=[
            pltpu.VMEM((2, MOE_TOP_K, tm, D_MODEL), F32),
            pltpu.SMEM((2, MOE_TOP_K * tm), jnp.int32),
            pltpu.SemaphoreType.DMA((2,)),
            pltpu.SemaphoreType.DMA((2,)),
        ],
        compiler_params=pltpu.CompilerParams(
            dimension_semantics=("arbitrary",), vmem_limit_bytes=_vmem(est)),
        name="moe_combine",
    )(h, rec, idx, ys)


def ffn_moe(h, g, wr, wg, wu, wd, *, bm=512, tm=256):
    n = h.shape[0]
    n_sorted = MOE_TOP_K * n + N_EXP * bm
    rec, cnt = moe_route(h, g, wr)
    expert = rec[:, REC_E:REC_E + MOE_TOP_K].astype(jnp.int32)
    rank = rec[:, REC_RANK:REC_RANK + MOE_TOP_K].astype(jnp.int32)
    counts = cnt[0, :N_EXP].astype(jnp.int32)
    padded = (counts + bm - 1) // bm * bm
    pend = jnp.cumsum(padded)
    pstart = pend - padded
    dest = jnp.sum(jnp.where(expert[..., None] == jnp.arange(N_EXP), pstart, 0), axis=-1) + rank
    idx = dest.reshape(n // tm, MOE_TOP_K * tm)
    padinfo = jnp.concatenate([pstart + counts, padded - counts]).astype(jnp.int32)
    n_used = (pend[-1:] // bm).astype(jnp.int32)
    blk_start = jnp.arange(n_sorted // bm, dtype=jnp.int32) * bm
    block_expert = jnp.minimum(jnp.sum(blk_start[:, None] >= pend[None, :], axis=1), N_EXP - 1).astype(jnp.int32)
    xs = moe_dispatch(h, g, idx, padinfo, n_sorted, tm=tm)
    ys = moe_experts(block_expert, n_used, xs, wg, wu, wd, bm=bm)
    return moe_combine(h, rec, idx, ys, tm=tm)


def _ple_kernel(h_ref, p_ref, gi_ref, wgate_ref, wp_ref, go_ref, o_ref):
    h = h_ref[...]
    hn = _rms(h, gi_ref[...]).astype(BF16)
    gate = jax.nn.sigmoid(jnp.dot(hn, wgate_ref[...], preferred_element_type=F32))
    e = jnp.dot(p_ref[...].astype(BF16), wp_ref[...], preferred_element_type=F32)
    o_ref[...] = h + gate * _rms(e, go_ref[...])


def ple(h, p_all, layer, gi, wgate, wp, go, *, tm=512):
    n = h.shape[0]
    est = 4 * tm * D_MODEL * 4 + 2 * D_MODEL * D_MODEL * 2 + 2 * PLE_DIM * D_MODEL * 2 + 3 * tm * D_MODEL * 4
    return pl.pallas_call(
        _ple_kernel,
        grid=(n // tm,),
        in_specs=[
            pl.BlockSpec((tm, D_MODEL), lambda i: (i, 0)),
            pl.BlockSpec((None, tm, PLE_DIM), lambda i: (layer, i, 0)),
            pl.BlockSpec((1, D_MODEL), lambda i: (0, 0)),
            pl.BlockSpec((D_MODEL, D_MODEL), lambda i: (0, 0)),
            pl.BlockSpec((PLE_DIM, D_MODEL), lambda i: (0, 0)),
            pl.BlockSpec((1, D_MODEL), lambda i: (0, 0)),
        ],
        out_specs=pl.BlockSpec((tm, D_MODEL), lambda i: (i, 0)),
        out_shape=jax.ShapeDtypeStruct((n, D_MODEL), F32),
        compiler_params=pltpu.CompilerParams(
            dimension_semantics=("parallel",), vmem_limit_bytes=_vmem(est)),
        name="ple",
    )(h, p_all, gi, wgate, wp, go)


def _prep_layer(i, P):
    w_in = P["w_in"][i]
    lr0 = Z_AQ
    w_in_r = jnp.concatenate([w_in[:, :lr0], w_in[:, lr0 + 2 * GLA_RANK:]], axis=1).astype(BF16)
    w_lr = jnp.concatenate([w_in[:, lr0:lr0 + 2 * GLA_RANK],
                            jnp.zeros((D_MODEL, LR_W - 2 * GLA_RANK), w_in.dtype)], axis=1).astype(BF16)
    zpad = jnp.zeros((LR_W, GLA_KW), F32)
    wgf = zpad.at[0:GLA_RANK].set(P["w_gate_f"][i]).astype(BF16)
    wgb = zpad.at[GLA_RANK:2 * GLA_RANK].set(P["w_gate_b"][i]).astype(BF16)
    L = dict(
        g_mix=P["g_mix"][i][None], w_in=w_in_r, w_lr=w_lr, wgf=wgf, wgb=wgb,
        bgf=P["b_gate_f"][i][None], bgb=P["b_gate_b"][i][None],
        gla_norm=P["gla_norm"][i].reshape(GLA_H, 1, GLA_DV),
        q_norm=P["q_norm"][i][None], k_norm=P["k_norm"][i][None],
        bias_tbl=natten_bias_table(P["rpb"][i]), attn_norm=P["attn_norm"][i][None],
        w_o=P["w_o"][i].astype(BF16), g_ffn=P["g_ffn"][i][None],
        g_ple_in=P["g_ple_in"][i][None], w_ple_gate=P["w_ple_gate"][i].astype(BF16),
        w_ple=P["w_ple"][i].astype(BF16), g_ple_out=P["g_ple_out"][i][None],
    )
    j = i // 2
    if i % 2 == 0:
        L.update(wgu=ffn_gate_up_tiles(P["w_ff_gate"][j], P["w_ff_up"][j]).astype(BF16),
                 wd=P["w_ff_down"][j].astype(BF16))
    else:
        wr = jnp.zeros((D_MODEL, LANES), F32).at[:, :N_EXP].set(P["w_router"][j]).astype(BF16)
        L.update(wr=wr, wg=P["w_exp_gate"][j].astype(BF16), wu=P["w_exp_up"][j].astype(BF16),
                 wd=P["w_exp_down"][j].astype(BF16))
    return L


def _trunk(x, p, layers):
    B, T, _ = x.shape
    n = B * T
    h = x.reshape(n, D_MODEL)
    p_all = p.reshape(DEPTH, n, PLE_DIM)
    for i, L in enumerate(layers):
        z, zl = inproj(h, L["g_mix"], L["w_in"], L["w_lr"])
        z3 = z.reshape(B, T, NZ)
        zl3 = zl.reshape(B, T, LR_W)
        o_f = gla_direction(z3, zl3, L["wgf"], L["bgf"], reverse=False)
        o_gla = gla_direction(z3, zl3, L["wgb"], L["bgb"], reverse=True, o_other=o_f, gn=L["gla_norm"])
        o_na = natten(z3, L["bias_tbl"], L["q_norm"], L["k_norm"])
        h = outproj(o_gla.reshape(n, GLA_VW), o_na.reshape(n, NA_W), L["attn_norm"], L["w_o"], h)
        if i % 2 == 0:
            h = ffn_dense(h, L["g_ffn"], L["wgu"], L["wd"])
        else:
            h = ffn_moe(h, L["g_ffn"], L["wr"], L["wg"], L["wu"], L["wd"])
        h = ple(h, p_all, i, L["g_ple_in"], L["w_ple_gate"], L["w_ple"], L["g_ple_out"])
    return h.reshape(B, T, D_MODEL)


def kernel(x_prompt, x_sample, p_prompt, p_sample, g_mix, w_in, w_gate_f, b_gate_f, w_gate_b, b_gate_b, gla_norm, q_norm, k_norm, rpb, attn_norm, w_o, g_ffn, w_ff_gate, w_ff_up, w_ff_down, w_router, w_exp_gate, w_exp_up, w_exp_down, g_ple_in, w_ple_gate, w_ple, g_ple_out):
    P = dict(g_mix=g_mix, w_in=w_in, w_gate_f=w_gate_f, b_gate_f=b_gate_f, w_gate_b=w_gate_b,
             b_gate_b=b_gate_b, gla_norm=gla_norm, q_norm=q_norm, k_norm=k_norm, rpb=rpb,
             attn_norm=attn_norm, w_o=w_o, g_ffn=g_ffn, w_ff_gate=w_ff_gate, w_ff_up=w_ff_up,
             w_ff_down=w_ff_down, w_router=w_router, w_exp_gate=w_exp_gate, w_exp_up=w_exp_up,
             w_exp_down=w_exp_down, g_ple_in=g_ple_in, w_ple_gate=w_ple_gate, w_ple=w_ple,
             g_ple_out=g_ple_out)
    layers = [_prep_layer(i, P) for i in range(DEPTH)]
    return (_trunk(x_prompt, p_prompt, layers), _trunk(x_sample, p_sample, layers))
```

```python
import functools

import jax
import jax.numpy as jnp
from jax import lax
from jax.experimental import pallas as pl
from jax.experimental.pallas import tpu as pltpu

F32 = jnp.float32
BF16 = jnp.bfloat16

D_MODEL = 2048
DEPTH = 4
GRID_W = 64
GLA_H = 4
GLA_VW = 1024
GLA_KW = 512
GLA_DK = 128
GLA_DV = 256
GLA_RANK = 16
GLA_INV_NORMALIZER = 1.0 / 16.0
GLA_CHUNK = 64
NA_W = 1024
NA_H = 8
NA_HD = 128
WIN_ROWS = 8
WIN_COLS = 16
FF_DIM = 4096
N_EXP = 8
EXP_DIM = 1024
PLE_DIM = 256
EPS = 1e-6
NEG_INF = -1e30

Z_GQ, Z_GK, Z_GV, Z_GG = 0, 512, 1024, 2048
Z_AQ, Z_AK, Z_AV, NZ = 3072, 4096, 5120, 6144
LR_W = 256

V7X_VMEM_BYTES = 64 * 1024 * 1024
LANES = 128


def _vmem(nbytes):
    return int(min(nbytes * 3 // 2 + (4 << 20), V7X_VMEM_BYTES - (6 << 20)))


def _rms(x, g):
    ms = jnp.mean(x * x, axis=-1, keepdims=True)
    return x * lax.rsqrt(ms + EPS) * g


def _silu(x):
    return x * jax.nn.sigmoid(x)


def _inproj_kernel(h_ref, g_ref, w_ref, wl_ref, z_ref, zl_ref, u_scr):
    j = pl.program_id(1)

    @pl.when(j == 0)
    def _():
        u = _rms(h_ref[...], g_ref[...]).astype(BF16)
        u_scr[...] = u
        zl_ref[...] = jnp.dot(u, wl_ref[...], preferred_element_type=F32).astype(zl_ref.dtype)

    @pl.when(j > 0)
    def _():
        z_ref[...] = jnp.dot(u_scr[...], w_ref[...], preferred_element_type=F32).astype(z_ref.dtype)


def inproj(h, g, w, wl, *, tm=1024, tn=1536):
    n = h.shape[0]
    nmain = NZ // tn
    est = (2 * tm * D_MODEL * 4 + 2 * D_MODEL * tn * 2 + 2 * tm * tn * 2 + tm * D_MODEL * 2 + tm * tn * 4
           + 2 * D_MODEL * LR_W * 2 + 2 * tm * LR_W * 2)
    main = lambda j: jnp.maximum(j - 1, 0)
    return pl.pallas_call(
        _inproj_kernel,
        grid=(n // tm, nmain + 1),
        in_specs=[
            pl.BlockSpec((tm, D_MODEL), lambda i, j: (i, 0)),
            pl.BlockSpec((1, D_MODEL), lambda i, j: (0, 0)),
            pl.BlockSpec((D_MODEL, tn), lambda i, j: (0, main(j))),
            pl.BlockSpec((D_MODEL, LR_W), lambda i, j: (0, 0)),
        ],
        out_specs=[pl.BlockSpec((tm, tn), lambda i, j: (i, main(j))),
                   pl.BlockSpec((tm, LR_W), lambda i, j: (i, 0))],
        out_shape=[jax.ShapeDtypeStruct((n, NZ), BF16), jax.ShapeDtypeStruct((n, LR_W), BF16)],
        scratch_shapes=[pltpu.VMEM((tm, D_MODEL), BF16)],
        compiler_params=pltpu.CompilerParams(
            dimension_semantics=("parallel", "arbitrary"), vmem_limit_bytes=_vmem(est)),
        name="inproj",
    )(h, g, w, wl)


def _cumsum_rows(x, seg):
    idx = lax.broadcasted_iota(jnp.int32, x.shape, 0) & (seg - 1)
    s = 1
    while s < seg:
        x = x + jnp.where(idx >= s, pltpu.roll(x, s, 0), 0.0)
        s *= 2
    return x


def _log_sigmoid(x):
    return jnp.minimum(x, 0.0) - jnp.log(1.0 + jnp.exp(-jnp.abs(x)))


def _split_bf16(x):
    hi = x.astype(BF16)
    r1 = x - hi.astype(F32)
    mid = r1.astype(BF16)
    lo = (r1 - mid.astype(F32)).astype(BF16)
    return hi, mid, lo


def _gla_kernel(*refs, reverse, final, tb):
    if final:
        q_ref, k_ref, v_ref, lr_ref, wg_ref, bg_ref, of_ref, gg_ref, gn_ref, o_ref, s_scr = refs
    else:
        q_ref, k_ref, v_ref, lr_ref, wg_ref, bg_ref, o_ref, s_scr = refs
    C = GLA_CHUNK
    nchunk = tb // C

    @pl.when(pl.program_id(2) == 0)
    def _():
        s_scr[...] = jnp.zeros_like(s_scr)

    x = jnp.dot(lr_ref[...], wg_ref[...], preferred_element_type=F32) + bg_ref[...]
    la_parts = _split_bf16(_log_sigmoid(x) * GLA_INV_NORMALIZER)

    row = lax.broadcasted_iota(jnp.int32, (C, C), 0)
    col = lax.broadcasted_iota(jnp.int32, (C, C), 1)
    mask = (col >= row) if reverse else (col <= row)
    tri = jnp.where(mask, 1.0, 0.0).astype(BF16)
    scale = GLA_DK ** -0.5

    order = [(nchunk - 1 - ci) if reverse else ci for ci in range(nchunk)]
    sl = [slice(c * C, (c + 1) * C) for c in range(nchunk)]
    qts, atts, incs, dls = {}, {}, {}, {}
    bs = {c: sum(jnp.dot(tri, part[sl[c]], preferred_element_type=F32) for part in la_parts) for c in order}
    for c in order:
        b = bs[c]
        blast = b[0:1, :] if reverse else b[C - 1:C, :]
        q = q_ref[sl[c], :].astype(F32)
        k = k_ref[sl[c], :].astype(F32)
        qts[c] = (q * jnp.exp(b) * scale).astype(BF16)
        kt = (k * jnp.exp(-b)).astype(BF16)
        ke_t = (k * jnp.exp(blast - b)).T.astype(BF16)
        dls[c] = jnp.broadcast_to(jnp.exp(blast), (8, GLA_DK)).T[:, 0:1]
        att = lax.dot_general(qts[c], kt, (((1,), (1,)), ((), ())), preferred_element_type=F32)
        atts[c] = jnp.where(mask, att, 0.0).astype(BF16)
        incs[c] = jnp.dot(ke_t, v_ref[sl[c], :], preferred_element_type=F32)
    state = s_scr[...]
    states = {}
    for c in order:
        states[c] = state.astype(BF16)
        state = dls[c] * state + incs[c]
    s_scr[...] = state
    for c in order:
        o = (jnp.dot(atts[c], v_ref[sl[c], :], preferred_element_type=F32)
             + jnp.dot(qts[c], states[c], preferred_element_type=F32))
        if final:
            y = _rms(o + of_ref[sl[c], :], gn_ref[...])
            gg = gg_ref[sl[c], :].astype(F32)
            o_ref[sl[c], :] = (y * _silu(gg)).astype(o_ref.dtype)
        else:
            o_ref[sl[c], :] = o


def gla_direction(z3, zl3, wg, bg, *, reverse, o_other=None, gn=None, tb=1024):
    B, T, _ = z3.shape
    nb = T // tb
    final = o_other is not None
    blk = (lambda j: nb - 1 - j) if reverse else (lambda j: j)
    qw, vw = GLA_DK, GLA_DV
    in_specs = [
        pl.BlockSpec((None, tb, qw), lambda b, h, j: (b, blk(j), Z_GQ // qw + h)),
        pl.BlockSpec((None, tb, qw), lambda b, h, j: (b, blk(j), Z_GK // qw + h)),
        pl.BlockSpec((None, tb, vw), lambda b, h, j: (b, blk(j), Z_GV // vw + h)),
        pl.BlockSpec((None, tb, LR_W), lambda b, h, j: (b, blk(j), 0)),
        pl.BlockSpec((LR_W, qw), lambda b, h, j: (0, h)),
        pl.BlockSpec((1, qw), lambda b, h, j: (0, h)),
    ]
    args = [z3, z3, z3, zl3, wg, bg]
    if final:
        in_specs += [
            pl.BlockSpec((None, tb, vw), lambda b, h, j: (b, blk(j), h)),
            pl.BlockSpec((None, tb, vw), lambda b, h, j: (b, blk(j), Z_GG // vw + h)),
            pl.BlockSpec((None, 1, vw), lambda b, h, j: (h, 0, 0)),
        ]
        args += [o_other, z3, gn]
    out_dtype = BF16 if final else F32
    return pl.pallas_call(
        functools.partial(_gla_kernel, reverse=reverse, final=final, tb=tb),
        grid=(B, GLA_H, nb),
        in_specs=in_specs,
        out_specs=pl.BlockSpec((None, tb, vw), lambda b, h, j: (b, blk(j), h)),
        out_shape=jax.ShapeDtypeStruct((B, T, GLA_VW), out_dtype),
        scratch_shapes=[pltpu.VMEM((GLA_DK, GLA_DV), F32)],
        compiler_params=pltpu.CompilerParams(
            dimension_semantics=("parallel", "parallel", "arbitrary"), vmem_limit_bytes=_vmem(8 << 20)),
        name="gla_bwd" if reverse else "gla_fwd",
    )(*args)


def _natten_kernel(q_ref, k_ref, v_ref, bias_ref, qg_ref, kg_ref, o_ref, kn_scr, *, rows, rq):
    T = rows * GRID_W
    KB = 2048
    WK = WIN_ROWS * GRID_W
    j = pl.program_id(2)

    @pl.when(j == 0)
    def _():
        def norm_block(i, carry):
            off = pl.multiple_of(i * KB, KB)
            kk = k_ref[pl.ds(off, KB), :].astype(F32)
            kn_scr[pl.ds(off, KB), :] = _rms(kk, kg_ref[...]).astype(BF16)
            return carry

        lax.fori_loop(0, T // KB, norm_block, 0)

    qn_blk = (_rms(q_ref[...].astype(F32), qg_ref[...]) * (NA_HD ** -0.5)).astype(BF16)
    koffs, dtis, scores = [], [], []
    for i in range(rq):
        r = j * rq + i
        rs = jnp.clip(r - WIN_ROWS // 2, 0, rows - WIN_ROWS)
        dtis.append(rs - r + (WIN_ROWS - 1))
        koffs.append(pl.multiple_of(rs * GRID_W, GRID_W))
        kw = kn_scr[pl.ds(koffs[i], WK), :]
        scores.append(lax.dot_general(qn_blk[i * GRID_W:(i + 1) * GRID_W], kw, (((1,), (1,)), ((), ())),
                                      preferred_element_type=F32))
    probs, sums = [], []
    for i in range(rq):
        s = scores[i] + bias_ref[dtis[i]]
        p = jnp.exp(s - jnp.max(s, axis=-1, keepdims=True))
        sums.append(jnp.sum(p, axis=-1, keepdims=True))
        probs.append(p.astype(BF16))
    for i in range(rq):
        vw = v_ref[pl.ds(koffs[i], WK), :]
        o = jnp.dot(probs[i], vw, preferred_element_type=F32) / sums[i]
        o_ref[i * GRID_W:(i + 1) * GRID_W, :] = o.astype(o_ref.dtype)


def natten(z3, bias_tbl, qg, kg, *, rq=32):
    B, T, _ = z3.shape
    rows = T // GRID_W
    assert rows >= WIN_ROWS and rows % rq == 0
    tq = rq * GRID_W
    hd = NA_HD
    est = 2 * 2 * T * hd * 2 + T * hd * 2 + 2 * WIN_ROWS * GRID_W * WIN_ROWS * GRID_W * 4 + 4 * tq * hd * 2
    return pl.pallas_call(
        functools.partial(_natten_kernel, rows=rows, rq=rq),
        grid=(B, NA_H, rows // rq),
        in_specs=[
            pl.BlockSpec((None, tq, hd), lambda b, h, j: (b, j, Z_AQ // hd + h)),
            pl.BlockSpec((None, T, hd), lambda b, h, j: (b, 0, Z_AK // hd + h)),
            pl.BlockSpec((None, T, hd), lambda b, h, j: (b, 0, Z_AV // hd + h)),
            pl.BlockSpec((None, WIN_ROWS, GRID_W, WIN_ROWS * GRID_W), lambda b, h, j: (h, 0, 0, 0)),
            pl.BlockSpec((1, hd), lambda b, h, j: (0, 0)),
            pl.BlockSpec((1, hd), lambda b, h, j: (0, 0)),
        ],
        out_specs=pl.BlockSpec((None, tq, hd), lambda b, h, j: (b, j, h)),
        out_shape=jax.ShapeDtypeStruct((B, T, NA_W), BF16),
        scratch_shapes=[pltpu.VMEM((T, hd), BF16)],
        compiler_params=pltpu.CompilerParams(
            dimension_semantics=("parallel", "parallel", "arbitrary"), vmem_limit_bytes=_vmem(est)),
        name="natten",
    )(z3, z3, z3, bias_tbl, qg, kg)


def natten_bias_table(rpb):
    c = jnp.arange(GRID_W, dtype=jnp.int32)
    c_start = jnp.clip(c - WIN_COLS // 2, 0, GRID_W - WIN_COLS)
    valid = (c[None, :] >= c_start[:, None]) & (c[None, :] < c_start[:, None] + WIN_COLS)
    dc = jnp.clip(c[None, :] - c[:, None], -(WIN_COLS - 1), WIN_COLS - 1) + (WIN_COLS - 1)
    bias_c = jnp.where(valid, rpb.astype(F32)[:, :, dc], NEG_INF)
    dr = jnp.arange(WIN_ROWS)[:, None] + jnp.arange(WIN_ROWS)[None, :]
    tbl = bias_c[:, dr]
    return tbl.transpose(0, 1, 3, 2, 4).reshape(NA_H, WIN_ROWS, GRID_W, WIN_ROWS * GRID_W)


def _outproj_kernel(og_ref, on_ref, an_ref, w_ref, h_ref, o_ref):
    na = _rms(on_ref[...].astype(F32), an_ref[...]).astype(BF16)
    acc = jnp.dot(og_ref[...], w_ref[0:GLA_VW, :], preferred_element_type=F32)
    acc += jnp.dot(na, w_ref[GLA_VW:GLA_VW + NA_W, :], preferred_element_type=F32)
    o_ref[...] = h_ref[...] + acc


def outproj(o_gla, o_na, an, w_o, h, *, tm=512):
    n = h.shape[0]
    est = 2 * 2 * tm * 1024 * 2 + 2 * D_MODEL * D_MODEL * 2 + 4 * tm * D_MODEL * 4 + tm * D_MODEL * 4
    return pl.pallas_call(
        _outproj_kernel,
        grid=(n // tm,),
        in_specs=[
            pl.BlockSpec((tm, GLA_VW), lambda i: (i, 0)),
            pl.BlockSpec((tm, NA_W), lambda i: (i, 0)),
            pl.BlockSpec((1, NA_W), lambda i: (0, 0)),
            pl.BlockSpec((D_MODEL, D_MODEL), lambda i: (0, 0)),
            pl.BlockSpec((tm, D_MODEL), lambda i: (i, 0)),
        ],
        out_specs=pl.BlockSpec((tm, D_MODEL), lambda i: (i, 0)),
        out_shape=jax.ShapeDtypeStruct((n, D_MODEL), F32),
        compiler_params=pltpu.CompilerParams(
            dimension_semantics=("parallel",), vmem_limit_bytes=_vmem(est)),
        name="outproj",
    )(o_gla, o_na, an, w_o, h)


FFN_TF = 512


def _ffn_kernel(h_ref, g_ref, wg_ref, wu_ref, wd_ref, o_ref, u_scr):
    @pl.when(pl.program_id(1) == 0)
    def _():
        h = h_ref[...]
        u_scr[...] = _rms(h, g_ref[...]).astype(BF16)
        o_ref[...] = h

    u = u_scr[...]
    a = jnp.dot(u, wg_ref[...], preferred_element_type=F32)
    b = jnp.dot(u, wu_ref[...], preferred_element_type=F32)
    hid = (_silu(a) * b).astype(BF16)
    o_ref[...] += jnp.dot(hid, wd_ref[...], preferred_element_type=F32)


def ffn_dense(h, g, wg, wu, wd, *, tm=1024, tf=FFN_TF):
    n = h.shape[0]
    est = 4 * tm * D_MODEL * 4 + tm * D_MODEL * 2 + 2 * 3 * D_MODEL * tf * 2 + 3 * tm * tf * 4
    return pl.pallas_call(
        _ffn_kernel,
        grid=(n // tm, FF_DIM // tf),
        in_specs=[
            pl.BlockSpec((tm, D_MODEL), lambda i, f: (i, 0)),
            pl.BlockSpec((1, D_MODEL), lambda i, f: (0, 0)),
            pl.BlockSpec((D_MODEL, tf), lambda i, f: (0, f)),
            pl.BlockSpec((D_MODEL, tf), lambda i, f: (0, f)),
            pl.BlockSpec((tf, D_MODEL), lambda i, f: (f, 0)),
        ],
        out_specs=pl.BlockSpec((tm, D_MODEL), lambda i, f: (i, 0)),
        out_shape=jax.ShapeDtypeStruct((n, D_MODEL), F32),
        scratch_shapes=[pltpu.VMEM((tm, D_MODEL), BF16)],
        compiler_params=pltpu.CompilerParams(
            dimension_semantics=("parallel", "arbitrary"), vmem_limit_bytes=_vmem(est)),
        name="ffn_dense",
    )(h, g, wg, wu, wd)


REC_E, REC_W, REC_RANK = 0, 2, 4
MOE_TOP_K = 2


def _route_kernel(h_ref, g_ref, wr_ref, rec_ref, cnt_ref, carry_scr):
    tm = h_ref.shape[0]
    lane = lax.broadcasted_iota(jnp.int32, (tm, LANES), 1)

    @pl.when(pl.program_id(0) == 0)
    def _():
        carry_scr[...] = jnp.zeros_like(carry_scr)

    u = _rms(h_ref[...], g_ref[...]).astype(BF16)
    logits = jnp.dot(u, wr_ref[...], preferred_element_type=F32)
    lg = jnp.where(lane < N_EXP, logits, -jnp.inf)
    v1 = jnp.max(lg, axis=-1, keepdims=True)
    i1 = jnp.min(jnp.where(lg == v1, lane, LANES), axis=-1, keepdims=True)
    lg2 = jnp.where(lane == i1, -jnp.inf, lg)
    v2 = jnp.max(lg2, axis=-1, keepdims=True)
    i2 = jnp.min(jnp.where(lg2 == v2, lane, LANES), axis=-1, keepdims=True)
    e2 = jnp.exp(v2 - v1)
    w1 = 1.0 / (1.0 + e2)
    w2 = e2 / (1.0 + e2)
    onehot = jnp.where(lane == i1, 1.0, 0.0) + jnp.where(lane == i2, 1.0, 0.0)
    incl = _cumsum_rows(onehot, tm)
    before = incl - onehot + carry_scr[0:1, :]
    r1 = jnp.sum(jnp.where(lane == i1, before, 0.0), axis=-1, keepdims=True)
    r2 = jnp.sum(jnp.where(lane == i2, before, 0.0), axis=-1, keepdims=True)
    rec = jnp.where(lane == REC_E, i1.astype(F32), 0.0)
    rec = jnp.where(lane == REC_E + 1, i2.astype(F32), rec)
    rec = jnp.where(lane == REC_W, w1, rec)
    rec = jnp.where(lane == REC_W + 1, w2, rec)
    rec = jnp.where(lane == REC_RANK, r1, rec)
    rec = jnp.where(lane == REC_RANK + 1, r2, rec)
    rec_ref[...] = rec
    carry_scr[...] = carry_scr[...] + incl[tm - 1:tm, :]
    cnt_ref[...] = carry_scr[...]


def moe_route(h, g, wr, *, tm=512):
    n = h.shape[0]
    est = 2 * tm * D_MODEL * 4 + D_MODEL * LANES * 4 + 8 * tm * LANES * 4
    return pl.pallas_call(
        _route_kernel,
        grid=(n // tm,),
        in_specs=[
            pl.BlockSpec((tm, D_MODEL), lambda i: (i, 0)),
            pl.BlockSpec((1, D_MODEL), lambda i: (0, 0)),
            pl.BlockSpec((D_MODEL, LANES), lambda i: (0, 0)),
        ],
        out_specs=[pl.BlockSpec((tm, LANES), lambda i: (i, 0)), pl.BlockSpec((8, LANES), lambda i: (0, 0))],
        out_shape=[jax.ShapeDtypeStruct((n, LANES), F32), jax.ShapeDtypeStruct((8, LANES), F32)],
        scratch_shapes=[pltpu.VMEM((8, LANES), F32)],
        compiler_params=pltpu.CompilerParams(
            dimension_semantics=("arbitrary",), vmem_limit_bytes=_vmem(est)),
        name="moe_route",
    )(h, g, wr)


def _static_slot(slot, fn):
    for s in range(2):
        pl.when(slot == s)(functools.partial(fn, s))


ROWS_PER_ISSUE = 8


def _idx_copy(idx_hbm, idx_smem, sem, step, s):
    n = idx_smem.shape[0] // 2
    return pltpu.make_async_copy(idx_hbm.at[pl.ds(pl.multiple_of(step * n, n), n)],
                                 idx_smem.at[pl.ds(s * n, n)], sem.at[s])


def _dispatch_kernel(pad_ref, h_ref, g_ref, idx_hbm, xs_ref, ubuf, zrow, idx_smem, row_sem, idx_sem, pad_sem):
    tm = h_ref.shape[0]
    i = pl.program_id(0)
    nsteps = pl.num_programs(0)
    n_idx = idx_smem.shape[0] // 2

    def row_copy(s, r, dst_row):
        return pltpu.make_async_copy(ubuf.at[s, pl.ds(r, 1)], xs_ref.at[pl.ds(dst_row, 1)], row_sem.at[s])

    def wait_rows(s):
        def body(r, c):
            for _ in range(MOE_TOP_K):
                row_copy(s, 0, 0).wait()
            return c
        lax.fori_loop(0, tm, body, 0, unroll=16)

    def issue_rows(s):
        def body(j, c):
            r0 = pl.multiple_of(j * ROWS_PER_ISSUE, ROWS_PER_ISSUE)
            for rr in range(ROWS_PER_ISSUE):
                for k in range(MOE_TOP_K):
                    row_copy(s, r0 + rr, idx_smem[s * n_idx + MOE_TOP_K * (r0 + rr) + k]).start()
            return c
        lax.fori_loop(0, tm // ROWS_PER_ISSUE, body, 0)

    def zero_pad_rows():
        zrow[...] = jnp.zeros_like(zrow)
        for e in range(N_EXP):
            def pad_copy(j, e=e):
                return pltpu.make_async_copy(zrow.at[pl.ds(0, 1)], xs_ref.at[pl.ds(pad_ref[e] + j, 1)], pad_sem.at[0])

            def start(j, c, pad_copy=pad_copy):
                pad_copy(j).start()
                return c

            def wait(j, c, pad_copy=pad_copy):
                pad_copy(j).wait()
                return c
            lax.fori_loop(0, pad_ref[N_EXP + e], start, 0)
            lax.fori_loop(0, pad_ref[N_EXP + e], wait, 0)

    def step(s):
        @pl.when(i == 0)
        def _():
            _idx_copy(idx_hbm, idx_smem, idx_sem, 0, s).start()

        _idx_copy(idx_hbm, idx_smem, idx_sem, i, s).wait()

        @pl.when(i + 1 < nsteps)
        def _():
            _idx_copy(idx_hbm, idx_smem, idx_sem, i + 1, 1 - s).start()

        @pl.when(i >= 2)
        def _():
            wait_rows(s)

        ubuf[s] = _rms(h_ref[...], g_ref[...])
        issue_rows(s)

        @pl.when(i == nsteps - 1)
        def _():
            zero_pad_rows()
            wait_rows(s)

            @pl.when(nsteps >= 2)
            def _():
                wait_rows(1 - s)

    _static_slot(i % 2, step)


def moe_dispatch(h, g, idx, padinfo, n_sorted, *, tm):
    n = h.shape[0]
    est = 2 * tm * D_MODEL * 4 + 2 * tm * D_MODEL * 4 + 8 * D_MODEL * 4
    return pl.pallas_call(
        _dispatch_kernel,
        grid_spec=pltpu.PrefetchScalarGridSpec(
            num_scalar_prefetch=1,
            grid=(n // tm,),
            in_specs=[
                pl.BlockSpec((tm, D_MODEL), lambda i, pad: (i, 0)),
                pl.BlockSpec((1, D_MODEL), lambda i, pad: (0, 0)),
                pl.BlockSpec(memory_space=pl.ANY),
            ],
            out_specs=pl.BlockSpec(memory_space=pl.ANY),
            scratch_shapes=[
                pltpu.VMEM((2, tm, D_MODEL), F32),
                pltpu.VMEM((8, D_MODEL), F32),
                pltpu.SMEM((2 * MOE_TOP_K * tm,), jnp.int32),
                pltpu.SemaphoreType.DMA((2,)),
                pltpu.SemaphoreType.DMA((2,)),
                pltpu.SemaphoreType.DMA((1,)),
            ],
        ),
        out_shape=jax.ShapeDtypeStruct((n_sorted, D_MODEL), F32),
        compiler_params=pltpu.CompilerParams(
            dimension_semantics=("arbitrary",), vmem_limit_bytes=_vmem(est)),
        name="moe_dispatch",
    )(padinfo, h, g, idx)


def _experts_kernel(be_ref, nu_ref, x_ref, wg_ref, wu_ref, wd_ref, y_ref):
    del be_ref
    used = pl.program_id(0) < nu_ref[0]

    @pl.when(used)
    def _():
        x = x_ref[...].astype(BF16)
        a = jnp.dot(x, wg_ref[...], preferred_element_type=F32)
        b = jnp.dot(x, wu_ref[...], preferred_element_type=F32)
        hid = (_silu(a) * b).astype(BF16)
        y_ref[...] = jnp.dot(hid, wd_ref[...], preferred_element_type=F32)

    @pl.when(jnp.logical_not(used))
    def _():
        y_ref[...] = jnp.zeros_like(y_ref)


def moe_experts(block_expert, n_used, xs, wg, wu, wd, *, bm):
    n_sorted = xs.shape[0]
    est = 4 * bm * D_MODEL * 4 + 2 * 3 * D_MODEL * EXP_DIM * 2 + 3 * bm * EXP_DIM * 4 + bm * D_MODEL * 2
    row_blk = lambda i, be, nu: (jnp.where(i < nu[0], i, 0), 0)
    return pl.pallas_call(
        _experts_kernel,
        grid_spec=pltpu.PrefetchScalarGridSpec(
            num_scalar_prefetch=2,
            grid=(n_sorted // bm,),
            in_specs=[
                pl.BlockSpec((bm, D_MODEL), row_blk),
                pl.BlockSpec((None, D_MODEL, EXP_DIM), lambda i, be, nu: (be[i], 0, 0)),
                pl.BlockSpec((None, D_MODEL, EXP_DIM), lambda i, be, nu: (be[i], 0, 0)),
                pl.BlockSpec((None, EXP_DIM, D_MODEL), lambda i, be, nu: (be[i], 0, 0)),
            ],
            out_specs=pl.BlockSpec((bm, D_MODEL), lambda i, be, nu: (i, 0)),
        ),
        out_shape=jax.ShapeDtypeStruct((n_sorted, D_MODEL), F32),
        compiler_params=pltpu.CompilerParams(
            dimension_semantics=("arbitrary",), vmem_limit_bytes=_vmem(est)),
        name="moe_experts",
    )(block_expert, n_used, xs, wg, wu, wd)


def _combine_kernel(h_ref, rec_ref, idx_hbm, ys_ref, o_ref, ybuf, idx_smem, row_sem, idx_sem):
    tm = h_ref.shape[0]
    i = pl.program_id(0)
    nsteps = pl.num_programs(0)
    n_idx = idx_smem.shape[0] // 2

    def row_copy(s, k, r, src_row):
        return pltpu.make_async_copy(ys_ref.at[pl.ds(src_row, 1)], ybuf.at[s, k, pl.ds(r, 1)], row_sem.at[s])

    def issue_rows(s):
        def body(j, c):
            r0 = pl.multiple_of(j * ROWS_PER_ISSUE, ROWS_PER_ISSUE)
            for rr in range(ROWS_PER_ISSUE):
                for k in range(MOE_TOP_K):
                    row_copy(s, k, r0 + rr, idx_smem[s * n_idx + MOE_TOP_K * (r0 + rr) + k]).start()
            return c
        lax.fori_loop(0, tm // ROWS_PER_ISSUE, body, 0)

    def wait_rows(s):
        def body(r, c):
            for k in range(MOE_TOP_K):
                row_copy(s, k, 0, 0).wait()
            return c
        lax.fori_loop(0, tm, body, 0, unroll=16)

    def step(s):
        @pl.when(i == 0)
        def _():
            _idx_copy(idx_hbm, idx_smem, idx_sem, 0, s).start()
            _idx_copy(idx_hbm, idx_smem, idx_sem, 0, s).wait()
            issue_rows(s)

            @pl.when(nsteps >= 2)
            def _():
                _idx_copy(idx_hbm, idx_smem, idx_sem, 1, 1 - s).start()

        @pl.when(i + 1 < nsteps)
        def _():
            _idx_copy(idx_hbm, idx_smem, idx_sem, i + 1, 1 - s).wait()
            issue_rows(1 - s)

        wait_rows(s)

        @pl.when(i + 2 < nsteps)
        def _():
            _idx_copy(idx_hbm, idx_smem, idx_sem, i + 2, s).start()

        rec = rec_ref[...]
        w1 = rec[:, REC_W:REC_W + 1]
        w2 = rec[:, REC_W + 1:REC_W + 2]
        o_ref[...] = h_ref[...] + w1 * ybuf[s, 0] + w2 * ybuf[s, 1]

    _static_slot(i % 2, step)


def moe_combine(h, rec, idx, ys, *, tm):
    n = h.shape[0]
    est = 4 * tm * D_MODEL * 4 + 2 * tm * LANES * 4 + 2 * MOE_TOP_K * tm * D_MODEL * 4
    return pl.pallas_call(
        _combine_kernel,
        grid=(n // tm,),
        in_specs=[
            pl.BlockSpec((tm, D_MODEL), lambda i: (i, 0)),
            pl.BlockSpec((tm, LANES), lambda i: (i, 0)),
            pl.BlockSpec(memory_space=pl.ANY),
            pl.BlockSpec(memory_space=pl.ANY),
        ],
        out_specs=pl.BlockSpec((tm, D_MODEL), lambda i: (i, 0)),
        out_shape=jax.ShapeDtypeStruct((n, D_MODEL), F32),
        scratch_shapes=[
            pltpu.VMEM((2, MOE_TOP_K, tm, D_MODEL), F32),
            pltpu.SMEM((2 * MOE_TOP_K * tm,), jnp.int32),
            pltpu.SemaphoreType.DMA((2,)),
            pltpu.SemaphoreType.DMA((2,)),
        ],
        compiler_params=pltpu.CompilerParams(
            dimension_semantics=("arbitrary",), vmem_limit_bytes=_vmem(est)),
        name="moe_combine",
    )(h, rec, idx, ys)


def ffn_moe(h, g, wr, wg, wu, wd, *, bm=512, tm=512):
    n = h.shape[0]
    n_sorted = MOE_TOP_K * n + N_EXP * bm
    rec, cnt = moe_route(h, g, wr)
    expert = rec[:, REC_E:REC_E + MOE_TOP_K].astype(jnp.int32)
    rank = rec[:, REC_RANK:REC_RANK + MOE_TOP_K].astype(jnp.int32)
    counts = cnt[0, :N_EXP].astype(jnp.int32)
    padded = (counts + bm - 1) // bm * bm
    pend = jnp.cumsum(padded)
    pstart = pend - padded
    dest = jnp.sum(jnp.where(expert[..., None] == jnp.arange(N_EXP), pstart, 0), axis=-1) + rank
    idx = dest.reshape(MOE_TOP_K * n)
    padinfo = jnp.concatenate([pstart + counts, padded - counts]).astype(jnp.int32)
    n_used = (pend[-1:] // bm).astype(jnp.int32)
    blk_start = jnp.arange(n_sorted // bm, dtype=jnp.int32) * bm
    block_expert = jnp.minimum(jnp.sum(blk_start[:, None] >= pend[None, :], axis=1), N_EXP - 1).astype(jnp.int32)
    xs = moe_dispatch(h, g, idx, padinfo, n_sorted, tm=tm)
    ys = moe_experts(block_expert, n_used, xs, wg, wu, wd, bm=bm)
    return moe_combine(h, rec, idx, ys, tm=tm)


def _ple_kernel(h_ref, p_ref, gi_ref, wgate_ref, wp_ref, go_ref, o_ref):
    h = h_ref[...]
    hn = _rms(h, gi_ref[...]).astype(BF16)
    gate = jax.nn.sigmoid(jnp.dot(hn, wgate_ref[...], preferred_element_type=F32))
    e = jnp.dot(p_ref[...].astype(BF16), wp_ref[...], preferred_element_type=F32)
    o_ref[...] = h + gate * _rms(e, go_ref[...])


def ple(h, p_all, layer, gi, wgate, wp, go, *, tm=512):
    n = h.shape[0]
    est = 4 * tm * D_MODEL * 4 + 2 * D_MODEL * D_MODEL * 2 + 2 * PLE_DIM * D_MODEL * 2 + 3 * tm * D_MODEL * 4
    return pl.pallas_call(
        _ple_kernel,
        grid=(n // tm,),
        in_specs=[
            pl.BlockSpec((tm, D_MODEL), lambda i: (i, 0)),
            pl.BlockSpec((None, tm, PLE_DIM), lambda i: (layer, i, 0)),
            pl.BlockSpec((1, D_MODEL), lambda i: (0, 0)),
            pl.BlockSpec((D_MODEL, D_MODEL), lambda i: (0, 0)),
            pl.BlockSpec((PLE_DIM, D_MODEL), lambda i: (0, 0)),
            pl.BlockSpec((1, D_MODEL), lambda i: (0, 0)),
        ],
        out_specs=pl.BlockSpec((tm, D_MODEL), lambda i: (i, 0)),
        out_shape=jax.ShapeDtypeStruct((n, D_MODEL), F32),
        compiler_params=pltpu.CompilerParams(
            dimension_semantics=("parallel",), vmem_limit_bytes=_vmem(est)),
        name="ple",
    )(h, p_all, gi, wgate, wp, go)


def _prep_layer(i, P):
    lr0 = Z_AQ
    n_in = P["w_in"].shape[-1]

    def w_in_cols(a, b):
        return lax.slice(P["w_in"], (i, 0, a), (i + 1, D_MODEL, b)).reshape(D_MODEL, b - a).astype(BF16)

    w_in_r = jnp.concatenate([w_in_cols(0, lr0), w_in_cols(lr0 + 2 * GLA_RANK, n_in)], axis=1)
    w_lr = jnp.concatenate([w_in_cols(lr0, lr0 + 2 * GLA_RANK),
                            jnp.zeros((D_MODEL, LR_W - 2 * GLA_RANK), BF16)], axis=1)
    zpad = jnp.zeros((LR_W, GLA_KW), F32)
    wgf = zpad.at[0:GLA_RANK].set(P["w_gate_f"][i]).astype(BF16)
    wgb = zpad.at[GLA_RANK:2 * GLA_RANK].set(P["w_gate_b"][i]).astype(BF16)
    L = dict(
        g_mix=P["g_mix"][i][None], w_in=w_in_r, w_lr=w_lr, wgf=wgf, wgb=wgb,
        bgf=P["b_gate_f"][i][None], bgb=P["b_gate_b"][i][None],
        gla_norm=P["gla_norm"][i].reshape(GLA_H, 1, GLA_DV),
        q_norm=P["q_norm"][i][None], k_norm=P["k_norm"][i][None],
        bias_tbl=natten_bias_table(P["rpb"][i]), attn_norm=P["attn_norm"][i][None],
        w_o=P["w_o"][i].astype(BF16), g_ffn=P["g_ffn"][i][None],
        g_ple_in=P["g_ple_in"][i][None], w_ple_gate=P["w_ple_gate"][i].astype(BF16),
        w_ple=P["w_ple"][i].astype(BF16), g_ple_out=P["g_ple_out"][i][None],
    )
    j = i // 2
    if i % 2 == 0:
        L.update(wg=P["w_ff_gate"][j].astype(BF16), wu=P["w_ff_up"][j].astype(BF16),
                 wd=P["w_ff_down"][j].astype(BF16))
    else:
        wr = jnp.zeros((D_MODEL, LANES), F32).at[:, :N_EXP].set(P["w_router"][j]).astype(BF16)
        L.update(wr=wr, wg=P["w_exp_gate"][j].astype(BF16), wu=P["w_exp_up"][j].astype(BF16),
                 wd=P["w_exp_down"][j].astype(BF16))
    return L


def _trunk(x, p, layers):
    B, T, _ = x.shape
    n = B * T
    h = x.reshape(n, D_MODEL)
    p_all = p.reshape(DEPTH, n, PLE_DIM)
    for i, L in enumerate(layers):
        z, zl = inproj(h, L["g_mix"], L["w_in"], L["w_lr"])
        z3 = z.reshape(B, T, NZ)
        zl3 = zl.reshape(B, T, LR_W)
        o_f = gla_direction(z3, zl3, L["wgf"], L["bgf"], reverse=False)
        o_gla = gla_direction(z3, zl3, L["wgb"], L["bgb"], reverse=True, o_other=o_f, gn=L["gla_norm"])
        o_na = natten(z3, L["bias_tbl"], L["q_norm"], L["k_norm"])
        h = outproj(o_gla.reshape(n, GLA_VW), o_na.reshape(n, NA_W), L["attn_norm"], L["w_o"], h)
        if i % 2 == 0:
            h = ffn_dense(h, L["g_ffn"], L["wg"], L["wu"], L["wd"])
        else:
            h = ffn_moe(h, L["g_ffn"], L["wr"], L["wg"], L["wu"], L["wd"])
        h = ple(h, p_all, i, L["g_ple_in"], L["w_ple_gate"], L["w_ple"], L["g_ple_out"])
    return h.reshape(B, T, D_MODEL)


def kernel(x_prompt, x_sample, p_prompt, p_sample, g_mix, w_in, w_gate_f, b_gate_f, w_gate_b, b_gate_b, gla_norm, q_norm, k_norm, rpb, attn_norm, w_o, g_ffn, w_ff_gate, w_ff_up, w_ff_down, w_router, w_exp_gate, w_exp_up, w_exp_down, g_ple_in, w_ple_gate, w_ple, g_ple_out):
    P = dict(g_mix=g_mix, w_in=w_in, w_gate_f=w_gate_f, b_gate_f=b_gate_f, w_gate_b=w_gate_b,
             b_gate_b=b_gate_b, gla_norm=gla_norm, q_norm=q_norm, k_norm=k_norm, rpb=rpb,
             attn_norm=attn_norm, w_o=w_o, g_ffn=g_ffn, w_ff_gate=w_ff_gate, w_ff_up=w_ff_up,
             w_ff_down=w_ff_down, w_router=w_router, w_exp_gate=w_exp_gate, w_exp_up=w_exp_up,
             w_exp_down=w_exp_down, g_ple_in=g_ple_in, w_ple_gate=w_ple_gate, w_ple=w_ple,
             g_ple_out=g_ple_out)
    layers = [_prep_layer(i, P) for i in range(DEPTH)]
    return (_trunk(x_prompt, p_prompt, layers), _trunk(x_sample, p_sample, layers))
```

```python
import functools

import jax
import jax.numpy as jnp
from jax import lax
from jax.experimental import pallas as pl
from jax.experimental.pallas import tpu as pltpu

F32 = jnp.float32
BF16 = jnp.bfloat16

D_MODEL = 2048
DEPTH = 4
GRID_W = 64
GLA_H = 4
GLA_VW = 1024
GLA_KW = 512
GLA_DK = 128
GLA_DV = 256
GLA_RANK = 16
GLA_INV_NORMALIZER = 1.0 / 16.0
GLA_CHUNK = 64
NA_W = 1024
NA_H = 8
NA_HD = 128
WIN_ROWS = 8
WIN_COLS = 16
FF_DIM = 4096
N_EXP = 8
EXP_DIM = 1024
PLE_DIM = 256
EPS = 1e-6
NEG_INF = -1e30

Z_GQ, Z_GK, Z_GV, Z_GG = 0, 512, 1024, 2048
Z_AQ, Z_AK, Z_AV, NZ = 3072, 4096, 5120, 6144
LR_W = 256

V7X_VMEM_BYTES = 64 * 1024 * 1024
LANES = 128


def _vmem(nbytes):
    return int(min(nbytes * 3 // 2 + (4 << 20), V7X_VMEM_BYTES - (6 << 20)))


def _rms(x, g):
    ms = jnp.mean(x * x, axis=-1, keepdims=True)
    return x * lax.rsqrt(ms + EPS) * g


def _silu(x):
    return x * jax.nn.sigmoid(x)


def _inproj_kernel(h_ref, g_ref, w_ref, wl_ref, z_ref, zl_ref, u_scr):
    j = pl.program_id(1)

    @pl.when(j == 0)
    def _():
        u = _rms(h_ref[...], g_ref[...]).astype(BF16)
        u_scr[...] = u
        zl_ref[...] = jnp.dot(u, wl_ref[...], preferred_element_type=F32).astype(zl_ref.dtype)

    @pl.when(j > 0)
    def _():
        z_ref[...] = jnp.dot(u_scr[...], w_ref[...], preferred_element_type=F32).astype(z_ref.dtype)


def inproj(h, g, w, wl, *, tm=1024, tn=1536):
    n = h.shape[0]
    nmain = NZ // tn
    est = (2 * tm * D_MODEL * 4 + 2 * D_MODEL * tn * 2 + 2 * tm * tn * 2 + tm * D_MODEL * 2 + tm * tn * 4
           + 2 * D_MODEL * LR_W * 2 + 2 * tm * LR_W * 2)
    main = lambda j: jnp.maximum(j - 1, 0)
    return pl.pallas_call(
        _inproj_kernel,
        grid=(n // tm, nmain + 1),
        in_specs=[
            pl.BlockSpec((tm, D_MODEL), lambda i, j: (i, 0)),
            pl.BlockSpec((1, D_MODEL), lambda i, j: (0, 0)),
            pl.BlockSpec((D_MODEL, tn), lambda i, j: (0, main(j))),
            pl.BlockSpec((D_MODEL, LR_W), lambda i, j: (0, 0)),
        ],
        out_specs=[pl.BlockSpec((tm, tn), lambda i, j: (i, main(j))),
                   pl.BlockSpec((tm, LR_W), lambda i, j: (i, 0))],
        out_shape=[jax.ShapeDtypeStruct((n, NZ), BF16), jax.ShapeDtypeStruct((n, LR_W), BF16)],
        scratch_shapes=[pltpu.VMEM((tm, D_MODEL), BF16)],
        compiler_params=pltpu.CompilerParams(
            dimension_semantics=("parallel", "arbitrary"), vmem_limit_bytes=_vmem(est)),
        name="inproj",
    )(h, g, w, wl)


def _cumsum_rows(x, seg):
    idx = lax.broadcasted_iota(jnp.int32, x.shape, 0) & (seg - 1)
    s = 1
    while s < seg:
        x = x + jnp.where(idx >= s, pltpu.roll(x, s, 0), 0.0)
        s *= 2
    return x


def _log_sigmoid(x):
    return jnp.minimum(x, 0.0) - jnp.log(1.0 + jnp.exp(-jnp.abs(x)))


def _split_bf16(x):
    hi = x.astype(BF16)
    r1 = x - hi.astype(F32)
    mid = r1.astype(BF16)
    lo = (r1 - mid.astype(F32)).astype(BF16)
    return hi, mid, lo


def _gla_kernel(*refs, reverse, final, tb):
    if final:
        q_ref, k_ref, v_ref, lr_ref, wg_ref, bg_ref, of_ref, gg_ref, gn_ref, o_ref, s_scr = refs
    else:
        q_ref, k_ref, v_ref, lr_ref, wg_ref, bg_ref, o_ref, s_scr = refs
    C = GLA_CHUNK
    nchunk = tb // C

    @pl.when(pl.program_id(2) == 0)
    def _():
        s_scr[...] = jnp.zeros_like(s_scr)

    x = jnp.dot(lr_ref[...], wg_ref[...], preferred_element_type=F32) + bg_ref[...]
    la_parts = _split_bf16(_log_sigmoid(x) * GLA_INV_NORMALIZER)

    row = lax.broadcasted_iota(jnp.int32, (C, C), 0)
    col = lax.broadcasted_iota(jnp.int32, (C, C), 1)
    mask = (col >= row) if reverse else (col <= row)
    tri = jnp.where(mask, 1.0, 0.0).astype(BF16)
    scale = GLA_DK ** -0.5

    order = [(nchunk - 1 - ci) if reverse else ci for ci in range(nchunk)]
    sl = [slice(c * C, (c + 1) * C) for c in range(nchunk)]
    qts, atts, incs, dls = {}, {}, {}, {}
    bs = {c: sum(jnp.dot(tri, part[sl[c]], preferred_element_type=F32) for part in la_parts) for c in order}
    for c in order:
        b = bs[c]
        blast = b[0:1, :] if reverse else b[C - 1:C, :]
        q = q_ref[sl[c], :].astype(F32)
        k = k_ref[sl[c], :].astype(F32)
        qts[c] = (q * jnp.exp(b) * scale).astype(BF16)
        kdec = k * jnp.exp(-b)
        dl = jnp.exp(blast)
        kt = kdec.astype(BF16)
        ke_t = (kdec * dl).T.astype(BF16)
        dls[c] = jnp.broadcast_to(dl, (8, GLA_DK)).T[:, 0:1]
        att = lax.dot_general(qts[c], kt, (((1,), (1,)), ((), ())), preferred_element_type=F32)
        atts[c] = jnp.where(mask, att, 0.0).astype(BF16)
        incs[c] = jnp.dot(ke_t, v_ref[sl[c], :], preferred_element_type=F32)
    state = s_scr[...]
    states = {}
    for c in order:
        states[c] = state.astype(BF16)
        state = dls[c] * state + incs[c]
    s_scr[...] = state
    for c in order:
        o = (jnp.dot(atts[c], v_ref[sl[c], :], preferred_element_type=F32)
             + jnp.dot(qts[c], states[c], preferred_element_type=F32))
        if final:
            y = _rms(o + of_ref[sl[c], :], gn_ref[...])
            gg = gg_ref[sl[c], :].astype(F32)
            o_ref[sl[c], :] = (y * _silu(gg)).astype(o_ref.dtype)
        else:
            o_ref[sl[c], :] = o


def gla_direction(z3, zl3, wg, bg, *, reverse, o_other=None, gn=None, tb=2048):
    B, T, _ = z3.shape
    nb = T // tb
    final = o_other is not None
    blk = (lambda j: nb - 1 - j) if reverse else (lambda j: j)
    qw, vw = GLA_DK, GLA_DV
    in_specs = [
        pl.BlockSpec((None, tb, qw), lambda b, h, j: (b, blk(j), Z_GQ // qw + h)),
        pl.BlockSpec((None, tb, qw), lambda b, h, j: (b, blk(j), Z_GK // qw + h)),
        pl.BlockSpec((None, tb, vw), lambda b, h, j: (b, blk(j), Z_GV // vw + h)),
        pl.BlockSpec((None, tb, LR_W), lambda b, h, j: (b, blk(j), 0)),
        pl.BlockSpec((LR_W, qw), lambda b, h, j: (0, h)),
        pl.BlockSpec((1, qw), lambda b, h, j: (0, h)),
    ]
    args = [z3, z3, z3, zl3, wg, bg]
    if final:
        in_specs += [
            pl.BlockSpec((None, tb, vw), lambda b, h, j: (b, blk(j), h)),
            pl.BlockSpec((None, tb, vw), lambda b, h, j: (b, blk(j), Z_GG // vw + h)),
            pl.BlockSpec((None, 1, vw), lambda b, h, j: (h, 0, 0)),
        ]
        args += [o_other, z3, gn]
    out_dtype = BF16 if final else F32
    est = tb * (2 * 2 * (2 * qw + vw + LR_W) + (2 * (4 + 2) * vw if final else 0) + 2 * 4 * vw + 10 * 4 * qw)
    return pl.pallas_call(
        functools.partial(_gla_kernel, reverse=reverse, final=final, tb=tb),
        grid=(B, GLA_H, nb),
        in_specs=in_specs,
        out_specs=pl.BlockSpec((None, tb, vw), lambda b, h, j: (b, blk(j), h)),
        out_shape=jax.ShapeDtypeStruct((B, T, GLA_VW), out_dtype),
        scratch_shapes=[pltpu.VMEM((GLA_DK, GLA_DV), F32)],
        compiler_params=pltpu.CompilerParams(
            dimension_semantics=("parallel", "parallel", "arbitrary"), vmem_limit_bytes=_vmem(est)),
        name="gla_bwd" if reverse else "gla_fwd",
    )(*args)


def _natten_kernel(q_ref, k_ref, v_ref, bias_ref, qg_ref, kg_ref, o_ref, kn_scr, *, rows, rq):
    T = rows * GRID_W
    KB = 2048
    WK = WIN_ROWS * GRID_W
    j = pl.program_id(2)

    @pl.when(j == 0)
    def _():
        def norm_block(i, carry):
            off = pl.multiple_of(i * KB, KB)
            kk = k_ref[pl.ds(off, KB), :].astype(F32)
            kn_scr[pl.ds(off, KB), :] = _rms(kk, kg_ref[...]).astype(BF16)
            return carry

        lax.fori_loop(0, T // KB, norm_block, 0)

    qn_blk = (_rms(q_ref[...].astype(F32), qg_ref[...]) * (NA_HD ** -0.5)).astype(BF16)
    koffs, probs, sums = [], [], []
    for i in range(rq):
        r = j * rq + i
        rs = jnp.clip(r - WIN_ROWS // 2, 0, rows - WIN_ROWS)
        dti = rs - r + (WIN_ROWS - 1)
        koffs.append(pl.multiple_of(rs * GRID_W, GRID_W))
        kw = kn_scr[pl.ds(koffs[i], WK), :]
        s = lax.dot_general(qn_blk[i * GRID_W:(i + 1) * GRID_W], kw, (((1,), (1,)), ((), ())),
                            preferred_element_type=F32)
        s = s + bias_ref[dti]
        p = jnp.exp(s - jnp.max(s, axis=-1, keepdims=True))
        sums.append(jnp.sum(p, axis=-1, keepdims=True))
        probs.append(p.astype(BF16))
    for i in range(rq):
        vw = v_ref[pl.ds(koffs[i], WK), :]
        o = jnp.dot(probs[i], vw, preferred_element_type=F32) / sums[i]
        o_ref[i * GRID_W:(i + 1) * GRID_W, :] = o.astype(o_ref.dtype)


def natten(z3, bias_tbl, qg, kg, *, rq=32):
    B, T, _ = z3.shape
    rows = T // GRID_W
    assert rows >= WIN_ROWS and rows % rq == 0
    tq = rq * GRID_W
    hd = NA_HD
    est = 2 * 2 * T * hd * 2 + T * hd * 2 + 2 * WIN_ROWS * GRID_W * WIN_ROWS * GRID_W * 4 + 4 * tq * hd * 2
    return pl.pallas_call(
        functools.partial(_natten_kernel, rows=rows, rq=rq),
        grid=(B, NA_H, rows // rq),
        in_specs=[
            pl.BlockSpec((None, tq, hd), lambda b, h, j: (b, j, Z_AQ // hd + h)),
            pl.BlockSpec((None, T, hd), lambda b, h, j: (b, 0, Z_AK // hd + h)),
            pl.BlockSpec((None, T, hd), lambda b, h, j: (b, 0, Z_AV // hd + h)),
            pl.BlockSpec((None, WIN_ROWS, GRID_W, WIN_ROWS * GRID_W), lambda b, h, j: (h, 0, 0, 0)),
            pl.BlockSpec((1, hd), lambda b, h, j: (0, 0)),
            pl.BlockSpec((1, hd), lambda b, h, j: (0, 0)),
        ],
        out_specs=pl.BlockSpec((None, tq, hd), lambda b, h, j: (b, j, h)),
        out_shape=jax.ShapeDtypeStruct((B, T, NA_W), BF16),
        scratch_shapes=[pltpu.VMEM((T, hd), BF16)],
        compiler_params=pltpu.CompilerParams(
            dimension_semantics=("parallel", "parallel", "arbitrary"), vmem_limit_bytes=_vmem(est)),
        name="natten",
    )(z3, z3, z3, bias_tbl, qg, kg)


def natten_bias_table(rpb):
    c = jnp.arange(GRID_W, dtype=jnp.int32)
    c_start = jnp.clip(c - WIN_COLS // 2, 0, GRID_W - WIN_COLS)
    valid = (c[None, :] >= c_start[:, None]) & (c[None, :] < c_start[:, None] + WIN_COLS)
    dc = jnp.clip(c[None, :] - c[:, None], -(WIN_COLS - 1), WIN_COLS - 1) + (WIN_COLS - 1)
    bias_c = jnp.where(valid, rpb.astype(F32)[:, :, dc], NEG_INF)
    dr = jnp.arange(WIN_ROWS)[:, None] + jnp.arange(WIN_ROWS)[None, :]
    tbl = bias_c[:, dr]
    return tbl.transpose(0, 1, 3, 2, 4).reshape(NA_H, WIN_ROWS, GRID_W, WIN_ROWS * GRID_W)


def _outproj_kernel(og_ref, on_ref, an_ref, w_ref, h_ref, o_ref):
    na = _rms(on_ref[...].astype(F32), an_ref[...]).astype(BF16)
    acc = jnp.dot(og_ref[...], w_ref[0:GLA_VW, :], preferred_element_type=F32)
    acc += jnp.dot(na, w_ref[GLA_VW:GLA_VW + NA_W, :], preferred_element_type=F32)
    o_ref[...] = h_ref[...] + acc


def outproj(o_gla, o_na, an, w_o, layer, h, *, tm=512):
    n = h.shape[0]
    est = 2 * 2 * tm * 1024 * 2 + 2 * D_MODEL * D_MODEL * 2 + 4 * tm * D_MODEL * 4 + tm * D_MODEL * 4
    return pl.pallas_call(
        _outproj_kernel,
        grid=(n // tm,),
        in_specs=[
            pl.BlockSpec((tm, GLA_VW), lambda i: (i, 0)),
            pl.BlockSpec((tm, NA_W), lambda i: (i, 0)),
            pl.BlockSpec((1, NA_W), lambda i: (0, 0)),
            pl.BlockSpec((None, D_MODEL, D_MODEL), lambda i: (layer, 0, 0)),
            pl.BlockSpec((tm, D_MODEL), lambda i: (i, 0)),
        ],
        out_specs=pl.BlockSpec((tm, D_MODEL), lambda i: (i, 0)),
        out_shape=jax.ShapeDtypeStruct((n, D_MODEL), F32),
        compiler_params=pltpu.CompilerParams(
            dimension_semantics=("parallel",), vmem_limit_bytes=_vmem(est)),
        name="outproj",
    )(o_gla, o_na, an, w_o, h)


FFN_TF = 512


def _ffn_kernel(h_ref, g_ref, wg_ref, wu_ref, wd_ref, o_ref, u_scr):
    @pl.when(pl.program_id(1) == 0)
    def _():
        h = h_ref[...]
        u_scr[...] = _rms(h, g_ref[...]).astype(BF16)
        o_ref[...] = h

    u = u_scr[...]
    a = jnp.dot(u, wg_ref[...], preferred_element_type=F32)
    b = jnp.dot(u, wu_ref[...], preferred_element_type=F32)
    hid = (_silu(a) * b).astype(BF16)
    o_ref[...] += jnp.dot(hid, wd_ref[...], preferred_element_type=F32)


def ffn_dense(h, g, wg, wu, wd, layer, *, tm=1024, tf=FFN_TF):
    n = h.shape[0]
    est = 4 * tm * D_MODEL * 4 + tm * D_MODEL * 2 + 2 * 3 * D_MODEL * tf * 2 + 3 * tm * tf * 4
    return pl.pallas_call(
        _ffn_kernel,
        grid=(n // tm, FF_DIM // tf),
        in_specs=[
            pl.BlockSpec((tm, D_MODEL), lambda i, f: (i, 0)),
            pl.BlockSpec((1, D_MODEL), lambda i, f: (0, 0)),
            pl.BlockSpec((None, D_MODEL, tf), lambda i, f: (layer, 0, f)),
            pl.BlockSpec((None, D_MODEL, tf), lambda i, f: (layer, 0, f)),
            pl.BlockSpec((None, tf, D_MODEL), lambda i, f: (layer, f, 0)),
        ],
        out_specs=pl.BlockSpec((tm, D_MODEL), lambda i, f: (i, 0)),
        out_shape=jax.ShapeDtypeStruct((n, D_MODEL), F32),
        scratch_shapes=[pltpu.VMEM((tm, D_MODEL), BF16)],
        compiler_params=pltpu.CompilerParams(
            dimension_semantics=("parallel", "arbitrary"), vmem_limit_bytes=_vmem(est)),
        name="ffn_dense",
    )(h, g, wg, wu, wd)


REC_E, REC_W, REC_RANK = 0, 2, 4
MOE_TOP_K = 2


def _route_kernel(h_ref, g_ref, wr_ref, rec_ref, cnt_ref, carry_scr):
    tm = h_ref.shape[0]
    lane = lax.broadcasted_iota(jnp.int32, (tm, LANES), 1)

    @pl.when(pl.program_id(0) == 0)
    def _():
        carry_scr[...] = jnp.zeros_like(carry_scr)

    u = _rms(h_ref[...], g_ref[...]).astype(BF16)
    logits = jnp.dot(u, wr_ref[...], preferred_element_type=F32)
    lg = jnp.where(lane < N_EXP, logits, -jnp.inf)
    v1 = jnp.max(lg, axis=-1, keepdims=True)
    i1 = jnp.min(jnp.where(lg == v1, lane, LANES), axis=-1, keepdims=True)
    lg2 = jnp.where(lane == i1, -jnp.inf, lg)
    v2 = jnp.max(lg2, axis=-1, keepdims=True)
    i2 = jnp.min(jnp.where(lg2 == v2, lane, LANES), axis=-1, keepdims=True)
    e2 = jnp.exp(v2 - v1)
    w1 = 1.0 / (1.0 + e2)
    w2 = e2 / (1.0 + e2)
    onehot = jnp.where(lane == i1, 1.0, 0.0) + jnp.where(lane == i2, 1.0, 0.0)
    incl = _cumsum_rows(onehot, tm)
    before = incl - onehot + carry_scr[0:1, :]
    r1 = jnp.sum(jnp.where(lane == i1, before, 0.0), axis=-1, keepdims=True)
    r2 = jnp.sum(jnp.where(lane == i2, before, 0.0), axis=-1, keepdims=True)
    rec = jnp.where(lane == REC_E, i1.astype(F32), 0.0)
    rec = jnp.where(lane == REC_E + 1, i2.astype(F32), rec)
    rec = jnp.where(lane == REC_W, w1, rec)
    rec = jnp.where(lane == REC_W + 1, w2, rec)
    rec = jnp.where(lane == REC_RANK, r1, rec)
    rec = jnp.where(lane == REC_RANK + 1, r2, rec)
    rec_ref[...] = rec
    carry_scr[...] = carry_scr[...] + incl[tm - 1:tm, :]
    cnt_ref[...] = carry_scr[...]


def moe_route(h, g, wr, *, tm=512):
    n = h.shape[0]
    est = 2 * tm * D_MODEL * 4 + D_MODEL * LANES * 4 + 8 * tm * LANES * 4
    return pl.pallas_call(
        _route_kernel,
        grid=(n // tm,),
        in_specs=[
            pl.BlockSpec((tm, D_MODEL), lambda i: (i, 0)),
            pl.BlockSpec((1, D_MODEL), lambda i: (0, 0)),
            pl.BlockSpec((D_MODEL, LANES), lambda i: (0, 0)),
        ],
        out_specs=[pl.BlockSpec((tm, LANES), lambda i: (i, 0)), pl.BlockSpec((8, LANES), lambda i: (0, 0))],
        out_shape=[jax.ShapeDtypeStruct((n, LANES), F32), jax.ShapeDtypeStruct((8, LANES), F32)],
        scratch_shapes=[pltpu.VMEM((8, LANES), F32)],
        compiler_params=pltpu.CompilerParams(
            dimension_semantics=("arbitrary",), vmem_limit_bytes=_vmem(est)),
        name="moe_route",
    )(h, g, wr)


def _static_slot(slot, fn):
    for s in range(2):
        pl.when(slot == s)(functools.partial(fn, s))


ROWS_PER_ISSUE = 8


def _idx_copy(idx_hbm, idx_smem, sem, step, s):
    n = idx_smem.shape[0] // 2
    return pltpu.make_async_copy(idx_hbm.at[pl.ds(pl.multiple_of(step * n, n), n)],
                                 idx_smem.at[pl.ds(s * n, n)], sem.at[s])


def _dispatch_kernel(pad_ref, h_ref, g_ref, idx_hbm, xs_ref, ubuf, zrow, idx_smem, row_sem, idx_sem, pad_sem):
    tm = h_ref.shape[0]
    i = pl.program_id(0)
    nsteps = pl.num_programs(0)
    n_idx = idx_smem.shape[0] // 2

    def row_copy(s, r, dst_row):
        return pltpu.make_async_copy(ubuf.at[s, pl.ds(r, 1)], xs_ref.at[pl.ds(dst_row, 1)], row_sem.at[s])

    def wait_rows(s):
        def body(r, c):
            for _ in range(MOE_TOP_K):
                row_copy(s, 0, 0).wait()
            return c
        lax.fori_loop(0, tm, body, 0, unroll=16)

    def issue_rows(s):
        def body(j, c):
            r0 = pl.multiple_of(j * ROWS_PER_ISSUE, ROWS_PER_ISSUE)
            for rr in range(ROWS_PER_ISSUE):
                for k in range(MOE_TOP_K):
                    row_copy(s, r0 + rr, idx_smem[s * n_idx + MOE_TOP_K * (r0 + rr) + k]).start()
            return c
        lax.fori_loop(0, tm // ROWS_PER_ISSUE, body, 0)

    def zero_pad_rows():
        zrow[...] = jnp.zeros_like(zrow)
        for e in range(N_EXP):
            def pad_copy(j, e=e):
                return pltpu.make_async_copy(zrow.at[pl.ds(0, 1)], xs_ref.at[pl.ds(pad_ref[e] + j, 1)], pad_sem.at[0])

            def start(j, c, pad_copy=pad_copy):
                pad_copy(j).start()
                return c

            def wait(j, c, pad_copy=pad_copy):
                pad_copy(j).wait()
                return c
            lax.fori_loop(0, pad_ref[N_EXP + e], start, 0)
            lax.fori_loop(0, pad_ref[N_EXP + e], wait, 0)

    def step(s):
        @pl.when(i == 0)
        def _():
            _idx_copy(idx_hbm, idx_smem, idx_sem, 0, s).start()

        _idx_copy(idx_hbm, idx_smem, idx_sem, i, s).wait()

        @pl.when(i + 1 < nsteps)
        def _():
            _idx_copy(idx_hbm, idx_smem, idx_sem, i + 1, 1 - s).start()

        @pl.when(i >= 2)
        def _():
            wait_rows(s)

        ubuf[s] = _rms(h_ref[...], g_ref[...])
        issue_rows(s)

        @pl.when(i == nsteps - 1)
        def _():
            zero_pad_rows()
            wait_rows(s)

            @pl.when(nsteps >= 2)
            def _():
                wait_rows(1 - s)

    _static_slot(i % 2, step)


def moe_dispatch(h, g, idx, padinfo, n_sorted, *, tm):
    n = h.shape[0]
    est = 2 * tm * D_MODEL * 4 + 2 * tm * D_MODEL * 4 + 8 * D_MODEL * 4
    return pl.pallas_call(
        _dispatch_kernel,
        grid_spec=pltpu.PrefetchScalarGridSpec(
            num_scalar_prefetch=1,
            grid=(n // tm,),
            in_specs=[
                pl.BlockSpec((tm, D_MODEL), lambda i, pad: (i, 0)),
                pl.BlockSpec((1, D_MODEL), lambda i, pad: (0, 0)),
                pl.BlockSpec(memory_space=pl.ANY),
            ],
            out_specs=pl.BlockSpec(memory_space=pl.ANY),
            scratch_shapes=[
                pltpu.VMEM((2, tm, D_MODEL), F32),
                pltpu.VMEM((8, D_MODEL), F32),
                pltpu.SMEM((2 * MOE_TOP_K * tm,), jnp.int32),
                pltpu.SemaphoreType.DMA((2,)),
                pltpu.SemaphoreType.DMA((2,)),
                pltpu.SemaphoreType.DMA((1,)),
            ],
        ),
        out_shape=jax.ShapeDtypeStruct((n_sorted, D_MODEL), F32),
        compiler_params=pltpu.CompilerParams(
            dimension_semantics=("arbitrary",), vmem_limit_bytes=_vmem(est)),
        name="moe_dispatch",
    )(padinfo, h, g, idx)


def _experts_kernel(be_ref, nu_ref, x_ref, wg_ref, wu_ref, wd_ref, y_ref):
    del be_ref
    used = pl.program_id(0) < nu_ref[0]

    @pl.when(used)
    def _():
        x = x_ref[...].astype(BF16)
        a = jnp.dot(x, wg_ref[...], preferred_element_type=F32)
        b = jnp.dot(x, wu_ref[...], preferred_element_type=F32)
        hid = (_silu(a) * b).astype(BF16)
        y_ref[...] = jnp.dot(hid, wd_ref[...], preferred_element_type=F32)

    @pl.when(jnp.logical_not(used))
    def _():
        y_ref[...] = jnp.zeros_like(y_ref)


def moe_experts(block_expert, n_used, xs, wg, wu, wd, layer, *, bm):
    n_sorted = xs.shape[0]
    est = 4 * bm * D_MODEL * 4 + 2 * 3 * D_MODEL * EXP_DIM * 2 + 3 * bm * EXP_DIM * 4 + bm * D_MODEL * 2
    row_blk = lambda i, be, nu: (jnp.where(i < nu[0], i, 0), 0)
    return pl.pallas_call(
        _experts_kernel,
        grid_spec=pltpu.PrefetchScalarGridSpec(
            num_scalar_prefetch=2,
            grid=(n_sorted // bm,),
            in_specs=[
                pl.BlockSpec((bm, D_MODEL), row_blk),
                pl.BlockSpec((None, None, D_MODEL, EXP_DIM), lambda i, be, nu: (layer, be[i], 0, 0)),
                pl.BlockSpec((None, None, D_MODEL, EXP_DIM), lambda i, be, nu: (layer, be[i], 0, 0)),
                pl.BlockSpec((None, None, EXP_DIM, D_MODEL), lambda i, be, nu: (layer, be[i], 0, 0)),
            ],
            out_specs=pl.BlockSpec((bm, D_MODEL), lambda i, be, nu: (i, 0)),
        ),
        out_shape=jax.ShapeDtypeStruct((n_sorted, D_MODEL), F32),
        compiler_params=pltpu.CompilerParams(
            dimension_semantics=("arbitrary",), vmem_limit_bytes=_vmem(est)),
        name="moe_experts",
    )(block_expert, n_used, xs, wg, wu, wd)


def _combine_kernel(h_ref, rec_ref, idx_hbm, ys_ref, o_ref, ybuf, idx_smem, row_sem, idx_sem):
    tm = h_ref.shape[0]
    i = pl.program_id(0)
    nsteps = pl.num_programs(0)
    n_idx = idx_smem.shape[0] // 2

    def row_copy(s, k, r, src_row):
        return pltpu.make_async_copy(ys_ref.at[pl.ds(src_row, 1)], ybuf.at[s, k, pl.ds(r, 1)], row_sem.at[s])

    def issue_rows(s):
        def body(j, c):
            r0 = pl.multiple_of(j * ROWS_PER_ISSUE, ROWS_PER_ISSUE)
            for rr in range(ROWS_PER_ISSUE):
                for k in range(MOE_TOP_K):
                    row_copy(s, k, r0 + rr, idx_smem[s * n_idx + MOE_TOP_K * (r0 + rr) + k]).start()
            return c
        lax.fori_loop(0, tm // ROWS_PER_ISSUE, body, 0)

    def wait_rows(s):
        def body(r, c):
            for k in range(MOE_TOP_K):
                row_copy(s, k, 0, 0).wait()
            return c
        lax.fori_loop(0, tm, body, 0, unroll=16)

    def step(s):
        @pl.when(i == 0)
        def _():
            _idx_copy(idx_hbm, idx_smem, idx_sem, 0, s).start()
            _idx_copy(idx_hbm, idx_smem, idx_sem, 0, s).wait()
            issue_rows(s)

            @pl.when(nsteps >= 2)
            def _():
                _idx_copy(idx_hbm, idx_smem, idx_sem, 1, 1 - s).start()

        @pl.when(i + 1 < nsteps)
        def _():
            _idx_copy(idx_hbm, idx_smem, idx_sem, i + 1, 1 - s).wait()
            issue_rows(1 - s)

        wait_rows(s)

        @pl.when(i + 2 < nsteps)
        def _():
            _idx_copy(idx_hbm, idx_smem, idx_sem, i + 2, s).start()

        rec = rec_ref[...]
        w1 = rec[:, REC_W:REC_W + 1]
        w2 = rec[:, REC_W + 1:REC_W + 2]
        o_ref[...] = h_ref[...] + w1 * ybuf[s, 0] + w2 * ybuf[s, 1]

    _static_slot(i % 2, step)


def moe_combine(h, rec, idx, ys, *, tm):
    n = h.shape[0]
    est = 4 * tm * D_MODEL * 4 + 2 * tm * LANES * 4 + 2 * MOE_TOP_K * tm * D_MODEL * 4
    return pl.pallas_call(
        _combine_kernel,
        grid=(n // tm,),
        in_specs=[
            pl.BlockSpec((tm, D_MODEL), lambda i: (i, 0)),
            pl.BlockSpec((tm, LANES), lambda i: (i, 0)),
            pl.BlockSpec(memory_space=pl.ANY),
            pl.BlockSpec(memory_space=pl.ANY),
        ],
        out_specs=pl.BlockSpec((tm, D_MODEL), lambda i: (i, 0)),
        out_shape=jax.ShapeDtypeStruct((n, D_MODEL), F32),
        scratch_shapes=[
            pltpu.VMEM((2, MOE_TOP_K, tm, D_MODEL), F32),
            pltpu.SMEM((2 * MOE_TOP_K * tm,), jnp.int32),
            pltpu.SemaphoreType.DMA((2,)),
            pltpu.SemaphoreType.DMA((2,)),
        ],
        compiler_params=pltpu.CompilerParams(
            dimension_semantics=("arbitrary",), vmem_limit_bytes=_vmem(est)),
        name="moe_combine",
    )(h, rec, idx, ys)


def ffn_moe(h, g, wr, wg, wu, wd, layer, *, bm=512, tm=512):
    n = h.shape[0]
    n_sorted = MOE_TOP_K * n + N_EXP * bm
    rec, cnt = moe_route(h, g, wr)
    expert = rec[:, REC_E:REC_E + MOE_TOP_K].astype(jnp.int32)
    rank = rec[:, REC_RANK:REC_RANK + MOE_TOP_K].astype(jnp.int32)
    counts = cnt[0, :N_EXP].astype(jnp.int32)
    padded = (counts + bm - 1) // bm * bm
    pend = jnp.cumsum(padded)
    pstart = pend - padded
    dest = jnp.sum(jnp.where(expert[..., None] == jnp.arange(N_EXP), pstart, 0), axis=-1) + rank
    idx = dest.reshape(MOE_TOP_K * n)
    padinfo = jnp.concatenate([pstart + counts, padded - counts]).astype(jnp.int32)
    n_used = (pend[-1:] // bm).astype(jnp.int32)
    blk_start = jnp.arange(n_sorted // bm, dtype=jnp.int32) * bm
    block_expert = jnp.minimum(jnp.sum(blk_start[:, None] >= pend[None, :], axis=1), N_EXP - 1).astype(jnp.int32)
    xs = moe_dispatch(h, g, idx, padinfo, n_sorted, tm=tm)
    ys = moe_experts(block_expert, n_used, xs, wg, wu, wd, layer, bm=bm)
    return moe_combine(h, rec, idx, ys, tm=tm)


PLE_ROW_CHUNKS = 2


def _ple_kernel(h_ref, p_ref, gi_ref, wgate_ref, wp_ref, go_ref, o_ref):
    rc = h_ref.shape[0] // PLE_ROW_CHUNKS
    rows = [slice(c * rc, (c + 1) * rc) for c in range(PLE_ROW_CHUNKS)]
    hn = [_rms(h_ref[r, :], gi_ref[...]).astype(BF16) for r in rows]
    for c, r in enumerate(rows):
        gate = jax.nn.sigmoid(jnp.dot(hn[c], wgate_ref[...], preferred_element_type=F32))
        e = jnp.dot(p_ref[r, :].astype(BF16), wp_ref[...], preferred_element_type=F32)
        o_ref[r, :] = h_ref[r, :] + gate * _rms(e, go_ref[...])


def ple(h, p_all, layer, gi, wgate, wp, go, *, tm=512):
    n = h.shape[0]
    est = 4 * tm * D_MODEL * 4 + 2 * D_MODEL * D_MODEL * 2 + 2 * PLE_DIM * D_MODEL * 2 + 3 * tm * D_MODEL * 4
    return pl.pallas_call(
        _ple_kernel,
        grid=(n // tm,),
        in_specs=[
            pl.BlockSpec((tm, D_MODEL), lambda i: (i, 0)),
            pl.BlockSpec((None, tm, PLE_DIM), lambda i: (layer, i, 0)),
            pl.BlockSpec((1, D_MODEL), lambda i: (0, 0)),
            pl.BlockSpec((None, D_MODEL, D_MODEL), lambda i: (layer, 0, 0)),
            pl.BlockSpec((None, PLE_DIM, D_MODEL), lambda i: (layer, 0, 0)),
            pl.BlockSpec((1, D_MODEL), lambda i: (0, 0)),
        ],
        out_specs=pl.BlockSpec((tm, D_MODEL), lambda i: (i, 0)),
        out_shape=jax.ShapeDtypeStruct((n, D_MODEL), F32),
        compiler_params=pltpu.CompilerParams(
            dimension_semantics=("parallel",), vmem_limit_bytes=_vmem(est)),
        name="ple",
    )(h, p_all, gi, wgate, wp, go)


def _prep_layer(i, P):
    lr0 = Z_AQ
    n_in = P["w_in"].shape[-1]

    def w_in_cols(a, b):
        return lax.slice(P["w_in"], (i, 0, a), (i + 1, D_MODEL, b)).reshape(D_MODEL, b - a).astype(BF16)

    w_in_r = jnp.concatenate([w_in_cols(0, lr0), w_in_cols(lr0 + 2 * GLA_RANK, n_in)], axis=1)
    w_lr = jnp.concatenate([w_in_cols(lr0, lr0 + 2 * GLA_RANK),
                            jnp.zeros((D_MODEL, LR_W - 2 * GLA_RANK), BF16)], axis=1)
    zpad = jnp.zeros((LR_W, GLA_KW), F32)
    wgf = zpad.at[0:GLA_RANK].set(P["w_gate_f"][i]).astype(BF16)
    wgb = zpad.at[GLA_RANK:2 * GLA_RANK].set(P["w_gate_b"][i]).astype(BF16)
    L = dict(
        g_mix=P["g_mix"][i][None], w_in=w_in_r, w_lr=w_lr, wgf=wgf, wgb=wgb,
        bgf=P["b_gate_f"][i][None], bgb=P["b_gate_b"][i][None],
        gla_norm=P["gla_norm"][i].reshape(GLA_H, 1, GLA_DV),
        q_norm=P["q_norm"][i][None], k_norm=P["k_norm"][i][None],
        bias_tbl=natten_bias_table(P["rpb"][i]), attn_norm=P["attn_norm"][i][None],
        g_ffn=P["g_ffn"][i][None], g_ple_in=P["g_ple_in"][i][None], g_ple_out=P["g_ple_out"][i][None],
    )
    if i % 2 == 1:
        L["wr"] = jnp.zeros((D_MODEL, LANES), F32).at[:, :N_EXP].set(P["w_router"][i // 2]).astype(BF16)
    return L


def _prep_stacked(P):
    names = ("w_o", "w_ple_gate", "w_ple", "w_ff_gate", "w_ff_up", "w_ff_down",
             "w_exp_gate", "w_exp_up", "w_exp_down")
    return {k: P[k].astype(BF16) for k in names}


def _trunk(x, p, layers, W):
    B, T, _ = x.shape
    n = B * T
    h = x.reshape(n, D_MODEL)
    p_all = p.reshape(DEPTH, n, PLE_DIM)
    for i, L in enumerate(layers):
        z, zl = inproj(h, L["g_mix"], L["w_in"], L["w_lr"])
        z3 = z.reshape(B, T, NZ)
        zl3 = zl.reshape(B, T, LR_W)
        o_f = gla_direction(z3, zl3, L["wgf"], L["bgf"], reverse=False)
        o_gla = gla_direction(z3, zl3, L["wgb"], L["bgb"], reverse=True, o_other=o_f, gn=L["gla_norm"])
        o_na = natten(z3, L["bias_tbl"], L["q_norm"], L["k_norm"])
        h = outproj(o_gla.reshape(n, GLA_VW), o_na.reshape(n, NA_W), L["attn_norm"], W["w_o"], i, h)
        if i % 2 == 0:
            h = ffn_dense(h, L["g_ffn"], W["w_ff_gate"], W["w_ff_up"], W["w_ff_down"], i // 2)
        else:
            h = ffn_moe(h, L["g_ffn"], L["wr"], W["w_exp_gate"], W["w_exp_up"], W["w_exp_down"], i // 2)
        h = ple(h, p_all, i, L["g_ple_in"], W["w_ple_gate"], W["w_ple"], L["g_ple_out"])
    return h.reshape(B, T, D_MODEL)


def kernel(x_prompt, x_sample, p_prompt, p_sample, g_mix, w_in, w_gate_f, b_gate_f, w_gate_b, b_gate_b, gla_norm, q_norm, k_norm, rpb, attn_norm, w_o, g_ffn, w_ff_gate, w_ff_up, w_ff_down, w_router, w_exp_gate, w_exp_up, w_exp_down, g_ple_in, w_ple_gate, w_ple, g_ple_out):
    P = dict(g_mix=g_mix, w_in=w_in, w_gate_f=w_gate_f, b_gate_f=b_gate_f, w_gate_b=w_gate_b,
             b_gate_b=b_gate_b, gla_norm=gla_norm, q_norm=q_norm, k_norm=k_norm, rpb=rpb,
             attn_norm=attn_norm, w_o=w_o, g_ffn=g_ffn, w_ff_gate=w_ff_gate, w_ff_up=w_ff_up,
             w_ff_down=w_ff_down, w_router=w_router, w_exp_gate=w_exp_gate, w_exp_up=w_exp_up,
             w_exp_down=w_exp_down, g_ple_in=g_ple_in, w_ple_gate=w_ple_gate, w_ple=w_ple,
             g_ple_out=g_ple_out)
    layers = [_prep_layer(i, P) for i in range(DEPTH)]
    W = _prep_stacked(P)
    return (_trunk(x_prompt, p_prompt, layers, W), _trunk(x_sample, p_sample, layers, W))
```

```python
import functools

import jax
import jax.numpy as jnp
from jax import lax
from jax.experimental import pallas as pl
from jax.experimental.pallas import tpu as pltpu

F32 = jnp.float32
BF16 = jnp.bfloat16

D_MODEL = 2048
DEPTH = 4
GRID_W = 64
GLA_H = 4
GLA_VW = 1024
GLA_KW = 512
GLA_DK = 128
GLA_DV = 256
GLA_RANK = 16
GLA_INV_NORMALIZER = 1.0 / 16.0
GLA_CHUNK = 64
NA_W = 1024
NA_H = 8
NA_HD = 128
WIN_ROWS = 8
WIN_COLS = 16
FF_DIM = 4096
N_EXP = 8
EXP_DIM = 1024
PLE_DIM = 256
EPS = 1e-6
NEG_INF = -1e30

Z_GQ, Z_GK, Z_GV, Z_GG = 0, 512, 1024, 2048
Z_AQ, Z_AK, Z_AV, NZ = 3072, 4096, 5120, 6144
LR_W = 256

V7X_VMEM_BYTES = 64 * 1024 * 1024
LANES = 128


def _vmem(nbytes):
    return int(min(nbytes * 3 // 2 + (4 << 20), V7X_VMEM_BYTES - (6 << 20)))


def _rms(x, g):
    ms = jnp.mean(x * x, axis=-1, keepdims=True)
    return x * lax.rsqrt(ms + EPS) * g


def _silu(x):
    return x * jax.nn.sigmoid(x)


def _inproj_kernel(h_ref, g_ref, w_ref, wl_ref, z_ref, zl_ref, u_scr):
    j = pl.program_id(1)

    @pl.when(j == 0)
    def _():
        u = _rms(h_ref[...], g_ref[...]).astype(BF16)
        u_scr[...] = u
        zl_ref[...] = jnp.dot(u, wl_ref[...], preferred_element_type=F32).astype(zl_ref.dtype)

    @pl.when(j > 0)
    def _():
        z_ref[...] = jnp.dot(u_scr[...], w_ref[...], preferred_element_type=F32).astype(z_ref.dtype)


def inproj(h, g, w, wl, *, tm=1024, tn=1536):
    n = h.shape[0]
    nmain = NZ // tn
    est = (2 * tm * D_MODEL * 4 + 2 * D_MODEL * tn * 2 + 2 * tm * tn * 2 + tm * D_MODEL * 2 + tm * tn * 4
           + 2 * D_MODEL * LR_W * 2 + 2 * tm * LR_W * 2)
    main = lambda j: jnp.maximum(j - 1, 0)
    return pl.pallas_call(
        _inproj_kernel,
        grid=(n // tm, nmain + 1),
        in_specs=[
            pl.BlockSpec((tm, D_MODEL), lambda i, j: (i, 0)),
            pl.BlockSpec((1, D_MODEL), lambda i, j: (0, 0)),
            pl.BlockSpec((D_MODEL, tn), lambda i, j: (0, main(j))),
            pl.BlockSpec((D_MODEL, LR_W), lambda i, j: (0, 0)),
        ],
        out_specs=[pl.BlockSpec((tm, tn), lambda i, j: (i, main(j))),
                   pl.BlockSpec((tm, LR_W), lambda i, j: (i, 0))],
        out_shape=[jax.ShapeDtypeStruct((n, NZ), BF16), jax.ShapeDtypeStruct((n, LR_W), BF16)],
        scratch_shapes=[pltpu.VMEM((tm, D_MODEL), BF16)],
        compiler_params=pltpu.CompilerParams(
            dimension_semantics=("parallel", "arbitrary"), vmem_limit_bytes=_vmem(est)),
        name="inproj",
    )(h, g, w, wl)


def _cumsum_rows(x, seg):
    idx = lax.broadcasted_iota(jnp.int32, x.shape, 0) & (seg - 1)
    s = 1
    while s < seg:
        x = x + jnp.where(idx >= s, pltpu.roll(x, s, 0), 0.0)
        s *= 2
    return x


def _log_sigmoid(x):
    return jnp.minimum(x, 0.0) - jnp.log(1.0 + jnp.exp(-jnp.abs(x)))


def _split_bf16(x):
    hi = x.astype(BF16)
    r1 = x - hi.astype(F32)
    mid = r1.astype(BF16)
    lo = (r1 - mid.astype(F32)).astype(BF16)
    return hi, mid, lo


def _gla_kernel(*refs, reverse, final, tb):
    if final:
        q_ref, k_ref, v_ref, lr_ref, wg_ref, bg_ref, of_ref, gg_ref, gn_ref, o_ref, s_scr = refs
    else:
        q_ref, k_ref, v_ref, lr_ref, wg_ref, bg_ref, o_ref, s_scr = refs
    C = GLA_CHUNK
    nchunk = tb // C

    @pl.when(pl.program_id(2) == 0)
    def _():
        s_scr[...] = jnp.zeros_like(s_scr)

    x = jnp.dot(lr_ref[...], wg_ref[...], preferred_element_type=F32) + bg_ref[...]
    la_parts = _split_bf16(_log_sigmoid(x) * GLA_INV_NORMALIZER)

    row = lax.broadcasted_iota(jnp.int32, (C, C), 0)
    col = lax.broadcasted_iota(jnp.int32, (C, C), 1)
    mask = (col >= row) if reverse else (col <= row)
    tri = jnp.where(mask, 1.0, 0.0).astype(BF16)
    scale = GLA_DK ** -0.5

    order = [(nchunk - 1 - ci) if reverse else ci for ci in range(nchunk)]
    sl = [slice(c * C, (c + 1) * C) for c in range(nchunk)]
    qts, atts, incs, dls = {}, {}, {}, {}
    bs = {c: sum(jnp.dot(tri, part[sl[c]], preferred_element_type=F32) for part in la_parts) for c in order}
    for c in order:
        b = bs[c]
        blast = b[0:1, :] if reverse else b[C - 1:C, :]
        q = q_ref[sl[c], :].astype(F32)
        k = k_ref[sl[c], :].astype(F32)
        qts[c] = (q * jnp.exp(b) * scale).astype(BF16)
        kdec = k * jnp.exp(-b)
        dl = jnp.exp(blast)
        kt = kdec.astype(BF16)
        ke_t = (kdec * dl).T.astype(BF16)
        dls[c] = jnp.broadcast_to(dl, (8, GLA_DK)).T[:, 0:1]
        att = lax.dot_general(qts[c], kt, (((1,), (1,)), ((), ())), preferred_element_type=F32)
        atts[c] = jnp.where(mask, att, 0.0).astype(BF16)
        incs[c] = jnp.dot(ke_t, v_ref[sl[c], :], preferred_element_type=F32)
    state = s_scr[...]
    states = {}
    for c in order:
        states[c] = state.astype(BF16)
        state = dls[c] * state + incs[c]
    s_scr[...] = state
    for c in order:
        o = (jnp.dot(atts[c], v_ref[sl[c], :], preferred_element_type=F32)
             + jnp.dot(qts[c], states[c], preferred_element_type=F32))
        if final:
            y = _rms(o + of_ref[sl[c], :], gn_ref[...])
            gg = gg_ref[sl[c], :].astype(F32)
            o_ref[sl[c], :] = (y * _silu(gg)).astype(o_ref.dtype)
        else:
            o_ref[sl[c], :] = o


def gla_direction(z3, zl3, wg, bg, *, reverse, o_other=None, gn=None, tb=2048):
    B, T, _ = z3.shape
    nb = T // tb
    final = o_other is not None
    blk = (lambda j: nb - 1 - j) if reverse else (lambda j: j)
    qw, vw = GLA_DK, GLA_DV
    in_specs = [
        pl.BlockSpec((None, tb, qw), lambda b, h, j: (b, blk(j), Z_GQ // qw + h)),
        pl.BlockSpec((None, tb, qw), lambda b, h, j: (b, blk(j), Z_GK // qw + h)),
        pl.BlockSpec((None, tb, vw), lambda b, h, j: (b, blk(j), Z_GV // vw + h)),
        pl.BlockSpec((None, tb, LR_W), lambda b, h, j: (b, blk(j), 0)),
        pl.BlockSpec((LR_W, qw), lambda b, h, j: (0, h)),
        pl.BlockSpec((1, qw), lambda b, h, j: (0, h)),
    ]
    args = [z3, z3, z3, zl3, wg, bg]
    if final:
        in_specs += [
            pl.BlockSpec((None, tb, vw), lambda b, h, j: (b, blk(j), h)),
            pl.BlockSpec((None, tb, vw), lambda b, h, j: (b, blk(j), Z_GG // vw + h)),
            pl.BlockSpec((None, 1, vw), lambda b, h, j: (h, 0, 0)),
        ]
        args += [o_other, z3, gn]
    out_dtype = BF16 if final else F32
    est = tb * (2 * 2 * (2 * qw + vw + LR_W) + (2 * (4 + 2) * vw if final else 0) + 2 * 4 * vw + 10 * 4 * qw)
    return pl.pallas_call(
        functools.partial(_gla_kernel, reverse=reverse, final=final, tb=tb),
        grid=(B, GLA_H, nb),
        in_specs=in_specs,
        out_specs=pl.BlockSpec((None, tb, vw), lambda b, h, j: (b, blk(j), h)),
        out_shape=jax.ShapeDtypeStruct((B, T, GLA_VW), out_dtype),
        scratch_shapes=[pltpu.VMEM((GLA_DK, GLA_DV), F32)],
        compiler_params=pltpu.CompilerParams(
            dimension_semantics=("parallel", "parallel", "arbitrary"), vmem_limit_bytes=_vmem(est)),
        name="gla_bwd" if reverse else "gla_fwd",
    )(*args)


def _natten_kernel(q_ref, k_ref, v_ref, bias_ref, qg_ref, kg_ref, o_ref, kn_scr, *, rows, rq):
    T = rows * GRID_W
    KB = 2048
    WK = WIN_ROWS * GRID_W
    j = pl.program_id(2)

    @pl.when(j == 0)
    def _():
        def norm_block(i, carry):
            off = pl.multiple_of(i * KB, KB)
            kk = k_ref[pl.ds(off, KB), :].astype(F32)
            kn_scr[pl.ds(off, KB), :] = _rms(kk, kg_ref[...]).astype(BF16)
            return carry

        lax.fori_loop(0, T // KB, norm_block, 0)

    qn_blk = (_rms(q_ref[...].astype(F32), qg_ref[...]) * (NA_HD ** -0.5)).astype(BF16)
    koffs, probs, sums = [], [], []
    for i in range(rq):
        r = j * rq + i
        rs = jnp.clip(r - WIN_ROWS // 2, 0, rows - WIN_ROWS)
        dti = rs - r + (WIN_ROWS - 1)
        koffs.append(pl.multiple_of(rs * GRID_W, GRID_W))
        kw = kn_scr[pl.ds(koffs[i], WK), :]
        s = lax.dot_general(qn_blk[i * GRID_W:(i + 1) * GRID_W], kw, (((1,), (1,)), ((), ())),
                            preferred_element_type=F32)
        s = s + bias_ref[dti]
        p = jnp.exp(s - jnp.max(s, axis=-1, keepdims=True))
        sums.append(jnp.sum(p, axis=-1, keepdims=True))
        probs.append(p.astype(BF16))
    for i in range(rq):
        vw = v_ref[pl.ds(koffs[i], WK), :]
        o = jnp.dot(probs[i], vw, preferred_element_type=F32) / sums[i]
        o_ref[i * GRID_W:(i + 1) * GRID_W, :] = o.astype(o_ref.dtype)


def natten(z3, bias_tbl, qg, kg, *, rq=32):
    B, T, _ = z3.shape
    rows = T // GRID_W
    assert rows >= WIN_ROWS and rows % rq == 0
    tq = rq * GRID_W
    hd = NA_HD
    est = 2 * 2 * T * hd * 2 + T * hd * 2 + 2 * WIN_ROWS * GRID_W * WIN_ROWS * GRID_W * 4 + 4 * tq * hd * 2
    return pl.pallas_call(
        functools.partial(_natten_kernel, rows=rows, rq=rq),
        grid=(B, NA_H, rows // rq),
        in_specs=[
            pl.BlockSpec((None, tq, hd), lambda b, h, j: (b, j, Z_AQ // hd + h)),
            pl.BlockSpec((None, T, hd), lambda b, h, j: (b, 0, Z_AK // hd + h)),
            pl.BlockSpec((None, T, hd), lambda b, h, j: (b, 0, Z_AV // hd + h)),
            pl.BlockSpec((None, WIN_ROWS, GRID_W, WIN_ROWS * GRID_W), lambda b, h, j: (h, 0, 0, 0)),
            pl.BlockSpec((1, hd), lambda b, h, j: (0, 0)),
            pl.BlockSpec((1, hd), lambda b, h, j: (0, 0)),
        ],
        out_specs=pl.BlockSpec((None, tq, hd), lambda b, h, j: (b, j, h)),
        out_shape=jax.ShapeDtypeStruct((B, T, NA_W), BF16),
        scratch_shapes=[pltpu.VMEM((T, hd), BF16)],
        compiler_params=pltpu.CompilerParams(
            dimension_semantics=("parallel", "parallel", "arbitrary"), vmem_limit_bytes=_vmem(est)),
        name="natten",
    )(z3, z3, z3, bias_tbl, qg, kg)


def natten_bias_table(rpb):
    c = jnp.arange(GRID_W, dtype=jnp.int32)
    c_start = jnp.clip(c - WIN_COLS // 2, 0, GRID_W - WIN_COLS)
    valid = (c[None, :] >= c_start[:, None]) & (c[None, :] < c_start[:, None] + WIN_COLS)
    dc = jnp.clip(c[None, :] - c[:, None], -(WIN_COLS - 1), WIN_COLS - 1) + (WIN_COLS - 1)
    bias_c = jnp.where(valid, rpb.astype(F32)[:, :, dc], NEG_INF)
    dr = jnp.arange(WIN_ROWS)[:, None] + jnp.arange(WIN_ROWS)[None, :]
    tbl = bias_c[:, dr]
    return tbl.transpose(0, 1, 3, 2, 4).reshape(NA_H, WIN_ROWS, GRID_W, WIN_ROWS * GRID_W)


def _outproj_kernel(og_ref, on_ref, an_ref, w_ref, h_ref, o_ref):
    na = _rms(on_ref[...].astype(F32), an_ref[...]).astype(BF16)
    acc = jnp.dot(og_ref[...], w_ref[0:GLA_VW, :], preferred_element_type=F32)
    acc += jnp.dot(na, w_ref[GLA_VW:GLA_VW + NA_W, :], preferred_element_type=F32)
    o_ref[...] = h_ref[...] + acc


def outproj(o_gla, o_na, an, w_o, layer, h, *, tm=512):
    n = h.shape[0]
    est = 2 * 2 * tm * 1024 * 2 + 2 * D_MODEL * D_MODEL * 2 + 4 * tm * D_MODEL * 4 + tm * D_MODEL * 4
    return pl.pallas_call(
        _outproj_kernel,
        grid=(n // tm,),
        in_specs=[
            pl.BlockSpec((tm, GLA_VW), lambda i: (i, 0)),
            pl.BlockSpec((tm, NA_W), lambda i: (i, 0)),
            pl.BlockSpec((1, NA_W), lambda i: (0, 0)),
            pl.BlockSpec((None, D_MODEL, D_MODEL), lambda i: (layer, 0, 0)),
            pl.BlockSpec((tm, D_MODEL), lambda i: (i, 0)),
        ],
        out_specs=pl.BlockSpec((tm, D_MODEL), lambda i: (i, 0)),
        out_shape=jax.ShapeDtypeStruct((n, D_MODEL), F32),
        compiler_params=pltpu.CompilerParams(
            dimension_semantics=("parallel",), vmem_limit_bytes=_vmem(est)),
        name="outproj",
    )(o_gla, o_na, an, w_o, h)


FFN_TF = 512


def _ffn_kernel(h_ref, g_ref, wg_ref, wu_ref, wd_ref, o_ref, u_scr):
    @pl.when(pl.program_id(1) == 0)
    def _():
        h = h_ref[...]
        u_scr[...] = _rms(h, g_ref[...]).astype(BF16)
        o_ref[...] = h

    u = u_scr[...]
    a = jnp.dot(u, wg_ref[...], preferred_element_type=F32)
    b = jnp.dot(u, wu_ref[...], preferred_element_type=F32)
    hid = (_silu(a) * b).astype(BF16)
    o_ref[...] += jnp.dot(hid, wd_ref[...], preferred_element_type=F32)


def ffn_dense(h, g, wg, wu, wd, layer, *, tm=1024, tf=FFN_TF):
    n = h.shape[0]
    est = 4 * tm * D_MODEL * 4 + tm * D_MODEL * 2 + 2 * 3 * D_MODEL * tf * 2 + 3 * tm * tf * 4
    return pl.pallas_call(
        _ffn_kernel,
        grid=(n // tm, FF_DIM // tf),
        in_specs=[
            pl.BlockSpec((tm, D_MODEL), lambda i, f: (i, 0)),
            pl.BlockSpec((1, D_MODEL), lambda i, f: (0, 0)),
            pl.BlockSpec((None, D_MODEL, tf), lambda i, f: (layer, 0, f)),
            pl.BlockSpec((None, D_MODEL, tf), lambda i, f: (layer, 0, f)),
            pl.BlockSpec((None, tf, D_MODEL), lambda i, f: (layer, f, 0)),
        ],
        out_specs=pl.BlockSpec((tm, D_MODEL), lambda i, f: (i, 0)),
        out_shape=jax.ShapeDtypeStruct((n, D_MODEL), F32),
        scratch_shapes=[pltpu.VMEM((tm, D_MODEL), BF16)],
        compiler_params=pltpu.CompilerParams(
            dimension_semantics=("parallel", "arbitrary"), vmem_limit_bytes=_vmem(est)),
        name="ffn_dense",
    )(h, g, wg, wu, wd)


REC_E, REC_W, REC_RANK = 0, 2, 4
MOE_TOP_K = 2


def _route_kernel(h_ref, g_ref, wr_ref, rec_ref, cnt_ref, carry_scr):
    tm = h_ref.shape[0]
    lane = lax.broadcasted_iota(jnp.int32, (tm, LANES), 1)

    @pl.when(pl.program_id(0) == 0)
    def _():
        carry_scr[...] = jnp.zeros_like(carry_scr)

    u = _rms(h_ref[...], g_ref[...]).astype(BF16)
    logits = jnp.dot(u, wr_ref[...], preferred_element_type=F32)
    lg = jnp.where(lane < N_EXP, logits, -jnp.inf)
    v1 = jnp.max(lg, axis=-1, keepdims=True)
    i1 = jnp.min(jnp.where(lg == v1, lane, LANES), axis=-1, keepdims=True)
    lg2 = jnp.where(lane == i1, -jnp.inf, lg)
    v2 = jnp.max(lg2, axis=-1, keepdims=True)
    i2 = jnp.min(jnp.where(lg2 == v2, lane, LANES), axis=-1, keepdims=True)
    e2 = jnp.exp(v2 - v1)
    w1 = 1.0 / (1.0 + e2)
    w2 = e2 / (1.0 + e2)
    onehot = jnp.where(lane == i1, 1.0, 0.0) + jnp.where(lane == i2, 1.0, 0.0)
    incl = _cumsum_rows(onehot, tm)
    before = incl - onehot + carry_scr[0:1, :]
    r1 = jnp.sum(jnp.where(lane == i1, before, 0.0), axis=-1, keepdims=True)
    r2 = jnp.sum(jnp.where(lane == i2, before, 0.0), axis=-1, keepdims=True)
    rec = jnp.where(lane == REC_E, i1.astype(F32), 0.0)
    rec = jnp.where(lane == REC_E + 1, i2.astype(F32), rec)
    rec = jnp.where(lane == REC_W, w1, rec)
    rec = jnp.where(lane == REC_W + 1, w2, rec)
    rec = jnp.where(lane == REC_RANK, r1, rec)
    rec = jnp.where(lane == REC_RANK + 1, r2, rec)
    rec_ref[...] = rec
    carry_scr[...] = carry_scr[...] + incl[tm - 1:tm, :]
    cnt_ref[...] = carry_scr[...]


def moe_route(h, g, wr, *, tm=512):
    n = h.shape[0]
    est = 2 * tm * D_MODEL * 4 + D_MODEL * LANES * 4 + 8 * tm * LANES * 4
    return pl.pallas_call(
        _route_kernel,
        grid=(n // tm,),
        in_specs=[
            pl.BlockSpec((tm, D_MODEL), lambda i: (i, 0)),
            pl.BlockSpec((1, D_MODEL), lambda i: (0, 0)),
            pl.BlockSpec((D_MODEL, LANES), lambda i: (0, 0)),
        ],
        out_specs=[pl.BlockSpec((tm, LANES), lambda i: (i, 0)), pl.BlockSpec((8, LANES), lambda i: (0, 0))],
        out_shape=[jax.ShapeDtypeStruct((n, LANES), F32), jax.ShapeDtypeStruct((8, LANES), F32)],
        scratch_shapes=[pltpu.VMEM((8, LANES), F32)],
        compiler_params=pltpu.CompilerParams(
            dimension_semantics=("arbitrary",), vmem_limit_bytes=_vmem(est)),
        name="moe_route",
    )(h, g, wr)


def _static_slot(slot, fn):
    for s in range(2):
        pl.when(slot == s)(functools.partial(fn, s))


ROWS_PER_ISSUE = 8


def _idx_copy(idx_hbm, idx_smem, sem, step, s):
    n = idx_smem.shape[0] // 2
    return pltpu.make_async_copy(idx_hbm.at[pl.ds(pl.multiple_of(step * n, n), n)],
                                 idx_smem.at[pl.ds(s * n, n)], sem.at[s])


def _dispatch_kernel(pad_ref, h_ref, g_ref, idx_hbm, xs_ref, ubuf, zrow, idx_smem, row_sem, idx_sem, pad_sem):
    tm = h_ref.shape[0]
    i = pl.program_id(0)
    nsteps = pl.num_programs(0)
    n_idx = idx_smem.shape[0] // 2

    def row_copy(s, j, rr, dst_row):
        return pltpu.make_async_copy(ubuf.at[s, j, pl.ds(rr, 1)], xs_ref.at[pl.ds(dst_row, 1)], row_sem.at[s])

    def wait_rows(s):
        def body(r, c):
            for _ in range(MOE_TOP_K):
                row_copy(s, 0, 0, 0).wait()
            return c
        lax.fori_loop(0, tm, body, 0, unroll=16)

    def issue_rows(s):
        def body(j, c):
            base = s * n_idx + MOE_TOP_K * ROWS_PER_ISSUE * j
            for rr in range(ROWS_PER_ISSUE):
                for k in range(MOE_TOP_K):
                    row_copy(s, j, rr, idx_smem[base + MOE_TOP_K * rr + k]).start()
            return c
        lax.fori_loop(0, tm // ROWS_PER_ISSUE, body, 0)

    def zero_pad_rows():
        zrow[...] = jnp.zeros_like(zrow)
        for e in range(N_EXP):
            def pad_copy(j, e=e):
                return pltpu.make_async_copy(zrow.at[pl.ds(0, 1)], xs_ref.at[pl.ds(pad_ref[e] + j, 1)], pad_sem.at[0])

            def start(j, c, pad_copy=pad_copy):
                pad_copy(j).start()
                return c

            def wait(j, c, pad_copy=pad_copy):
                pad_copy(j).wait()
                return c
            lax.fori_loop(0, pad_ref[N_EXP + e], start, 0)
            lax.fori_loop(0, pad_ref[N_EXP + e], wait, 0)

    def step(s):
        @pl.when(i == 0)
        def _():
            _idx_copy(idx_hbm, idx_smem, idx_sem, 0, s).start()

        _idx_copy(idx_hbm, idx_smem, idx_sem, i, s).wait()

        @pl.when(i + 1 < nsteps)
        def _():
            _idx_copy(idx_hbm, idx_smem, idx_sem, i + 1, 1 - s).start()

        @pl.when(i >= 2)
        def _():
            wait_rows(s)

        ubuf[s] = _rms(h_ref[...], g_ref[...]).reshape(tm // ROWS_PER_ISSUE, ROWS_PER_ISSUE, D_MODEL)
        issue_rows(s)

        @pl.when(i == nsteps - 1)
        def _():
            zero_pad_rows()
            wait_rows(s)

            @pl.when(nsteps >= 2)
            def _():
                wait_rows(1 - s)

    _static_slot(i % 2, step)


def moe_dispatch(h, g, idx, padinfo, n_sorted, *, tm):
    n = h.shape[0]
    est = 2 * tm * D_MODEL * 4 + 2 * tm * D_MODEL * 4 + 8 * D_MODEL * 4
    return pl.pallas_call(
        _dispatch_kernel,
        grid_spec=pltpu.PrefetchScalarGridSpec(
            num_scalar_prefetch=1,
            grid=(n // tm,),
            in_specs=[
                pl.BlockSpec((tm, D_MODEL), lambda i, pad: (i, 0)),
                pl.BlockSpec((1, D_MODEL), lambda i, pad: (0, 0)),
                pl.BlockSpec(memory_space=pl.ANY),
            ],
            out_specs=pl.BlockSpec(memory_space=pl.ANY),
            scratch_shapes=[
                pltpu.VMEM((2, tm // ROWS_PER_ISSUE, ROWS_PER_ISSUE, D_MODEL), F32),
                pltpu.VMEM((8, D_MODEL), F32),
                pltpu.SMEM((2 * MOE_TOP_K * tm,), jnp.int32),
                pltpu.SemaphoreType.DMA((2,)),
                pltpu.SemaphoreType.DMA((2,)),
                pltpu.SemaphoreType.DMA((1,)),
            ],
        ),
        out_shape=jax.ShapeDtypeStruct((n_sorted, D_MODEL), F32),
        compiler_params=pltpu.CompilerParams(
            dimension_semantics=("arbitrary",), vmem_limit_bytes=_vmem(est)),
        name="moe_dispatch",
    )(padinfo, h, g, idx)


def _experts_kernel(be_ref, nu_ref, x_ref, wg_ref, wu_ref, wd_ref, y_ref):
    del be_ref
    used = pl.program_id(0) < nu_ref[0]

    @pl.when(used)
    def _():
        x = x_ref[...].astype(BF16)
        a = jnp.dot(x, wg_ref[...], preferred_element_type=F32)
        b = jnp.dot(x, wu_ref[...], preferred_element_type=F32)
        hid = (_silu(a) * b).astype(BF16)
        y_ref[...] = jnp.dot(hid, wd_ref[...], preferred_element_type=F32)

    @pl.when(jnp.logical_not(used))
    def _():
        y_ref[...] = jnp.zeros_like(y_ref)


def moe_experts(block_expert, n_used, xs, wg, wu, wd, layer, *, bm):
    n_sorted = xs.shape[0]
    est = 4 * bm * D_MODEL * 4 + 2 * 3 * D_MODEL * EXP_DIM * 2 + 3 * bm * EXP_DIM * 4 + bm * D_MODEL * 2
    row_blk = lambda i, be, nu: (jnp.where(i < nu[0], i, 0), 0)
    return pl.pallas_call(
        _experts_kernel,
        grid_spec=pltpu.PrefetchScalarGridSpec(
            num_scalar_prefetch=2,
            grid=(n_sorted // bm,),
            in_specs=[
                pl.BlockSpec((bm, D_MODEL), row_blk),
                pl.BlockSpec((None, None, D_MODEL, EXP_DIM), lambda i, be, nu: (layer, be[i], 0, 0)),
                pl.BlockSpec((None, None, D_MODEL, EXP_DIM), lambda i, be, nu: (layer, be[i], 0, 0)),
                pl.BlockSpec((None, None, EXP_DIM, D_MODEL), lambda i, be, nu: (layer, be[i], 0, 0)),
            ],
            out_specs=pl.BlockSpec((bm, D_MODEL), lambda i, be, nu: (i, 0)),
        ),
        out_shape=jax.ShapeDtypeStruct((n_sorted, D_MODEL), F32),
        compiler_params=pltpu.CompilerParams(
            dimension_semantics=("arbitrary",), vmem_limit_bytes=_vmem(est)),
        name="moe_experts",
    )(block_expert, n_used, xs, wg, wu, wd)


def _combine_kernel(h_ref, rec_ref, idx_hbm, ys_ref, o_ref, ybuf, idx_smem, row_sem, idx_sem):
    tm = h_ref.shape[0]
    i = pl.program_id(0)
    nsteps = pl.num_programs(0)
    n_idx = idx_smem.shape[0] // 2

    def row_copy(s, k, j, rr, src_row):
        return pltpu.make_async_copy(ys_ref.at[pl.ds(src_row, 1)], ybuf.at[s, k, j, pl.ds(rr, 1)], row_sem.at[s])

    def issue_rows(s):
        def body(j, c):
            base = s * n_idx + MOE_TOP_K * ROWS_PER_ISSUE * j
            for rr in range(ROWS_PER_ISSUE):
                for k in range(MOE_TOP_K):
                    row_copy(s, k, j, rr, idx_smem[base + MOE_TOP_K * rr + k]).start()
            return c
        lax.fori_loop(0, tm // ROWS_PER_ISSUE, body, 0)

    def wait_rows(s):
        def body(r, c):
            for k in range(MOE_TOP_K):
                row_copy(s, k, 0, 0, 0).wait()
            return c
        lax.fori_loop(0, tm, body, 0, unroll=16)

    def step(s):
        @pl.when(i == 0)
        def _():
            _idx_copy(idx_hbm, idx_smem, idx_sem, 0, s).start()
            _idx_copy(idx_hbm, idx_smem, idx_sem, 0, s).wait()
            issue_rows(s)

            @pl.when(nsteps >= 2)
            def _():
                _idx_copy(idx_hbm, idx_smem, idx_sem, 1, 1 - s).start()

        @pl.when(i + 1 < nsteps)
        def _():
            _idx_copy(idx_hbm, idx_smem, idx_sem, i + 1, 1 - s).wait()
            issue_rows(1 - s)

        wait_rows(s)

        @pl.when(i + 2 < nsteps)
        def _():
            _idx_copy(idx_hbm, idx_smem, idx_sem, i + 2, s).start()

        rec = rec_ref[...]
        w1 = rec[:, REC_W:REC_W + 1]
        w2 = rec[:, REC_W + 1:REC_W + 2]
        y1 = ybuf[s, 0].reshape(tm, D_MODEL)
        y2 = ybuf[s, 1].reshape(tm, D_MODEL)
        o_ref[...] = h_ref[...] + w1 * y1 + w2 * y2

    _static_slot(i % 2, step)


def moe_combine(h, rec, idx, ys, *, tm):
    n = h.shape[0]
    est = 4 * tm * D_MODEL * 4 + 2 * tm * LANES * 4 + 2 * MOE_TOP_K * tm * D_MODEL * 4
    return pl.pallas_call(
        _combine_kernel,
        grid=(n // tm,),
        in_specs=[
            pl.BlockSpec((tm, D_MODEL), lambda i: (i, 0)),
            pl.BlockSpec((tm, LANES), lambda i: (i, 0)),
            pl.BlockSpec(memory_space=pl.ANY),
            pl.BlockSpec(memory_space=pl.ANY),
        ],
        out_specs=pl.BlockSpec((tm, D_MODEL), lambda i: (i, 0)),
        out_shape=jax.ShapeDtypeStruct((n, D_MODEL), F32),
        scratch_shapes=[
            pltpu.VMEM((2, MOE_TOP_K, tm // ROWS_PER_ISSUE, ROWS_PER_ISSUE, D_MODEL), F32),
            pltpu.SMEM((2 * MOE_TOP_K * tm,), jnp.int32),
            pltpu.SemaphoreType.DMA((2,)),
            pltpu.SemaphoreType.DMA((2,)),
        ],
        compiler_params=pltpu.CompilerParams(
            dimension_semantics=("arbitrary",), vmem_limit_bytes=_vmem(est)),
        name="moe_combine",
    )(h, rec, idx, ys)


def ffn_moe(h, g, wr, wg, wu, wd, layer, *, bm=512, tm=512):
    n = h.shape[0]
    n_sorted = MOE_TOP_K * n + N_EXP * bm
    rec, cnt = moe_route(h, g, wr)
    expert = rec[:, REC_E:REC_E + MOE_TOP_K].astype(jnp.int32)
    rank = rec[:, REC_RANK:REC_RANK + MOE_TOP_K].astype(jnp.int32)
    counts = cnt[0, :N_EXP].astype(jnp.int32)
    padded = (counts + bm - 1) // bm * bm
    pend = jnp.cumsum(padded)
    pstart = pend - padded
    dest = jnp.sum(jnp.where(expert[..., None] == jnp.arange(N_EXP), pstart, 0), axis=-1) + rank
    idx = dest.reshape(MOE_TOP_K * n)
    padinfo = jnp.concatenate([pstart + counts, padded - counts]).astype(jnp.int32)
    n_used = (pend[-1:] // bm).astype(jnp.int32)
    blk_start = jnp.arange(n_sorted // bm, dtype=jnp.int32) * bm
    block_expert = jnp.minimum(jnp.sum(blk_start[:, None] >= pend[None, :], axis=1), N_EXP - 1).astype(jnp.int32)
    xs = moe_dispatch(h, g, idx, padinfo, n_sorted, tm=tm)
    ys = moe_experts(block_expert, n_used, xs, wg, wu, wd, layer, bm=bm)
    return moe_combine(h, rec, idx, ys, tm=tm)


PLE_ROW_CHUNKS = 2


def _ple_kernel(h_ref, p_ref, gi_ref, wgate_ref, wp_ref, go_ref, o_ref):
    rc = h_ref.shape[0] // PLE_ROW_CHUNKS
    rows = [slice(c * rc, (c + 1) * rc) for c in range(PLE_ROW_CHUNKS)]
    hn = [_rms(h_ref[r, :], gi_ref[...]).astype(BF16) for r in rows]
    for c, r in enumerate(rows):
        gate = jax.nn.sigmoid(jnp.dot(hn[c], wgate_ref[...], preferred_element_type=F32))
        e = jnp.dot(p_ref[r, :].astype(BF16), wp_ref[...], preferred_element_type=F32)
        o_ref[r, :] = h_ref[r, :] + gate * _rms(e, go_ref[...])


def ple(h, p_all, layer, gi, wgate, wp, go, *, tm=512):
    n = h.shape[0]
    est = 4 * tm * D_MODEL * 4 + 2 * D_MODEL * D_MODEL * 2 + 2 * PLE_DIM * D_MODEL * 2 + 3 * tm * D_MODEL * 4
    return pl.pallas_call(
        _ple_kernel,
        grid=(n // tm,),
        in_specs=[
            pl.BlockSpec((tm, D_MODEL), lambda i: (i, 0)),
            pl.BlockSpec((None, tm, PLE_DIM), lambda i: (layer, i, 0)),
            pl.BlockSpec((1, D_MODEL), lambda i: (0, 0)),
            pl.BlockSpec((None, D_MODEL, D_MODEL), lambda i: (layer, 0, 0)),
            pl.BlockSpec((None, PLE_DIM, D_MODEL), lambda i: (layer, 0, 0)),
            pl.BlockSpec((1, D_MODEL), lambda i: (0, 0)),
        ],
        out_specs=pl.BlockSpec((tm, D_MODEL), lambda i: (i, 0)),
        out_shape=jax.ShapeDtypeStruct((n, D_MODEL), F32),
        compiler_params=pltpu.CompilerParams(
            dimension_semantics=("parallel",), vmem_limit_bytes=_vmem(est)),
        name="ple",
    )(h, p_all, gi, wgate, wp, go)


def _prep_layer(i, P):
    lr0 = Z_AQ
    n_in = P["w_in"].shape[-1]

    def w_in_cols(a, b):
        return lax.slice(P["w_in"], (i, 0, a), (i + 1, D_MODEL, b)).reshape(D_MODEL, b - a).astype(BF16)

    w_in_r = jnp.concatenate([w_in_cols(0, lr0), w_in_cols(lr0 + 2 * GLA_RANK, n_in)], axis=1)
    w_lr = jnp.concatenate([w_in_cols(lr0, lr0 + 2 * GLA_RANK),
                            jnp.zeros((D_MODEL, LR_W - 2 * GLA_RANK), BF16)], axis=1)
    zpad = jnp.zeros((LR_W, GLA_KW), F32)
    wgf = zpad.at[0:GLA_RANK].set(P["w_gate_f"][i]).astype(BF16)
    wgb = zpad.at[GLA_RANK:2 * GLA_RANK].set(P["w_gate_b"][i]).astype(BF16)
    L = dict(
        g_mix=P["g_mix"][i][None], w_in=w_in_r, w_lr=w_lr, wgf=wgf, wgb=wgb,
        bgf=P["b_gate_f"][i][None], bgb=P["b_gate_b"][i][None],
        gla_norm=P["gla_norm"][i].reshape(GLA_H, 1, GLA_DV),
        q_norm=P["q_norm"][i][None], k_norm=P["k_norm"][i][None],
        bias_tbl=natten_bias_table(P["rpb"][i]), attn_norm=P["attn_norm"][i][None],
        g_ffn=P["g_ffn"][i][None], g_ple_in=P["g_ple_in"][i][None], g_ple_out=P["g_ple_out"][i][None],
    )
    if i % 2 == 1:
        L["wr"] = jnp.zeros((D_MODEL, LANES), F32).at[:, :N_EXP].set(P["w_router"][i // 2]).astype(BF16)
    return L


def _prep_stacked(P):
    names = ("w_o", "w_ple_gate", "w_ple", "w_ff_gate", "w_ff_up", "w_ff_down",
             "w_exp_gate", "w_exp_up", "w_exp_down")
    return {k: P[k].astype(BF16) for k in names}


def _trunk(x, p, layers, W):
    B, T, _ = x.shape
    n = B * T
    h = x.reshape(n, D_MODEL)
    p_all = p.reshape(DEPTH, n, PLE_DIM)
    for i, L in enumerate(layers):
        z, zl = inproj(h, L["g_mix"], L["w_in"], L["w_lr"])
        z3 = z.reshape(B, T, NZ)
        zl3 = zl.reshape(B, T, LR_W)
        o_f = gla_direction(z3, zl3, L["wgf"], L["bgf"], reverse=False)
        o_gla = gla_direction(z3, zl3, L["wgb"], L["bgb"], reverse=True, o_other=o_f, gn=L["gla_norm"])
        o_na = natten(z3, L["bias_tbl"], L["q_norm"], L["k_norm"])
        h = outproj(o_gla.reshape(n, GLA_VW), o_na.reshape(n, NA_W), L["attn_norm"], W["w_o"], i, h)
        if i % 2 == 0:
            h = ffn_dense(h, L["g_ffn"], W["w_ff_gate"], W["w_ff_up"], W["w_ff_down"], i // 2)
        else:
            h = ffn_moe(h, L["g_ffn"], L["wr"], W["w_exp_gate"], W["w_exp_up"], W["w_exp_down"], i // 2)
        h = ple(h, p_all, i, L["g_ple_in"], W["w_ple_gate"], W["w_ple"], L["g_ple_out"])
    return h.reshape(B, T, D_MODEL)


def kernel(x_prompt, x_sample, p_prompt, p_sample, g_mix, w_in, w_gate_f, b_gate_f, w_gate_b, b_gate_b, gla_norm, q_norm, k_norm, rpb, attn_norm, w_o, g_ffn, w_ff_gate, w_ff_up, w_ff_down, w_router, w_exp_gate, w_exp_up, w_exp_down, g_ple_in, w_ple_gate, w_ple, g_ple_out):
    P = dict(g_mix=g_mix, w_in=w_in, w_gate_f=w_gate_f, b_gate_f=b_gate_f, w_gate_b=w_gate_b,
             b_gate_b=b_gate_b, gla_norm=gla_norm, q_norm=q_norm, k_norm=k_norm, rpb=rpb,
             attn_norm=attn_norm, w_o=w_o, g_ffn=g_ffn, w_ff_gate=w_ff_gate, w_ff_up=w_ff_up,
             w_ff_down=w_ff_down, w_router=w_router, w_exp_gate=w_exp_gate, w_exp_up=w_exp_up,
             w_exp_down=w_exp_down, g_ple_in=g_ple_in, w_ple_gate=w_ple_gate, w_ple=w_ple,
             g_ple_out=g_ple_out)
    layers = [_prep_layer(i, P) for i in range(DEPTH)]
    W = _prep_stacked(P)
    return (_trunk(x_prompt, p_prompt, layers, W), _trunk(x_sample, p_sample, layers, W))
```

```python
import functools

import jax
import jax.numpy as jnp
from jax import lax
from jax.experimental import pallas as pl
from jax.experimental.pallas import tpu as pltpu

F32 = jnp.float32
BF16 = jnp.bfloat16

D_MODEL = 2048
DEPTH = 4
GRID_W = 64
GLA_H = 4
GLA_VW = 1024
GLA_KW = 512
GLA_DK = 128
GLA_DV = 256
GLA_RANK = 16
GLA_INV_NORMALIZER = 1.0 / 16.0
GLA_CHUNK = 64
NA_W = 1024
NA_H = 8
NA_HD = 128
WIN_ROWS = 8
WIN_COLS = 16
FF_DIM = 4096
N_EXP = 8
EXP_DIM = 1024
PLE_DIM = 256
EPS = 1e-6
NEG_INF = -1e30

Z_GQ, Z_GK, Z_GV, Z_GG = 0, 512, 1024, 2048
Z_AQ, Z_AK, Z_AV, NZ = 3072, 4096, 5120, 6144
LR_W = 256

V7X_VMEM_BYTES = 64 * 1024 * 1024
LANES = 128


def _vmem(nbytes):
    return int(min(nbytes * 3 // 2 + (4 << 20), V7X_VMEM_BYTES - (6 << 20)))


def _rms(x, g):
    ms = jnp.mean(x * x, axis=-1, keepdims=True)
    return x * lax.rsqrt(ms + EPS) * g


def _silu(x):
    return x * jax.nn.sigmoid(x)


def _inproj_kernel(h_ref, g_ref, w_ref, wl_ref, z_ref, zl_ref, u_scr):
    j = pl.program_id(1)

    @pl.when(j == 0)
    def _():
        u = _rms(h_ref[...], g_ref[...]).astype(BF16)
        u_scr[...] = u
        zl_ref[...] = jnp.dot(u, wl_ref[...], preferred_element_type=F32).astype(zl_ref.dtype)

    @pl.when(j > 0)
    def _():
        z_ref[...] = jnp.dot(u_scr[...], w_ref[...], preferred_element_type=F32).astype(z_ref.dtype)


def inproj(h, g, w, wl, *, tm=1024, tn=1536):
    n = h.shape[0]
    nmain = NZ // tn
    est = (2 * tm * D_MODEL * 4 + 2 * D_MODEL * tn * 2 + 2 * tm * tn * 2 + tm * D_MODEL * 2 + tm * tn * 4
           + 2 * D_MODEL * LR_W * 2 + 2 * tm * LR_W * 2)
    main = lambda j: jnp.maximum(j - 1, 0)
    return pl.pallas_call(
        _inproj_kernel,
        grid=(n // tm, nmain + 1),
        in_specs=[
            pl.BlockSpec((tm, D_MODEL), lambda i, j: (i, 0)),
            pl.BlockSpec((1, D_MODEL), lambda i, j: (0, 0)),
            pl.BlockSpec((D_MODEL, tn), lambda i, j: (0, main(j))),
            pl.BlockSpec((D_MODEL, LR_W), lambda i, j: (0, 0)),
        ],
        out_specs=[pl.BlockSpec((tm, tn), lambda i, j: (i, main(j))),
                   pl.BlockSpec((tm, LR_W), lambda i, j: (i, 0))],
        out_shape=[jax.ShapeDtypeStruct((n, NZ), BF16), jax.ShapeDtypeStruct((n, LR_W), BF16)],
        scratch_shapes=[pltpu.VMEM((tm, D_MODEL), BF16)],
        compiler_params=pltpu.CompilerParams(
            dimension_semantics=("parallel", "arbitrary"), vmem_limit_bytes=_vmem(est)),
        name="inproj",
    )(h, g, w, wl)


def _cumsum_rows(x, seg):
    idx = lax.broadcasted_iota(jnp.int32, x.shape, 0) & (seg - 1)
    s = 1
    while s < seg:
        x = x + jnp.where(idx >= s, pltpu.roll(x, s, 0), 0.0)
        s *= 2
    return x


def _log_sigmoid(x):
    return jnp.minimum(x, 0.0) - jnp.log(1.0 + jnp.exp(-jnp.abs(x)))


def _split_bf16(x):
    hi = x.astype(BF16)
    r1 = x - hi.astype(F32)
    mid = r1.astype(BF16)
    lo = (r1 - mid.astype(F32)).astype(BF16)
    return hi, mid, lo


def _gla_kernel(*refs, reverse, final, tb):
    if final:
        q_ref, k_ref, v_ref, lr_ref, wg_ref, bg_ref, of_ref, gg_ref, gn_ref, o_ref, s_scr = refs
    else:
        q_ref, k_ref, v_ref, lr_ref, wg_ref, bg_ref, o_ref, s_scr = refs
    C = GLA_CHUNK
    nchunk = tb // C

    @pl.when(pl.program_id(2) == 0)
    def _():
        s_scr[...] = jnp.zeros_like(s_scr)

    x = jnp.dot(lr_ref[...], wg_ref[...], preferred_element_type=F32) + bg_ref[...]
    la_parts = _split_bf16(_log_sigmoid(x) * GLA_INV_NORMALIZER)

    row = lax.broadcasted_iota(jnp.int32, (C, C), 0)
    col = lax.broadcasted_iota(jnp.int32, (C, C), 1)
    mask = (col >= row) if reverse else (col <= row)
    tri = jnp.where(mask, 1.0, 0.0).astype(BF16)
    scale = GLA_DK ** -0.5

    order = [(nchunk - 1 - ci) if reverse else ci for ci in range(nchunk)]
    sl = [slice(c * C, (c + 1) * C) for c in range(nchunk)]
    qts, atts, incs, dls = {}, {}, {}, {}
    bs = {c: sum(jnp.dot(tri, part[sl[c]], preferred_element_type=F32) for part in la_parts) for c in order}
    for c in order:
        b = bs[c]
        blast = b[0:1, :] if reverse else b[C - 1:C, :]
        q = q_ref[sl[c], :].astype(F32)
        k = k_ref[sl[c], :].astype(F32)
        qts[c] = (q * jnp.exp(b) * scale).astype(BF16)
        kdec = k * jnp.exp(-b)
        dl = jnp.exp(blast)
        kt = kdec.astype(BF16)
        ke_t = (kdec * dl).T.astype(BF16)
        dls[c] = jnp.broadcast_to(dl, (8, GLA_DK)).T[:, 0:1]
        att = lax.dot_general(qts[c], kt, (((1,), (1,)), ((), ())), preferred_element_type=F32)
        atts[c] = jnp.where(mask, att, 0.0).astype(BF16)
        incs[c] = jnp.dot(ke_t, v_ref[sl[c], :], preferred_element_type=F32)
    state = s_scr[...]
    states = {}
    for c in order:
        states[c] = state.astype(BF16)
        state = dls[c] * state + incs[c]
    s_scr[...] = state
    for c in order:
        o = (jnp.dot(atts[c], v_ref[sl[c], :], preferred_element_type=F32)
             + jnp.dot(qts[c], states[c], preferred_element_type=F32))
        if final:
            y = _rms(o + of_ref[sl[c], :], gn_ref[...])
            gg = gg_ref[sl[c], :].astype(F32)
            o_ref[sl[c], :] = (y * _silu(gg)).astype(o_ref.dtype)
        else:
            o_ref[sl[c], :] = o


def gla_direction(z3, zl3, wg, bg, *, reverse, o_other=None, gn=None, tb=2048):
    B, T, _ = z3.shape
    nb = T // tb
    final = o_other is not None
    blk = (lambda j: nb - 1 - j) if reverse else (lambda j: j)
    qw, vw = GLA_DK, GLA_DV
    in_specs = [
        pl.BlockSpec((None, tb, qw), lambda b, h, j: (b, blk(j), Z_GQ // qw + h)),
        pl.BlockSpec((None, tb, qw), lambda b, h, j: (b, blk(j), Z_GK // qw + h)),
        pl.BlockSpec((None, tb, vw), lambda b, h, j: (b, blk(j), Z_GV // vw + h)),
        pl.BlockSpec((None, tb, LR_W), lambda b, h, j: (b, blk(j), 0)),
        pl.BlockSpec((LR_W, qw), lambda b, h, j: (0, h)),
        pl.BlockSpec((1, qw), lambda b, h, j: (0, h)),
    ]
    args = [z3, z3, z3, zl3, wg, bg]
    if final:
        in_specs += [
            pl.BlockSpec((None, tb, vw), lambda b, h, j: (b, blk(j), h)),
            pl.BlockSpec((None, tb, vw), lambda b, h, j: (b, blk(j), Z_GG // vw + h)),
            pl.BlockSpec((None, 1, vw), lambda b, h, j: (h, 0, 0)),
        ]
        args += [o_other, z3, gn]
    out_dtype = BF16 if final else F32
    est = tb * (2 * 2 * (2 * qw + vw + LR_W) + (2 * (4 + 2) * vw if final else 0) + 2 * 4 * vw + 10 * 4 * qw)
    return pl.pallas_call(
        functools.partial(_gla_kernel, reverse=reverse, final=final, tb=tb),
        grid=(B, GLA_H, nb),
        in_specs=in_specs,
        out_specs=pl.BlockSpec((None, tb, vw), lambda b, h, j: (b, blk(j), h)),
        out_shape=jax.ShapeDtypeStruct((B, T, GLA_VW), out_dtype),
        scratch_shapes=[pltpu.VMEM((GLA_DK, GLA_DV), F32)],
        compiler_params=pltpu.CompilerParams(
            dimension_semantics=("parallel", "parallel", "arbitrary"), vmem_limit_bytes=_vmem(est)),
        name="gla_bwd" if reverse else "gla_fwd",
    )(*args)


def _natten_kernel(q_ref, k_ref, v_ref, bias_ref, qg_ref, kg_ref, o_ref, kn_scr, *, rows, rq):
    T = rows * GRID_W
    KB = 2048
    WK = WIN_ROWS * GRID_W
    j = pl.program_id(2)

    @pl.when(j == 0)
    def _():
        def norm_block(i, carry):
            off = pl.multiple_of(i * KB, KB)
            kk = k_ref[pl.ds(off, KB), :].astype(F32)
            kn_scr[pl.ds(off, KB), :] = _rms(kk, kg_ref[...]).astype(BF16)
            return carry

        lax.fori_loop(0, T // KB, norm_block, 0)

    qn_blk = (_rms(q_ref[...].astype(F32), qg_ref[...]) * (NA_HD ** -0.5)).astype(BF16)
    koffs, probs, sums = [], [], []
    for i in range(rq):
        r = j * rq + i
        rs = jnp.clip(r - WIN_ROWS // 2, 0, rows - WIN_ROWS)
        dti = rs - r + (WIN_ROWS - 1)
        koffs.append(pl.multiple_of(rs * GRID_W, GRID_W))
        kw = kn_scr[pl.ds(koffs[i], WK), :]
        s = lax.dot_general(qn_blk[i * GRID_W:(i + 1) * GRID_W], kw, (((1,), (1,)), ((), ())),
                            preferred_element_type=F32)
        s = s + bias_ref[dti]
        p = jnp.exp(s - jnp.max(s, axis=-1, keepdims=True))
        sums.append(jnp.sum(p, axis=-1, keepdims=True))
        probs.append(p.astype(BF16))
    for i in range(rq):
        vw = v_ref[pl.ds(koffs[i], WK), :]
        o = jnp.dot(probs[i], vw, preferred_element_type=F32) / sums[i]
        o_ref[i * GRID_W:(i + 1) * GRID_W, :] = o.astype(o_ref.dtype)


def natten(z3, bias_tbl, qg, kg, *, rq=64):
    B, T, _ = z3.shape
    rows = T // GRID_W
    assert rows >= WIN_ROWS and rows % rq == 0
    tq = rq * GRID_W
    hd = NA_HD
    est = 2 * 2 * T * hd * 2 + T * hd * 2 + 2 * WIN_ROWS * GRID_W * WIN_ROWS * GRID_W * 4 + 4 * tq * hd * 2
    return pl.pallas_call(
        functools.partial(_natten_kernel, rows=rows, rq=rq),
        grid=(B, NA_H, rows // rq),
        in_specs=[
            pl.BlockSpec((None, tq, hd), lambda b, h, j: (b, j, Z_AQ // hd + h)),
            pl.BlockSpec((None, T, hd), lambda b, h, j: (b, 0, Z_AK // hd + h)),
            pl.BlockSpec((None, T, hd), lambda b, h, j: (b, 0, Z_AV // hd + h)),
            pl.BlockSpec((None, WIN_ROWS, GRID_W, WIN_ROWS * GRID_W), lambda b, h, j: (h, 0, 0, 0)),
            pl.BlockSpec((1, hd), lambda b, h, j: (0, 0)),
            pl.BlockSpec((1, hd), lambda b, h, j: (0, 0)),
        ],
        out_specs=pl.BlockSpec((None, tq, hd), lambda b, h, j: (b, j, h)),
        out_shape=jax.ShapeDtypeStruct((B, T, NA_W), BF16),
        scratch_shapes=[pltpu.VMEM((T, hd), BF16)],
        compiler_params=pltpu.CompilerParams(
            dimension_semantics=("parallel", "parallel", "arbitrary"), vmem_limit_bytes=_vmem(est)),
        name="natten",
    )(z3, z3, z3, bias_tbl, qg, kg)


def natten_bias_table(rpb):
    c = jnp.arange(GRID_W, dtype=jnp.int32)
    c_start = jnp.clip(c - WIN_COLS // 2, 0, GRID_W - WIN_COLS)
    valid = (c[None, :] >= c_start[:, None]) & (c[None, :] < c_start[:, None] + WIN_COLS)
    dc = jnp.clip(c[None, :] - c[:, None], -(WIN_COLS - 1), WIN_COLS - 1) + (WIN_COLS - 1)
    bias_c = jnp.where(valid, rpb.astype(F32)[:, :, dc], NEG_INF)
    dr = jnp.arange(WIN_ROWS)[:, None] + jnp.arange(WIN_ROWS)[None, :]
    tbl = bias_c[:, dr]
    return tbl.transpose(0, 1, 3, 2, 4).reshape(NA_H, WIN_ROWS, GRID_W, WIN_ROWS * GRID_W)


def _outproj_kernel(og_ref, on_ref, an_ref, w_ref, h_ref, o_ref):
    na = _rms(on_ref[...].astype(F32), an_ref[...]).astype(BF16)
    acc = jnp.dot(og_ref[...], w_ref[0:GLA_VW, :], preferred_element_type=F32)
    acc += jnp.dot(na, w_ref[GLA_VW:GLA_VW + NA_W, :], preferred_element_type=F32)
    o_ref[...] = h_ref[...] + acc


def outproj(o_gla, o_na, an, w_o, layer, h, *, tm=512):
    n = h.shape[0]
    est = 2 * 2 * tm * 1024 * 2 + 2 * D_MODEL * D_MODEL * 2 + 4 * tm * D_MODEL * 4 + tm * D_MODEL * 4
    return pl.pallas_call(
        _outproj_kernel,
        grid=(n // tm,),
        in_specs=[
            pl.BlockSpec((tm, GLA_VW), lambda i: (i, 0)),
            pl.BlockSpec((tm, NA_W), lambda i: (i, 0)),
            pl.BlockSpec((1, NA_W), lambda i: (0, 0)),
            pl.BlockSpec((None, D_MODEL, D_MODEL), lambda i: (layer, 0, 0)),
            pl.BlockSpec((tm, D_MODEL), lambda i: (i, 0)),
        ],
        out_specs=pl.BlockSpec((tm, D_MODEL), lambda i: (i, 0)),
        out_shape=jax.ShapeDtypeStruct((n, D_MODEL), F32),
        compiler_params=pltpu.CompilerParams(
            dimension_semantics=("parallel",), vmem_limit_bytes=_vmem(est)),
        name="outproj",
    )(o_gla, o_na, an, w_o, h)


FFN_TF = 512


def _ffn_kernel(h_ref, g_ref, wg_ref, wu_ref, wd_ref, o_ref, u_scr):
    @pl.when(pl.program_id(1) == 0)
    def _():
        h = h_ref[...]
        u_scr[...] = _rms(h, g_ref[...]).astype(BF16)
        o_ref[...] = h

    u = u_scr[...]
    a = jnp.dot(u, wg_ref[...], preferred_element_type=F32)
    b = jnp.dot(u, wu_ref[...], preferred_element_type=F32)
    hid = (_silu(a) * b).astype(BF16)
    o_ref[...] += jnp.dot(hid, wd_ref[...], preferred_element_type=F32)


def ffn_dense(h, g, wg, wu, wd, layer, *, tm=1024, tf=FFN_TF):
    n = h.shape[0]
    est = 4 * tm * D_MODEL * 4 + tm * D_MODEL * 2 + 2 * 3 * D_MODEL * tf * 2 + 3 * tm * tf * 4
    return pl.pallas_call(
        _ffn_kernel,
        grid=(n // tm, FF_DIM // tf),
        in_specs=[
            pl.BlockSpec((tm, D_MODEL), lambda i, f: (i, 0)),
            pl.BlockSpec((1, D_MODEL), lambda i, f: (0, 0)),
            pl.BlockSpec((None, D_MODEL, tf), lambda i, f: (layer, 0, f)),
            pl.BlockSpec((None, D_MODEL, tf), lambda i, f: (layer, 0, f)),
            pl.BlockSpec((None, tf, D_MODEL), lambda i, f: (layer, f, 0)),
        ],
        out_specs=pl.BlockSpec((tm, D_MODEL), lambda i, f: (i, 0)),
        out_shape=jax.ShapeDtypeStruct((n, D_MODEL), F32),
        scratch_shapes=[pltpu.VMEM((tm, D_MODEL), BF16)],
        compiler_params=pltpu.CompilerParams(
            dimension_semantics=("parallel", "arbitrary"), vmem_limit_bytes=_vmem(est)),
        name="ffn_dense",
    )(h, g, wg, wu, wd)


REC_E, REC_W, REC_RANK = 0, 2, 4
MOE_TOP_K = 2


def _route_kernel(h_ref, g_ref, wr_ref, rec_ref, cnt_ref, carry_scr):
    tm = h_ref.shape[0]
    lane = lax.broadcasted_iota(jnp.int32, (tm, LANES), 1)

    @pl.when(pl.program_id(0) == 0)
    def _():
        carry_scr[...] = jnp.zeros_like(carry_scr)

    u = _rms(h_ref[...], g_ref[...]).astype(BF16)
    logits = jnp.dot(u, wr_ref[...], preferred_element_type=F32)
    lg = jnp.where(lane < N_EXP, logits, -jnp.inf)
    v1 = jnp.max(lg, axis=-1, keepdims=True)
    i1 = jnp.min(jnp.where(lg == v1, lane, LANES), axis=-1, keepdims=True)
    lg2 = jnp.where(lane == i1, -jnp.inf, lg)
    v2 = jnp.max(lg2, axis=-1, keepdims=True)
    i2 = jnp.min(jnp.where(lg2 == v2, lane, LANES), axis=-1, keepdims=True)
    e2 = jnp.exp(v2 - v1)
    w1 = 1.0 / (1.0 + e2)
    w2 = e2 / (1.0 + e2)
    onehot = jnp.where(lane == i1, 1.0, 0.0) + jnp.where(lane == i2, 1.0, 0.0)
    incl = _cumsum_rows(onehot, tm)
    before = incl - onehot + carry_scr[0:1, :]
    r1 = jnp.sum(jnp.where(lane == i1, before, 0.0), axis=-1, keepdims=True)
    r2 = jnp.sum(jnp.where(lane == i2, before, 0.0), axis=-1, keepdims=True)
    rec = jnp.where(lane == REC_E, i1.astype(F32), 0.0)
    rec = jnp.where(lane == REC_E + 1, i2.astype(F32), rec)
    rec = jnp.where(lane == REC_W, w1, rec)
    rec = jnp.where(lane == REC_W + 1, w2, rec)
    rec = jnp.where(lane == REC_RANK, r1, rec)
    rec = jnp.where(lane == REC_RANK + 1, r2, rec)
    rec_ref[...] = rec
    carry_scr[...] = carry_scr[...] + incl[tm - 1:tm, :]
    cnt_ref[...] = carry_scr[...]


def moe_route(h, g, wr, *, tm=512):
    n = h.shape[0]
    est = 2 * tm * D_MODEL * 4 + D_MODEL * LANES * 4 + 8 * tm * LANES * 4
    return pl.pallas_call(
        _route_kernel,
        grid=(n // tm,),
        in_specs=[
            pl.BlockSpec((tm, D_MODEL), lambda i: (i, 0)),
            pl.BlockSpec((1, D_MODEL), lambda i: (0, 0)),
            pl.BlockSpec((D_MODEL, LANES), lambda i: (0, 0)),
        ],
        out_specs=[pl.BlockSpec((tm, LANES), lambda i: (i, 0)), pl.BlockSpec((8, LANES), lambda i: (0, 0))],
        out_shape=[jax.ShapeDtypeStruct((n, LANES), F32), jax.ShapeDtypeStruct((8, LANES), F32)],
        scratch_shapes=[pltpu.VMEM((8, LANES), F32)],
        compiler_params=pltpu.CompilerParams(
            dimension_semantics=("arbitrary",), vmem_limit_bytes=_vmem(est)),
        name="moe_route",
    )(h, g, wr)


def _static_slot(slot, fn):
    for s in range(2):
        pl.when(slot == s)(functools.partial(fn, s))


ROWS_PER_ISSUE = 8


def _idx_copy(idx_hbm, idx_smem, sem, step, s):
    n = idx_smem.shape[0] // 2
    return pltpu.make_async_copy(idx_hbm.at[pl.ds(pl.multiple_of(step * n, n), n)],
                                 idx_smem.at[pl.ds(s * n, n)], sem.at[s])


def _dispatch_kernel(pad_ref, h_ref, g_ref, idx_hbm, xs_ref, ubuf, zrow, idx_smem, row_sem, idx_sem, pad_sem):
    tm = h_ref.shape[0]
    i = pl.program_id(0)
    nsteps = pl.num_programs(0)
    n_idx = idx_smem.shape[0] // 2

    def row_copy(s, j, rr, dst_row):
        return pltpu.make_async_copy(ubuf.at[s, j, pl.ds(rr, 1)], xs_ref.at[pl.ds(dst_row, 1)], row_sem.at[s])

    def wait_rows(s):
        def body(r, c):
            for _ in range(MOE_TOP_K):
                row_copy(s, 0, 0, 0).wait()
            return c
        lax.fori_loop(0, tm, body, 0, unroll=16)

    def issue_rows(s):
        def body(j, c):
            base = s * n_idx + MOE_TOP_K * ROWS_PER_ISSUE * j
            for rr in range(ROWS_PER_ISSUE):
                for k in range(MOE_TOP_K):
                    row_copy(s, j, rr, idx_smem[base + MOE_TOP_K * rr + k]).start()
            return c
        lax.fori_loop(0, tm // ROWS_PER_ISSUE, body, 0)

    def zero_pad_rows():
        zrow[...] = jnp.zeros_like(zrow)
        for e in range(N_EXP):
            def pad_copy(j, e=e):
                return pltpu.make_async_copy(zrow.at[pl.ds(0, 1)], xs_ref.at[pl.ds(pad_ref[e] + j, 1)], pad_sem.at[0])

            def start(j, c, pad_copy=pad_copy):
                pad_copy(j).start()
                return c

            def wait(j, c, pad_copy=pad_copy):
                pad_copy(j).wait()
                return c
            lax.fori_loop(0, pad_ref[N_EXP + e], start, 0)
            lax.fori_loop(0, pad_ref[N_EXP + e], wait, 0)

    def step(s):
        @pl.when(i == 0)
        def _():
            _idx_copy(idx_hbm, idx_smem, idx_sem, 0, s).start()

        _idx_copy(idx_hbm, idx_smem, idx_sem, i, s).wait()

        @pl.when(i + 1 < nsteps)
        def _():
            _idx_copy(idx_hbm, idx_smem, idx_sem, i + 1, 1 - s).start()

        @pl.when(i >= 2)
        def _():
            wait_rows(s)

        ubuf[s] = _rms(h_ref[...], g_ref[...]).reshape(tm // ROWS_PER_ISSUE, ROWS_PER_ISSUE, D_MODEL)
        issue_rows(s)

        @pl.when(i == nsteps - 1)
        def _():
            zero_pad_rows()
            wait_rows(s)

            @pl.when(nsteps >= 2)
            def _():
                wait_rows(1 - s)

    _static_slot(i % 2, step)


def moe_dispatch(h, g, idx, padinfo, n_sorted, *, tm):
    n = h.shape[0]
    est = 2 * tm * D_MODEL * 4 + 2 * tm * D_MODEL * 4 + 8 * D_MODEL * 4
    return pl.pallas_call(
        _dispatch_kernel,
        grid_spec=pltpu.PrefetchScalarGridSpec(
            num_scalar_prefetch=1,
            grid=(n // tm,),
            in_specs=[
                pl.BlockSpec((tm, D_MODEL), lambda i, pad: (i, 0)),
                pl.BlockSpec((1, D_MODEL), lambda i, pad: (0, 0)),
                pl.BlockSpec(memory_space=pl.ANY),
            ],
            out_specs=pl.BlockSpec(memory_space=pl.ANY),
            scratch_shapes=[
                pltpu.VMEM((2, tm // ROWS_PER_ISSUE, ROWS_PER_ISSUE, D_MODEL), F32),
                pltpu.VMEM((8, D_MODEL), F32),
                pltpu.SMEM((2 * MOE_TOP_K * tm,), jnp.int32),
                pltpu.SemaphoreType.DMA((2,)),
                pltpu.SemaphoreType.DMA((2,)),
                pltpu.SemaphoreType.DMA((1,)),
            ],
        ),
        out_shape=jax.ShapeDtypeStruct((n_sorted, D_MODEL), F32),
        compiler_params=pltpu.CompilerParams(
            dimension_semantics=("arbitrary",), vmem_limit_bytes=_vmem(est)),
        name="moe_dispatch",
    )(padinfo, h, g, idx)


def _experts_kernel(be_ref, nu_ref, x_ref, wg_ref, wu_ref, wd_ref, y_ref):
    del be_ref
    used = pl.program_id(0) < nu_ref[0]

    @pl.when(used)
    def _():
        x = x_ref[...].astype(BF16)
        a = jnp.dot(x, wg_ref[...], preferred_element_type=F32)
        b = jnp.dot(x, wu_ref[...], preferred_element_type=F32)
        hid = (_silu(a) * b).astype(BF16)
        y_ref[...] = jnp.dot(hid, wd_ref[...], preferred_element_type=F32)

    @pl.when(jnp.logical_not(used))
    def _():
        y_ref[...] = jnp.zeros_like(y_ref)


def moe_experts(block_expert, n_used, xs, wg, wu, wd, layer, *, bm):
    n_sorted = xs.shape[0]
    est = 4 * bm * D_MODEL * 4 + 2 * 3 * D_MODEL * EXP_DIM * 2 + 3 * bm * EXP_DIM * 4 + bm * D_MODEL * 2
    row_blk = lambda i, be, nu: (jnp.where(i < nu[0], i, 0), 0)
    return pl.pallas_call(
        _experts_kernel,
        grid_spec=pltpu.PrefetchScalarGridSpec(
            num_scalar_prefetch=2,
            grid=(n_sorted // bm,),
            in_specs=[
                pl.BlockSpec((bm, D_MODEL), row_blk),
                pl.BlockSpec((None, None, D_MODEL, EXP_DIM), lambda i, be, nu: (layer, be[i], 0, 0)),
                pl.BlockSpec((None, None, D_MODEL, EXP_DIM), lambda i, be, nu: (layer, be[i], 0, 0)),
                pl.BlockSpec((None, None, EXP_DIM, D_MODEL), lambda i, be, nu: (layer, be[i], 0, 0)),
            ],
            out_specs=pl.BlockSpec((bm, D_MODEL), lambda i, be, nu: (i, 0)),
        ),
        out_shape=jax.ShapeDtypeStruct((n_sorted, D_MODEL), F32),
        compiler_params=pltpu.CompilerParams(
            dimension_semantics=("arbitrary",), vmem_limit_bytes=_vmem(est)),
        name="moe_experts",
    )(block_expert, n_used, xs, wg, wu, wd)


def _combine_kernel(h_ref, rec_ref, idx_hbm, ys_ref, o_ref, ybuf, idx_smem, row_sem, idx_sem):
    tm = h_ref.shape[0]
    i = pl.program_id(0)
    nsteps = pl.num_programs(0)
    n_idx = idx_smem.shape[0] // 2

    def row_copy(s, k, j, rr, src_row):
        return pltpu.make_async_copy(ys_ref.at[pl.ds(src_row, 1)], ybuf.at[s, k, j, pl.ds(rr, 1)], row_sem.at[s])

    def issue_rows(s):
        def body(j, c):
            base = s * n_idx + MOE_TOP_K * ROWS_PER_ISSUE * j
            for rr in range(ROWS_PER_ISSUE):
                for k in range(MOE_TOP_K):
                    row_copy(s, k, j, rr, idx_smem[base + MOE_TOP_K * rr + k]).start()
            return c
        lax.fori_loop(0, tm // ROWS_PER_ISSUE, body, 0)

    def wait_rows(s):
        def body(r, c):
            for k in range(MOE_TOP_K):
                row_copy(s, k, 0, 0, 0).wait()
            return c
        lax.fori_loop(0, tm, body, 0, unroll=16)

    def step(s):
        @pl.when(i == 0)
        def _():
            _idx_copy(idx_hbm, idx_smem, idx_sem, 0, s).start()
            _idx_copy(idx_hbm, idx_smem, idx_sem, 0, s).wait()
            issue_rows(s)

            @pl.when(nsteps >= 2)
            def _():
                _idx_copy(idx_hbm, idx_smem, idx_sem, 1, 1 - s).start()

        @pl.when(i + 1 < nsteps)
        def _():
            _idx_copy(idx_hbm, idx_smem, idx_sem, i + 1, 1 - s).wait()
            issue_rows(1 - s)

        wait_rows(s)

        @pl.when(i + 2 < nsteps)
        def _():
            _idx_copy(idx_hbm, idx_smem, idx_sem, i + 2, s).start()

        rec = rec_ref[...]
        w1 = rec[:, REC_W:REC_W + 1]
        w2 = rec[:, REC_W + 1:REC_W + 2]
        y1 = ybuf[s, 0].reshape(tm, D_MODEL)
        y2 = ybuf[s, 1].reshape(tm, D_MODEL)
        o_ref[...] = h_ref[...] + w1 * y1 + w2 * y2

    _static_slot(i % 2, step)


def moe_combine(h, rec, idx, ys, *, tm):
    n = h.shape[0]
    est = 4 * tm * D_MODEL * 4 + 2 * tm * LANES * 4 + 2 * MOE_TOP_K * tm * D_MODEL * 4
    return pl.pallas_call(
        _combine_kernel,
        grid=(n // tm,),
        in_specs=[
            pl.BlockSpec((tm, D_MODEL), lambda i: (i, 0)),
            pl.BlockSpec((tm, LANES), lambda i: (i, 0)),
            pl.BlockSpec(memory_space=pl.ANY),
            pl.BlockSpec(memory_space=pl.ANY),
        ],
        out_specs=pl.BlockSpec((tm, D_MODEL), lambda i: (i, 0)),
        out_shape=jax.ShapeDtypeStruct((n, D_MODEL), F32),
        scratch_shapes=[
            pltpu.VMEM((2, MOE_TOP_K, tm // ROWS_PER_ISSUE, ROWS_PER_ISSUE, D_MODEL), F32),
            pltpu.SMEM((2 * MOE_TOP_K * tm,), jnp.int32),
            pltpu.SemaphoreType.DMA((2,)),
            pltpu.SemaphoreType.DMA((2,)),
        ],
        compiler_params=pltpu.CompilerParams(
            dimension_semantics=("arbitrary",), vmem_limit_bytes=_vmem(est)),
        name="moe_combine",
    )(h, rec, idx, ys)


def ffn_moe(h, g, wr, wg, wu, wd, layer, *, bm=512, tm=512):
    n = h.shape[0]
    n_sorted = MOE_TOP_K * n + N_EXP * bm
    rec, cnt = moe_route(h, g, wr)
    expert = rec[:, REC_E:REC_E + MOE_TOP_K].astype(jnp.int32)
    rank = rec[:, REC_RANK:REC_RANK + MOE_TOP_K].astype(jnp.int32)
    counts = cnt[0, :N_EXP].astype(jnp.int32)
    padded = (counts + bm - 1) // bm * bm
    pend = jnp.cumsum(padded)
    pstart = pend - padded
    dest = jnp.sum(jnp.where(expert[..., None] == jnp.arange(N_EXP), pstart, 0), axis=-1) + rank
    idx = dest.reshape(MOE_TOP_K * n)
    padinfo = jnp.concatenate([pstart + counts, padded - counts]).astype(jnp.int32)
    n_used = (pend[-1:] // bm).astype(jnp.int32)
    blk_start = jnp.arange(n_sorted // bm, dtype=jnp.int32) * bm
    block_expert = jnp.minimum(jnp.sum(blk_start[:, None] >= pend[None, :], axis=1), N_EXP - 1).astype(jnp.int32)
    xs = moe_dispatch(h, g, idx, padinfo, n_sorted, tm=tm)
    ys = moe_experts(block_expert, n_used, xs, wg, wu, wd, layer, bm=bm)
    return moe_combine(h, rec, idx, ys, tm=tm)


PLE_ROW_CHUNKS = 2


def _ple_kernel(h_ref, p_ref, gi_ref, wgate_ref, wp_ref, go_ref, o_ref):
    rc = h_ref.shape[0] // PLE_ROW_CHUNKS
    rows = [slice(c * rc, (c + 1) * rc) for c in range(PLE_ROW_CHUNKS)]
    hn = [_rms(h_ref[r, :], gi_ref[...]).astype(BF16) for r in rows]
    for c, r in enumerate(rows):
        gate = jax.nn.sigmoid(jnp.dot(hn[c], wgate_ref[...], preferred_element_type=F32))
        e = jnp.dot(p_ref[r, :].astype(BF16), wp_ref[...], preferred_element_type=F32)
        o_ref[r, :] = h_ref[r, :] + gate * _rms(e, go_ref[...])


def ple(h, p_all, layer, gi, wgate, wp, go, *, tm=512):
    n = h.shape[0]
    est = 4 * tm * D_MODEL * 4 + 2 * D_MODEL * D_MODEL * 2 + 2 * PLE_DIM * D_MODEL * 2 + 3 * tm * D_MODEL * 4
    return pl.pallas_call(
        _ple_kernel,
        grid=(n // tm,),
        in_specs=[
            pl.BlockSpec((tm, D_MODEL), lambda i: (i, 0)),
            pl.BlockSpec((None, tm, PLE_DIM), lambda i: (layer, i, 0)),
            pl.BlockSpec((1, D_MODEL), lambda i: (0, 0)),
            pl.BlockSpec((None, D_MODEL, D_MODEL), lambda i: (layer, 0, 0)),
            pl.BlockSpec((None, PLE_DIM, D_MODEL), lambda i: (layer, 0, 0)),
            pl.BlockSpec((1, D_MODEL), lambda i: (0, 0)),
        ],
        out_specs=pl.BlockSpec((tm, D_MODEL), lambda i: (i, 0)),
        out_shape=jax.ShapeDtypeStruct((n, D_MODEL), F32),
        compiler_params=pltpu.CompilerParams(
            dimension_semantics=("parallel",), vmem_limit_bytes=_vmem(est)),
        name="ple",
    )(h, p_all, gi, wgate, wp, go)


def _prep_layer(i, P):
    lr0 = Z_AQ
    n_in = P["w_in"].shape[-1]

    def w_in_cols(a, b):
        return lax.slice(P["w_in"], (i, 0, a), (i + 1, D_MODEL, b)).reshape(D_MODEL, b - a).astype(BF16)

    w_in_r = jnp.concatenate([w_in_cols(0, lr0), w_in_cols(lr0 + 2 * GLA_RANK, n_in)], axis=1)
    w_lr = jnp.concatenate([w_in_cols(lr0, lr0 + 2 * GLA_RANK),
                            jnp.zeros((D_MODEL, LR_W - 2 * GLA_RANK), BF16)], axis=1)
    zpad = jnp.zeros((LR_W, GLA_KW), F32)
    wgf = zpad.at[0:GLA_RANK].set(P["w_gate_f"][i]).astype(BF16)
    wgb = zpad.at[GLA_RANK:2 * GLA_RANK].set(P["w_gate_b"][i]).astype(BF16)
    L = dict(
        g_mix=P["g_mix"][i][None], w_in=w_in_r, w_lr=w_lr, wgf=wgf, wgb=wgb,
        bgf=P["b_gate_f"][i][None], bgb=P["b_gate_b"][i][None],
        gla_norm=P["gla_norm"][i].reshape(GLA_H, 1, GLA_DV),
        q_norm=P["q_norm"][i][None], k_norm=P["k_norm"][i][None],
        bias_tbl=natten_bias_table(P["rpb"][i]), attn_norm=P["attn_norm"][i][None],
        g_ffn=P["g_ffn"][i][None], g_ple_in=P["g_ple_in"][i][None], g_ple_out=P["g_ple_out"][i][None],
    )
    if i % 2 == 1:
        L["wr"] = jnp.zeros((D_MODEL, LANES), F32).at[:, :N_EXP].set(P["w_router"][i // 2]).astype(BF16)
    return L


def _prep_stacked(P):
    names = ("w_o", "w_ple_gate", "w_ple", "w_ff_gate", "w_ff_up", "w_ff_down",
             "w_exp_gate", "w_exp_up", "w_exp_down")
    return {k: P[k].astype(BF16) for k in names}


def _trunk(x, p, layers, W):
    B, T, _ = x.shape
    n = B * T
    h = x.reshape(n, D_MODEL)
    p_all = p.reshape(DEPTH, n, PLE_DIM)
    for i, L in enumerate(layers):
        z, zl = inproj(h, L["g_mix"], L["w_in"], L["w_lr"])
        z3 = z.reshape(B, T, NZ)
        zl3 = zl.reshape(B, T, LR_W)
        o_f = gla_direction(z3, zl3, L["wgf"], L["bgf"], reverse=False)
        o_gla = gla_direction(z3, zl3, L["wgb"], L["bgb"], reverse=True, o_other=o_f, gn=L["gla_norm"])
        o_na = natten(z3, L["bias_tbl"], L["q_norm"], L["k_norm"])
        h = outproj(o_gla.reshape(n, GLA_VW), o_na.reshape(n, NA_W), L["attn_norm"], W["w_o"], i, h)
        if i % 2 == 0:
            h = ffn_dense(h, L["g_ffn"], W["w_ff_gate"], W["w_ff_up"], W["w_ff_down"], i // 2)
        else:
            h = ffn_moe(h, L["g_ffn"], L["wr"], W["w_exp_gate"], W["w_exp_up"], W["w_exp_down"], i // 2)
        h = ple(h, p_all, i, L["g_ple_in"], W["w_ple_gate"], W["w_ple"], L["g_ple_out"])
    return h.reshape(B, T, D_MODEL)


def kernel(x_prompt, x_sample, p_prompt, p_sample, g_mix, w_in, w_gate_f, b_gate_f, w_gate_b, b_gate_b, gla_norm, q_norm, k_norm, rpb, attn_norm, w_o, g_ffn, w_ff_gate, w_ff_up, w_ff_down, w_router, w_exp_gate, w_exp_up, w_exp_down, g_ple_in, w_ple_gate, w_ple, g_ple_out):
    P = dict(g_mix=g_mix, w_in=w_in, w_gate_f=w_gate_f, b_gate_f=b_gate_f, w_gate_b=w_gate_b,
             b_gate_b=b_gate_b, gla_norm=gla_norm, q_norm=q_norm, k_norm=k_norm, rpb=rpb,
             attn_norm=attn_norm, w_o=w_o, g_ffn=g_ffn, w_ff_gate=w_ff_gate, w_ff_up=w_ff_up,
             w_ff_down=w_ff_down, w_router=w_router, w_exp_gate=w_exp_gate, w_exp_up=w_exp_up,
             w_exp_down=w_exp_down, g_ple_in=g_ple_in, w_ple_gate=w_ple_gate, w_ple=w_ple,
             g_ple_out=g_ple_out)
    layers = [_prep_layer(i, P) for i in range(DEPTH)]
    W = _prep_stacked(P)
    return (_trunk(x_prompt, p_prompt, layers, W), _trunk(x_sample, p_sample, layers, W))
```

```python
import functools

import jax
import jax.numpy as jnp
from jax import lax
from jax.experimental import pallas as pl
from jax.experimental.pallas import tpu as pltpu

F32 = jnp.float32
BF16 = jnp.bfloat16

D_MODEL = 2048
DEPTH = 4
GRID_W = 64
GLA_H = 4
GLA_VW = 1024
GLA_KW = 512
GLA_DK = 128
GLA_DV = 256
GLA_RANK = 16
GLA_INV_NORMALIZER = 1.0 / 16.0
GLA_CHUNK = 64
NA_W = 1024
NA_H = 8
NA_HD = 128
WIN_ROWS = 8
WIN_COLS = 16
FF_DIM = 4096
N_EXP = 8
EXP_DIM = 1024
PLE_DIM = 256
EPS = 1e-6
NEG_INF = -1e30

Z_GQ, Z_GK, Z_GV, Z_GG = 0, 512, 1024, 2048
Z_AQ, Z_AK, Z_AV, NZ = 3072, 4096, 5120, 6144
LR_W = 256

V7X_VMEM_BYTES = 64 * 1024 * 1024
LANES = 128


def _vmem(nbytes):
    return int(min(nbytes * 3 // 2 + (4 << 20), V7X_VMEM_BYTES - (6 << 20)))


def _rms(x, g):
    ms = jnp.mean(x * x, axis=-1, keepdims=True)
    return x * lax.rsqrt(ms + EPS) * g


def _silu(x):
    return x * jax.nn.sigmoid(x)


def _inproj_kernel(h_ref, g_ref, w_ref, wl_ref, z_ref, zl_ref, u_scr):
    j = pl.program_id(1)

    @pl.when(j == 0)
    def _():
        u = _rms(h_ref[...], g_ref[...]).astype(BF16)
        u_scr[...] = u
        zl_ref[...] = jnp.dot(u, wl_ref[...], preferred_element_type=F32).astype(zl_ref.dtype)

    @pl.when(j > 0)
    def _():
        z_ref[...] = jnp.dot(u_scr[...], w_ref[...], preferred_element_type=F32).astype(z_ref.dtype)


def inproj(h, g, w, wl, *, tm=1024, tn=1536):
    n = h.shape[0]
    nmain = NZ // tn
    est = (2 * tm * D_MODEL * 4 + 2 * D_MODEL * tn * 2 + 2 * tm * tn * 2 + tm * D_MODEL * 2 + tm * tn * 4
           + 2 * D_MODEL * LR_W * 2 + 2 * tm * LR_W * 2)
    main = lambda j: jnp.maximum(j - 1, 0)
    return pl.pallas_call(
        _inproj_kernel,
        grid=(n // tm, nmain + 1),
        in_specs=[
            pl.BlockSpec((tm, D_MODEL), lambda i, j: (i, 0)),
            pl.BlockSpec((1, D_MODEL), lambda i, j: (0, 0)),
            pl.BlockSpec((D_MODEL, tn), lambda i, j: (0, main(j))),
            pl.BlockSpec((D_MODEL, LR_W), lambda i, j: (0, 0)),
        ],
        out_specs=[pl.BlockSpec((tm, tn), lambda i, j: (i, main(j))),
                   pl.BlockSpec((tm, LR_W), lambda i, j: (i, 0))],
        out_shape=[jax.ShapeDtypeStruct((n, NZ), BF16), jax.ShapeDtypeStruct((n, LR_W), BF16)],
        scratch_shapes=[pltpu.VMEM((tm, D_MODEL), BF16)],
        compiler_params=pltpu.CompilerParams(
            dimension_semantics=("parallel", "arbitrary"), vmem_limit_bytes=_vmem(est)),
        name="inproj",
    )(h, g, w, wl)


def _cumsum_rows(x, seg):
    idx = lax.broadcasted_iota(jnp.int32, x.shape, 0) & (seg - 1)
    s = 1
    while s < seg:
        x = x + jnp.where(idx >= s, pltpu.roll(x, s, 0), 0.0)
        s *= 2
    return x


def _log_sigmoid(x):
    return jnp.minimum(x, 0.0) - jnp.log(1.0 + jnp.exp(-jnp.abs(x)))


def _split_bf16(x):
    hi = x.astype(BF16)
    r1 = x - hi.astype(F32)
    mid = r1.astype(BF16)
    lo = (r1 - mid.astype(F32)).astype(BF16)
    return hi, mid, lo


def _gla_kernel(*refs, reverse, final, tb):
    if final:
        q_ref, k_ref, v_ref, lr_ref, wg_ref, bg_ref, of_ref, gg_ref, gn_ref, o_ref, s_scr = refs
    else:
        q_ref, k_ref, v_ref, lr_ref, wg_ref, bg_ref, o_ref, s_scr = refs
    C = GLA_CHUNK
    nchunk = tb // C

    @pl.when(pl.program_id(2) == 0)
    def _():
        s_scr[...] = jnp.zeros_like(s_scr)

    x = jnp.dot(lr_ref[...], wg_ref[...], preferred_element_type=F32) + bg_ref[...]
    la_parts = _split_bf16(_log_sigmoid(x) * GLA_INV_NORMALIZER)

    row = lax.broadcasted_iota(jnp.int32, (C, C), 0)
    col = lax.broadcasted_iota(jnp.int32, (C, C), 1)
    mask = (col >= row) if reverse else (col <= row)
    tri = jnp.where(mask, 1.0, 0.0).astype(BF16)
    scale = GLA_DK ** -0.5

    order = [(nchunk - 1 - ci) if reverse else ci for ci in range(nchunk)]
    sl = [slice(c * C, (c + 1) * C) for c in range(nchunk)]
    qts, atts, incs, dls = {}, {}, {}, {}
    bs = {c: sum(jnp.dot(tri, part[sl[c]], preferred_element_type=F32) for part in la_parts) for c in order}
    for c in order:
        b = bs[c]
        blast = b[0:1, :] if reverse else b[C - 1:C, :]
        q = q_ref[sl[c], :].astype(F32)
        k = k_ref[sl[c], :].astype(F32)
        qts[c] = (q * jnp.exp(b) * scale).astype(BF16)
        kdec = k * jnp.exp(-b)
        dl = jnp.exp(blast)
        kt = kdec.astype(BF16)
        ke_t = (kdec * dl).T.astype(BF16)
        dls[c] = jnp.broadcast_to(dl, (8, GLA_DK)).T[:, 0:1]
        att = lax.dot_general(qts[c], kt, (((1,), (1,)), ((), ())), preferred_element_type=F32)
        atts[c] = jnp.where(mask, att, 0.0).astype(BF16)
        incs[c] = jnp.dot(ke_t, v_ref[sl[c], :], preferred_element_type=F32)
    state = s_scr[...]
    states = {}
    for c in order:
        states[c] = state.astype(BF16)
        state = dls[c] * state + incs[c]
    s_scr[...] = state
    for c in order:
        o = (jnp.dot(atts[c], v_ref[sl[c], :], preferred_element_type=F32)
             + jnp.dot(qts[c], states[c], preferred_element_type=F32))
        if final:
            y = _rms(o + of_ref[sl[c], :], gn_ref[...])
            gg = gg_ref[sl[c], :].astype(F32)
            o_ref[sl[c], :] = (y * _silu(gg)).astype(o_ref.dtype)
        else:
            o_ref[sl[c], :] = o


def gla_direction(z3, zl3, wg, bg, *, reverse, o_other=None, gn=None, tb=2048):
    B, T, _ = z3.shape
    nb = T // tb
    final = o_other is not None
    blk = (lambda j: nb - 1 - j) if reverse else (lambda j: j)
    qw, vw = GLA_DK, GLA_DV
    in_specs = [
        pl.BlockSpec((None, tb, qw), lambda b, h, j: (b, blk(j), Z_GQ // qw + h)),
        pl.BlockSpec((None, tb, qw), lambda b, h, j: (b, blk(j), Z_GK // qw + h)),
        pl.BlockSpec((None, tb, vw), lambda b, h, j: (b, blk(j), Z_GV // vw + h)),
        pl.BlockSpec((None, tb, LR_W), lambda b, h, j: (b, blk(j), 0)),
        pl.BlockSpec((LR_W, qw), lambda b, h, j: (0, h)),
        pl.BlockSpec((1, qw), lambda b, h, j: (0, h)),
    ]
    args = [z3, z3, z3, zl3, wg, bg]
    if final:
        in_specs += [
            pl.BlockSpec((None, tb, vw), lambda b, h, j: (b, blk(j), h)),
            pl.BlockSpec((None, tb, vw), lambda b, h, j: (b, blk(j), Z_GG // vw + h)),
            pl.BlockSpec((None, 1, vw), lambda b, h, j: (h, 0, 0)),
        ]
        args += [o_other, z3, gn]
    out_dtype = BF16 if final else F32
    est = tb * (2 * 2 * (2 * qw + vw + LR_W) + (2 * (4 + 2) * vw if final else 0) + 2 * 4 * vw + 10 * 4 * qw)
    return pl.pallas_call(
        functools.partial(_gla_kernel, reverse=reverse, final=final, tb=tb),
        grid=(B, GLA_H, nb),
        in_specs=in_specs,
        out_specs=pl.BlockSpec((None, tb, vw), lambda b, h, j: (b, blk(j), h)),
        out_shape=jax.ShapeDtypeStruct((B, T, GLA_VW), out_dtype),
        scratch_shapes=[pltpu.VMEM((GLA_DK, GLA_DV), F32)],
        compiler_params=pltpu.CompilerParams(
            dimension_semantics=("parallel", "parallel", "arbitrary"), vmem_limit_bytes=_vmem(est)),
        name="gla_bwd" if reverse else "gla_fwd",
    )(*args)


def _natten_kernel(q_ref, k_ref, v_ref, bias_ref, qg_ref, kg_ref, o_ref, kn_scr, *, rows, rq):
    T = rows * GRID_W
    KB = 2048
    WK = WIN_ROWS * GRID_W
    j = pl.program_id(2)

    @pl.when(j == 0)
    def _():
        def norm_block(i, carry):
            off = pl.multiple_of(i * KB, KB)
            kk = k_ref[pl.ds(off, KB), :].astype(F32)
            kn_scr[pl.ds(off, KB), :] = _rms(kk, kg_ref[...]).astype(BF16)
            return carry

        lax.fori_loop(0, T // KB, norm_block, 0)

    qn_blk = (_rms(q_ref[...].astype(F32), qg_ref[...]) * (NA_HD ** -0.5)).astype(BF16)
    koffs, probs, sums = [], [], []
    for i in range(rq):
        r = j * rq + i
        rs = jnp.clip(r - WIN_ROWS // 2, 0, rows - WIN_ROWS)
        dti = rs - r + (WIN_ROWS - 1)
        koffs.append(pl.multiple_of(rs * GRID_W, GRID_W))
        kw = kn_scr[pl.ds(koffs[i], WK), :]
        s = lax.dot_general(qn_blk[i * GRID_W:(i + 1) * GRID_W], kw, (((1,), (1,)), ((), ())),
                            preferred_element_type=F32)
        s = s + bias_ref[dti]
        p = jnp.exp(s - jnp.max(s, axis=-1, keepdims=True))
        sums.append(jnp.sum(p, axis=-1, keepdims=True))
        probs.append(p.astype(BF16))
    for i in range(rq):
        vw = v_ref[pl.ds(koffs[i], WK), :]
        o = jnp.dot(probs[i], vw, preferred_element_type=F32) / sums[i]
        o_ref[i * GRID_W:(i + 1) * GRID_W, :] = o.astype(o_ref.dtype)


def natten(z3, bias_tbl, qg, kg, *, rq=64):
    B, T, _ = z3.shape
    rows = T // GRID_W
    assert rows >= WIN_ROWS and rows % rq == 0
    tq = rq * GRID_W
    hd = NA_HD
    est = 2 * 2 * T * hd * 2 + T * hd * 2 + 2 * WIN_ROWS * GRID_W * WIN_ROWS * GRID_W * 4 + 4 * tq * hd * 2
    return pl.pallas_call(
        functools.partial(_natten_kernel, rows=rows, rq=rq),
        grid=(B, NA_H, rows // rq),
        in_specs=[
            pl.BlockSpec((None, tq, hd), lambda b, h, j: (b, j, Z_AQ // hd + h)),
            pl.BlockSpec((None, T, hd), lambda b, h, j: (b, 0, Z_AK // hd + h)),
            pl.BlockSpec((None, T, hd), lambda b, h, j: (b, 0, Z_AV // hd + h)),
            pl.BlockSpec((None, WIN_ROWS, GRID_W, WIN_ROWS * GRID_W), lambda b, h, j: (h, 0, 0, 0)),
            pl.BlockSpec((1, hd), lambda b, h, j: (0, 0)),
            pl.BlockSpec((1, hd), lambda b, h, j: (0, 0)),
        ],
        out_specs=pl.BlockSpec((None, tq, hd), lambda b, h, j: (b, j, h)),
        out_shape=jax.ShapeDtypeStruct((B, T, NA_W), BF16),
        scratch_shapes=[pltpu.VMEM((T, hd), BF16)],
        compiler_params=pltpu.CompilerParams(
            dimension_semantics=("parallel", "parallel", "arbitrary"), vmem_limit_bytes=_vmem(est)),
        name="natten",
    )(z3, z3, z3, bias_tbl, qg, kg)


def natten_bias_table(rpb):
    c = jnp.arange(GRID_W, dtype=jnp.int32)
    c_start = jnp.clip(c - WIN_COLS // 2, 0, GRID_W - WIN_COLS)
    valid = (c[None, :] >= c_start[:, None]) & (c[None, :] < c_start[:, None] + WIN_COLS)
    dc = jnp.clip(c[None, :] - c[:, None], -(WIN_COLS - 1), WIN_COLS - 1) + (WIN_COLS - 1)
    bias_c = jnp.where(valid, rpb.astype(F32)[:, :, dc], NEG_INF)
    dr = jnp.arange(WIN_ROWS)[:, None] + jnp.arange(WIN_ROWS)[None, :]
    tbl = bias_c[:, dr]
    return tbl.transpose(0, 1, 3, 2, 4).reshape(NA_H, WIN_ROWS, GRID_W, WIN_ROWS * GRID_W)


def _outproj_kernel(og_ref, on_ref, an_ref, w_ref, h_ref, o_ref):
    na = _rms(on_ref[...].astype(F32), an_ref[...]).astype(BF16)
    acc = jnp.dot(og_ref[...], w_ref[0:GLA_VW, :], preferred_element_type=F32)
    acc += jnp.dot(na, w_ref[GLA_VW:GLA_VW + NA_W, :], preferred_element_type=F32)
    o_ref[...] = h_ref[...] + acc


def outproj(o_gla, o_na, an, w_o, layer, h, *, tm=512):
    n = h.shape[0]
    est = 2 * 2 * tm * 1024 * 2 + 2 * D_MODEL * D_MODEL * 2 + 4 * tm * D_MODEL * 4 + tm * D_MODEL * 4
    return pl.pallas_call(
        _outproj_kernel,
        grid=(n // tm,),
        in_specs=[
            pl.BlockSpec((tm, GLA_VW), lambda i: (i, 0)),
            pl.BlockSpec((tm, NA_W), lambda i: (i, 0)),
            pl.BlockSpec((1, NA_W), lambda i: (0, 0)),
            pl.BlockSpec((None, D_MODEL, D_MODEL), lambda i: (layer, 0, 0)),
            pl.BlockSpec((tm, D_MODEL), lambda i: (i, 0)),
        ],
        out_specs=pl.BlockSpec((tm, D_MODEL), lambda i: (i, 0)),
        out_shape=jax.ShapeDtypeStruct((n, D_MODEL), F32),
        compiler_params=pltpu.CompilerParams(
            dimension_semantics=("parallel",), vmem_limit_bytes=_vmem(est)),
        name="outproj",
    )(o_gla, o_na, an, w_o, h)


FFN_TF = 512


def _ffn_kernel(h_ref, g_ref, wg_ref, wu_ref, wd_ref, o_ref, u_scr):
    @pl.when(pl.program_id(1) == 0)
    def _():
        h = h_ref[...]
        u_scr[...] = _rms(h, g_ref[...]).astype(BF16)
        o_ref[...] = h

    u = u_scr[...]
    a = jnp.dot(u, wg_ref[...], preferred_element_type=F32)
    b = jnp.dot(u, wu_ref[...], preferred_element_type=F32)
    hid = (_silu(a) * b).astype(BF16)
    o_ref[...] += jnp.dot(hid, wd_ref[...], preferred_element_type=F32)


def ffn_dense(h, g, wg, wu, wd, layer, *, tm=1024, tf=FFN_TF):
    n = h.shape[0]
    est = 4 * tm * D_MODEL * 4 + tm * D_MODEL * 2 + 2 * 3 * D_MODEL * tf * 2 + 3 * tm * tf * 4
    return pl.pallas_call(
        _ffn_kernel,
        grid=(n // tm, FF_DIM // tf),
        in_specs=[
            pl.BlockSpec((tm, D_MODEL), lambda i, f: (i, 0)),
            pl.BlockSpec((1, D_MODEL), lambda i, f: (0, 0)),
            pl.BlockSpec((None, D_MODEL, tf), lambda i, f: (layer, 0, f)),
            pl.BlockSpec((None, D_MODEL, tf), lambda i, f: (layer, 0, f)),
            pl.BlockSpec((None, tf, D_MODEL), lambda i, f: (layer, f, 0)),
        ],
        out_specs=pl.BlockSpec((tm, D_MODEL), lambda i, f: (i, 0)),
        out_shape=jax.ShapeDtypeStruct((n, D_MODEL), F32),
        scratch_shapes=[pltpu.VMEM((tm, D_MODEL), BF16)],
        compiler_params=pltpu.CompilerParams(
            dimension_semantics=("parallel", "arbitrary"), vmem_limit_bytes=_vmem(est)),
        name="ffn_dense",
    )(h, g, wg, wu, wd)


REC_E, REC_W, REC_RANK = 0, 2, 4
MOE_TOP_K = 2


def _route_kernel(h_ref, g_ref, wr_ref, rec_ref, cnt_ref, carry_scr):
    tm = h_ref.shape[0]
    lane = lax.broadcasted_iota(jnp.int32, (tm, LANES), 1)

    @pl.when(pl.program_id(0) == 0)
    def _():
        carry_scr[...] = jnp.zeros_like(carry_scr)

    u = _rms(h_ref[...], g_ref[...]).astype(BF16)
    logits = jnp.dot(u, wr_ref[...], preferred_element_type=F32)
    lg = jnp.where(lane < N_EXP, logits, -jnp.inf)
    v1 = jnp.max(lg, axis=-1, keepdims=True)
    i1 = jnp.min(jnp.where(lg == v1, lane, LANES), axis=-1, keepdims=True)
    lg2 = jnp.where(lane == i1, -jnp.inf, lg)
    v2 = jnp.max(lg2, axis=-1, keepdims=True)
    i2 = jnp.min(jnp.where(lg2 == v2, lane, LANES), axis=-1, keepdims=True)
    e2 = jnp.exp(v2 - v1)
    w1 = 1.0 / (1.0 + e2)
    w2 = e2 / (1.0 + e2)
    onehot = jnp.where(lane == i1, 1.0, 0.0) + jnp.where(lane == i2, 1.0, 0.0)
    incl = _cumsum_rows(onehot, tm)
    before = incl - onehot + carry_scr[0:1, :]
    r1 = jnp.sum(jnp.where(lane == i1, before, 0.0), axis=-1, keepdims=True)
    r2 = jnp.sum(jnp.where(lane == i2, before, 0.0), axis=-1, keepdims=True)
    rec = jnp.where(lane == REC_E, i1.astype(F32), 0.0)
    rec = jnp.where(lane == REC_E + 1, i2.astype(F32), rec)
    rec = jnp.where(lane == REC_W, w1, rec)
    rec = jnp.where(lane == REC_W + 1, w2, rec)
    rec = jnp.where(lane == REC_RANK, r1, rec)
    rec = jnp.where(lane == REC_RANK + 1, r2, rec)
    rec_ref[...] = rec
    carry_scr[...] = carry_scr[...] + incl[tm - 1:tm, :]
    cnt_ref[...] = carry_scr[...]


def moe_route(h, g, wr, *, tm=512):
    n = h.shape[0]
    est = 2 * tm * D_MODEL * 4 + D_MODEL * LANES * 4 + 8 * tm * LANES * 4
    return pl.pallas_call(
        _route_kernel,
        grid=(n // tm,),
        in_specs=[
            pl.BlockSpec((tm, D_MODEL), lambda i: (i, 0)),
            pl.BlockSpec((1, D_MODEL), lambda i: (0, 0)),
            pl.BlockSpec((D_MODEL, LANES), lambda i: (0, 0)),
        ],
        out_specs=[pl.BlockSpec((tm, LANES), lambda i: (i, 0)), pl.BlockSpec((8, LANES), lambda i: (0, 0))],
        out_shape=[jax.ShapeDtypeStruct((n, LANES), F32), jax.ShapeDtypeStruct((8, LANES), F32)],
        scratch_shapes=[pltpu.VMEM((8, LANES), F32)],
        compiler_params=pltpu.CompilerParams(
            dimension_semantics=("arbitrary",), vmem_limit_bytes=_vmem(est)),
        name="moe_route",
    )(h, g, wr)


def _static_slot(slot, fn):
    for s in range(2):
        pl.when(slot == s)(functools.partial(fn, s))


ROWS_PER_ISSUE = 8


def _idx_copy(idx_hbm, idx_smem, sem, step, s):
    n = idx_smem.shape[0] // 2
    return pltpu.make_async_copy(idx_hbm.at[pl.ds(pl.multiple_of(step * n, n), n)],
                                 idx_smem.at[pl.ds(s * n, n)], sem.at[s])


def _dispatch_kernel(pad_ref, h_ref, g_ref, idx_hbm, xs_ref, ubuf, zrow, idx_smem, row_sem, idx_sem, pad_sem):
    tm = h_ref.shape[0]
    i = pl.program_id(0)
    nsteps = pl.num_programs(0)
    n_idx = idx_smem.shape[0] // 2

    def row_copy(s, j, rr, dst_row):
        return pltpu.make_async_copy(ubuf.at[s, j, pl.ds(rr, 1)], xs_ref.at[pl.ds(dst_row, 1)], row_sem.at[s])

    def wait_rows(s):
        def body(r, c):
            for _ in range(MOE_TOP_K):
                row_copy(s, 0, 0, 0).wait()
            return c
        lax.fori_loop(0, tm, body, 0, unroll=16)

    def issue_rows(s):
        def body(j, c):
            base = s * n_idx + MOE_TOP_K * ROWS_PER_ISSUE * j
            for rr in range(ROWS_PER_ISSUE):
                for k in range(MOE_TOP_K):
                    row_copy(s, j, rr, idx_smem[base + MOE_TOP_K * rr + k]).start()
            return c
        lax.fori_loop(0, tm // ROWS_PER_ISSUE, body, 0)

    def zero_pad_rows():
        zrow[...] = jnp.zeros_like(zrow)
        for e in range(N_EXP):
            def pad_copy(j, e=e):
                return pltpu.make_async_copy(zrow.at[pl.ds(0, 1)], xs_ref.at[pl.ds(pad_ref[e] + j, 1)], pad_sem.at[0])

            def start(j, c, pad_copy=pad_copy):
                pad_copy(j).start()
                return c

            def wait(j, c, pad_copy=pad_copy):
                pad_copy(j).wait()
                return c
            lax.fori_loop(0, pad_ref[N_EXP + e], start, 0)
            lax.fori_loop(0, pad_ref[N_EXP + e], wait, 0)

        bm = zrow.shape[0]

        def tail_copy(b):
            return pltpu.make_async_copy(zrow, xs_ref.at[pl.ds(pl.multiple_of(b * bm, bm), bm)], pad_sem.at[0])

        def tail_start(b, c):
            tail_copy(b).start()
            return c

        def tail_wait(b, c):
            tail_copy(b).wait()
            return c
        lax.fori_loop(pad_ref[2 * N_EXP], xs_ref.shape[0] // bm, tail_start, 0)
        lax.fori_loop(pad_ref[2 * N_EXP], xs_ref.shape[0] // bm, tail_wait, 0)

    def step(s):
        @pl.when(i == 0)
        def _():
            _idx_copy(idx_hbm, idx_smem, idx_sem, 0, s).start()

        _idx_copy(idx_hbm, idx_smem, idx_sem, i, s).wait()

        @pl.when(i + 1 < nsteps)
        def _():
            _idx_copy(idx_hbm, idx_smem, idx_sem, i + 1, 1 - s).start()

        @pl.when(i >= 2)
        def _():
            wait_rows(s)

        ubuf[s] = _rms(h_ref[...], g_ref[...]).reshape(tm // ROWS_PER_ISSUE, ROWS_PER_ISSUE, D_MODEL)
        issue_rows(s)

        @pl.when(i == nsteps - 1)
        def _():
            zero_pad_rows()
            wait_rows(s)

            @pl.when(nsteps >= 2)
            def _():
                wait_rows(1 - s)

    _static_slot(i % 2, step)


def moe_dispatch(h, g, idx, padinfo, n_sorted, *, tm, bm):
    n = h.shape[0]
    est = 2 * tm * D_MODEL * 4 + 2 * tm * D_MODEL * 4 + bm * D_MODEL * 4
    return pl.pallas_call(
        _dispatch_kernel,
        grid_spec=pltpu.PrefetchScalarGridSpec(
            num_scalar_prefetch=1,
            grid=(n // tm,),
            in_specs=[
                pl.BlockSpec((tm, D_MODEL), lambda i, pad: (i, 0)),
                pl.BlockSpec((1, D_MODEL), lambda i, pad: (0, 0)),
                pl.BlockSpec(memory_space=pl.ANY),
            ],
            out_specs=pl.BlockSpec(memory_space=pl.ANY),
            scratch_shapes=[
                pltpu.VMEM((2, tm // ROWS_PER_ISSUE, ROWS_PER_ISSUE, D_MODEL), F32),
                pltpu.VMEM((bm, D_MODEL), F32),
                pltpu.SMEM((2 * MOE_TOP_K * tm,), jnp.int32),
                pltpu.SemaphoreType.DMA((2,)),
                pltpu.SemaphoreType.DMA((2,)),
                pltpu.SemaphoreType.DMA((1,)),
            ],
        ),
        out_shape=jax.ShapeDtypeStruct((n_sorted, D_MODEL), F32),
        compiler_params=pltpu.CompilerParams(
            dimension_semantics=("arbitrary",), vmem_limit_bytes=_vmem(est)),
        name="moe_dispatch",
    )(padinfo, h, g, idx)


def _experts_kernel(be_ref, nu_ref, x_ref, wg_ref, wu_ref, wd_ref, y_ref):
    del be_ref
    used = pl.program_id(0) < nu_ref[0]

    @pl.when(used)
    def _():
        x = x_ref[...].astype(BF16)
        a = jnp.dot(x, wg_ref[...], preferred_element_type=F32)
        b = jnp.dot(x, wu_ref[...], preferred_element_type=F32)
        hid = (_silu(a) * b).astype(BF16)
        y_ref[...] = jnp.dot(hid, wd_ref[...], preferred_element_type=F32)

    @pl.when(jnp.logical_not(used))
    def _():
        y_ref[...] = jnp.zeros_like(y_ref)


def moe_experts(block_expert, n_used, xs, wg, wu, wd, layer, *, bm):
    n_sorted = xs.shape[0]
    est = 4 * bm * D_MODEL * 4 + 2 * 3 * D_MODEL * EXP_DIM * 2 + 3 * bm * EXP_DIM * 4 + bm * D_MODEL * 2
    row_blk = lambda i, be, nu: (jnp.where(i < nu[0], i, 0), 0)
    return pl.pallas_call(
        _experts_kernel,
        grid_spec=pltpu.PrefetchScalarGridSpec(
            num_scalar_prefetch=2,
            grid=(n_sorted // bm,),
            in_specs=[
                pl.BlockSpec((bm, D_MODEL), row_blk),
                pl.BlockSpec((None, None, D_MODEL, EXP_DIM), lambda i, be, nu: (layer, be[i], 0, 0)),
                pl.BlockSpec((None, None, D_MODEL, EXP_DIM), lambda i, be, nu: (layer, be[i], 0, 0)),
                pl.BlockSpec((None, None, EXP_DIM, D_MODEL), lambda i, be, nu: (layer, be[i], 0, 0)),
            ],
            out_specs=pl.BlockSpec((bm, D_MODEL), lambda i, be, nu: (i, 0)),
        ),
        out_shape=jax.ShapeDtypeStruct((n_sorted, D_MODEL), F32),
        compiler_params=pltpu.CompilerParams(
            dimension_semantics=("arbitrary",), vmem_limit_bytes=_vmem(est)),
        name="moe_experts",
    )(block_expert, n_used, xs, wg, wu, wd)


def _combine_kernel(h_ref, rec_ref, idx_hbm, ys_ref, o_ref, ybuf, idx_smem, row_sem, idx_sem):
    tm = h_ref.shape[0]
    i = pl.program_id(0)
    nsteps = pl.num_programs(0)
    n_idx = idx_smem.shape[0] // 2

    def row_copy(s, k, j, rr, src_row):
        return pltpu.make_async_copy(ys_ref.at[pl.ds(src_row, 1)], ybuf.at[s, k, j, pl.ds(rr, 1)], row_sem.at[s])

    def issue_rows(s):
        def body(j, c):
            base = s * n_idx + MOE_TOP_K * ROWS_PER_ISSUE * j
            for rr in range(ROWS_PER_ISSUE):
                for k in range(MOE_TOP_K):
                    row_copy(s, k, j, rr, idx_smem[base + MOE_TOP_K * rr + k]).start()
            return c
        lax.fori_loop(0, tm // ROWS_PER_ISSUE, body, 0)

    def wait_rows(s):
        def body(r, c):
            for k in range(MOE_TOP_K):
                row_copy(s, k, 0, 0, 0).wait()
            return c
        lax.fori_loop(0, tm, body, 0, unroll=16)

    def step(s):
        @pl.when(i == 0)
        def _():
            _idx_copy(idx_hbm, idx_smem, idx_sem, 0, s).start()
            _idx_copy(idx_hbm, idx_smem, idx_sem, 0, s).wait()
            issue_rows(s)

            @pl.when(nsteps >= 2)
            def _():
                _idx_copy(idx_hbm, idx_smem, idx_sem, 1, 1 - s).start()

        @pl.when(i + 1 < nsteps)
        def _():
            _idx_copy(idx_hbm, idx_smem, idx_sem, i + 1, 1 - s).wait()
            issue_rows(1 - s)

        wait_rows(s)

        @pl.when(i + 2 < nsteps)
        def _():
            _idx_copy(idx_hbm, idx_smem, idx_sem, i + 2, s).start()

        rec = rec_ref[...]
        w1 = rec[:, REC_W:REC_W + 1]
        w2 = rec[:, REC_W + 1:REC_W + 2]
        y1 = ybuf[s, 0].reshape(tm, D_MODEL)
        y2 = ybuf[s, 1].reshape(tm, D_MODEL)
        o_ref[...] = h_ref[...] + w1 * y1 + w2 * y2

    _static_slot(i % 2, step)


def moe_combine(h, rec, idx, ys, *, tm):
    n = h.shape[0]
    est = 4 * tm * D_MODEL * 4 + 2 * tm * LANES * 4 + 2 * MOE_TOP_K * tm * D_MODEL * 4
    return pl.pallas_call(
        _combine_kernel,
        grid=(n // tm,),
        in_specs=[
            pl.BlockSpec((tm, D_MODEL), lambda i: (i, 0)),
            pl.BlockSpec((tm, LANES), lambda i: (i, 0)),
            pl.BlockSpec(memory_space=pl.ANY),
            pl.BlockSpec(memory_space=pl.ANY),
        ],
        out_specs=pl.BlockSpec((tm, D_MODEL), lambda i: (i, 0)),
        out_shape=jax.ShapeDtypeStruct((n, D_MODEL), F32),
        scratch_shapes=[
            pltpu.VMEM((2, MOE_TOP_K, tm // ROWS_PER_ISSUE, ROWS_PER_ISSUE, D_MODEL), F32),
            pltpu.SMEM((2 * MOE_TOP_K * tm,), jnp.int32),
            pltpu.SemaphoreType.DMA((2,)),
            pltpu.SemaphoreType.DMA((2,)),
        ],
        compiler_params=pltpu.CompilerParams(
            dimension_semantics=("arbitrary",), vmem_limit_bytes=_vmem(est)),
        name="moe_combine",
    )(h, rec, idx, ys)


def ffn_moe(h, g, wr, wg, wu, wd, layer, *, bm=512, tm=512):
    n = h.shape[0]
    n_sorted = MOE_TOP_K * n + N_EXP * bm
    rec, cnt = moe_route(h, g, wr)
    expert = rec[:, REC_E:REC_E + MOE_TOP_K].astype(jnp.int32)
    rank = rec[:, REC_RANK:REC_RANK + MOE_TOP_K].astype(jnp.int32)
    counts = cnt[0, :N_EXP].astype(jnp.int32)
    padded = (counts + bm - 1) // bm * bm
    pend = jnp.cumsum(padded)
    pstart = pend - padded
    dest = jnp.sum(jnp.where(expert[..., None] == jnp.arange(N_EXP), pstart, 0), axis=-1) + rank
    idx = dest.reshape(MOE_TOP_K * n)
    n_used = (pend[-1:] // bm).astype(jnp.int32)
    padinfo = jnp.concatenate([pstart + counts, padded - counts, n_used]).astype(jnp.int32)
    blk_start = jnp.arange(n_sorted // bm, dtype=jnp.int32) * bm
    block_expert = jnp.minimum(jnp.sum(blk_start[:, None] >= pend[None, :], axis=1), N_EXP - 1).astype(jnp.int32)
    xs = moe_dispatch(h, g, idx, padinfo, n_sorted, tm=tm, bm=bm)
    ys = moe_experts(block_expert, n_used, xs, wg, wu, wd, layer, bm=bm)
    return moe_combine(h, rec, idx, ys, tm=tm)


PLE_ROW_CHUNKS = 2


def _ple_kernel(h_ref, p_ref, gi_ref, wgate_ref, wp_ref, go_ref, o_ref):
    rc = h_ref.shape[0] // PLE_ROW_CHUNKS
    rows = [slice(c * rc, (c + 1) * rc) for c in range(PLE_ROW_CHUNKS)]
    hn = [_rms(h_ref[r, :], gi_ref[...]).astype(BF16) for r in rows]
    for c, r in enumerate(rows):
        gate = jax.nn.sigmoid(jnp.dot(hn[c], wgate_ref[...], preferred_element_type=F32))
        e = jnp.dot(p_ref[r, :].astype(BF16), wp_ref[...], preferred_element_type=F32)
        o_ref[r, :] = h_ref[r, :] + gate * _rms(e, go_ref[...])


def ple(h, p_all, layer, gi, wgate, wp, go, *, tm=512):
    n = h.shape[0]
    est = 4 * tm * D_MODEL * 4 + 2 * D_MODEL * D_MODEL * 2 + 2 * PLE_DIM * D_MODEL * 2 + 3 * tm * D_MODEL * 4
    return pl.pallas_call(
        _ple_kernel,
        grid=(n // tm,),
        in_specs=[
            pl.BlockSpec((tm, D_MODEL), lambda i: (i, 0)),
            pl.BlockSpec((None, tm, PLE_DIM), lambda i: (layer, i, 0)),
            pl.BlockSpec((1, D_MODEL), lambda i: (0, 0)),
            pl.BlockSpec((None, D_MODEL, D_MODEL), lambda i: (layer, 0, 0)),
            pl.BlockSpec((None, PLE_DIM, D_MODEL), lambda i: (layer, 0, 0)),
            pl.BlockSpec((1, D_MODEL), lambda i: (0, 0)),
        ],
        out_specs=pl.BlockSpec((tm, D_MODEL), lambda i: (i, 0)),
        out_shape=jax.ShapeDtypeStruct((n, D_MODEL), F32),
        compiler_params=pltpu.CompilerParams(
            dimension_semantics=("parallel",), vmem_limit_bytes=_vmem(est)),
        name="ple",
    )(h, p_all, gi, wgate, wp, go)


def _prep_layer(i, P):
    lr0 = Z_AQ
    n_in = P["w_in"].shape[-1]

    def w_in_cols(a, b):
        return lax.slice(P["w_in"], (i, 0, a), (i + 1, D_MODEL, b)).reshape(D_MODEL, b - a).astype(BF16)

    w_in_r = jnp.concatenate([w_in_cols(0, lr0), w_in_cols(lr0 + 2 * GLA_RANK, n_in)], axis=1)
    w_lr = jnp.concatenate([w_in_cols(lr0, lr0 + 2 * GLA_RANK),
                            jnp.zeros((D_MODEL, LR_W - 2 * GLA_RANK), BF16)], axis=1)
    zpad = jnp.zeros((LR_W, GLA_KW), F32)
    wgf = zpad.at[0:GLA_RANK].set(P["w_gate_f"][i]).astype(BF16)
    wgb = zpad.at[GLA_RANK:2 * GLA_RANK].set(P["w_gate_b"][i]).astype(BF16)
    L = dict(
        g_mix=P["g_mix"][i][None], w_in=w_in_r, w_lr=w_lr, wgf=wgf, wgb=wgb,
        bgf=P["b_gate_f"][i][None], bgb=P["b_gate_b"][i][None],
        gla_norm=P["gla_norm"][i].reshape(GLA_H, 1, GLA_DV),
        q_norm=P["q_norm"][i][None], k_norm=P["k_norm"][i][None],
        bias_tbl=natten_bias_table(P["rpb"][i]), attn_norm=P["attn_norm"][i][None],
        g_ffn=P["g_ffn"][i][None], g_ple_in=P["g_ple_in"][i][None], g_ple_out=P["g_ple_out"][i][None],
    )
    if i % 2 == 1:
        L["wr"] = jnp.zeros((D_MODEL, LANES), F32).at[:, :N_EXP].set(P["w_router"][i // 2]).astype(BF16)
    return L


def _prep_stacked(P):
    names = ("w_o", "w_ple_gate", "w_ple", "w_ff_gate", "w_ff_up", "w_ff_down",
             "w_exp_gate", "w_exp_up", "w_exp_down")
    return {k: P[k].astype(BF16) for k in names}


def _trunk(x, p, layers, W):
    B, T, _ = x.shape
    n = B * T
    h = x.reshape(n, D_MODEL)
    p_all = p.reshape(DEPTH, n, PLE_DIM)
    for i, L in enumerate(layers):
        z, zl = inproj(h, L["g_mix"], L["w_in"], L["w_lr"])
        z3 = z.reshape(B, T, NZ)
        zl3 = zl.reshape(B, T, LR_W)
        o_f = gla_direction(z3, zl3, L["wgf"], L["bgf"], reverse=False)
        o_gla = gla_direction(z3, zl3, L["wgb"], L["bgb"], reverse=True, o_other=o_f, gn=L["gla_norm"])
        o_na = natten(z3, L["bias_tbl"], L["q_norm"], L["k_norm"])
        h = outproj(o_gla.reshape(n, GLA_VW), o_na.reshape(n, NA_W), L["attn_norm"], W["w_o"], i, h)
        if i % 2 == 0:
            h = ffn_dense(h, L["g_ffn"], W["w_ff_gate"], W["w_ff_up"], W["w_ff_down"], i // 2)
        else:
            h = ffn_moe(h, L["g_ffn"], L["wr"], W["w_exp_gate"], W["w_exp_up"], W["w_exp_down"], i // 2)
        h = ple(h, p_all, i, L["g_ple_in"], W["w_ple_gate"], W["w_ple"], L["g_ple_out"])
    return h.reshape(B, T, D_MODEL)


def kernel(x_prompt, x_sample, p_prompt, p_sample, g_mix, w_in, w_gate_f, b_gate_f, w_gate_b, b_gate_b, gla_norm, q_norm, k_norm, rpb, attn_norm, w_o, g_ffn, w_ff_gate, w_ff_up, w_ff_down, w_router, w_exp_gate, w_exp_up, w_exp_down, g_ple_in, w_ple_gate, w_ple, g_ple_out):
    P = dict(g_mix=g_mix, w_in=w_in, w_gate_f=w_gate_f, b_gate_f=b_gate_f, w_gate_b=w_gate_b,
             b_gate_b=b_gate_b, gla_norm=gla_norm, q_norm=q_norm, k_norm=k_norm, rpb=rpb,
             attn_norm=attn_norm, w_o=w_o, g_ffn=g_ffn, w_ff_gate=w_ff_gate, w_ff_up=w_ff_up,
             w_ff_down=w_ff_down, w_router=w_router, w_exp_gate=w_exp_gate, w_exp_up=w_exp_up,
             w_exp_down=w_exp_down, g_ple_in=g_ple_in, w_ple_gate=w_ple_gate, w_ple=w_ple,
             g_ple_out=g_ple_out)
    layers = [_prep_layer(i, P) for i in range(DEPTH)]
    W = _prep_stacked(P)
    return (_trunk(x_prompt, p_prompt, layers, W), _trunk(x_sample, p_sample, layers, W))
```

```python
import functools

import jax
import jax.numpy as jnp
from jax import lax
from jax.experimental import pallas as pl
from jax.experimental.pallas import tpu as pltpu

F32 = jnp.float32
BF16 = jnp.bfloat16

D_MODEL = 2048
DEPTH = 4
GRID_W = 64
GLA_H = 4
GLA_VW = 1024
GLA_KW = 512
GLA_DK = 128
GLA_DV = 256
GLA_RANK = 16
GLA_INV_NORMALIZER = 1.0 / 16.0
GLA_CHUNK = 64
NA_W = 1024
NA_H = 8
NA_HD = 128
WIN_ROWS = 8
WIN_COLS = 16
FF_DIM = 4096
N_EXP = 8
EXP_DIM = 1024
PLE_DIM = 256
EPS = 1e-6
NEG_INF = -1e30

Z_GQ, Z_GK, Z_GV, Z_GG = 0, 512, 1024, 2048
Z_AQ, Z_AK, Z_AV, NZ = 3072, 4096, 5120, 6144
LR_W = 256

V7X_VMEM_BYTES = 64 * 1024 * 1024
LANES = 128


def _vmem(nbytes):
    return int(min(nbytes * 3 // 2 + (4 << 20), V7X_VMEM_BYTES - (6 << 20)))


def _rms(x, g):
    ms = jnp.mean(x * x, axis=-1, keepdims=True)
    return x * lax.rsqrt(ms + EPS) * g


def _silu(x):
    return x * jax.nn.sigmoid(x)


def _inproj_kernel(h_ref, g_ref, w_ref, wl_ref, z_ref, zl_ref, u_scr):
    j = pl.program_id(1)

    @pl.when(j == 0)
    def _():
        u = _rms(h_ref[...], g_ref[...]).astype(BF16)
        u_scr[...] = u
        zl_ref[...] = jnp.dot(u, wl_ref[...], preferred_element_type=F32).astype(zl_ref.dtype)

    @pl.when(j > 0)
    def _():
        z_ref[...] = jnp.dot(u_scr[...], w_ref[...], preferred_element_type=F32).astype(z_ref.dtype)


def inproj(h, g, w, wl, *, tm=1024, tn=1536):
    n = h.shape[0]
    nmain = NZ // tn
    est = (2 * tm * D_MODEL * 4 + 2 * D_MODEL * tn * 2 + 2 * tm * tn * 2 + tm * D_MODEL * 2 + tm * tn * 4
           + 2 * D_MODEL * LR_W * 2 + 2 * tm * LR_W * 2)
    main = lambda j: jnp.maximum(j - 1, 0)
    return pl.pallas_call(
        _inproj_kernel,
        grid=(n // tm, nmain + 1),
        in_specs=[
            pl.BlockSpec((tm, D_MODEL), lambda i, j: (i, 0)),
            pl.BlockSpec((1, D_MODEL), lambda i, j: (0, 0)),
            pl.BlockSpec((D_MODEL, tn), lambda i, j: (0, main(j))),
            pl.BlockSpec((D_MODEL, LR_W), lambda i, j: (0, 0)),
        ],
        out_specs=[pl.BlockSpec((tm, tn), lambda i, j: (i, main(j))),
                   pl.BlockSpec((tm, LR_W), lambda i, j: (i, 0))],
        out_shape=[jax.ShapeDtypeStruct((n, NZ), BF16), jax.ShapeDtypeStruct((n, LR_W), BF16)],
        scratch_shapes=[pltpu.VMEM((tm, D_MODEL), BF16)],
        compiler_params=pltpu.CompilerParams(
            dimension_semantics=("parallel", "arbitrary"), vmem_limit_bytes=_vmem(est)),
        name="inproj",
    )(h, g, w, wl)


def _cumsum_rows(x, seg):
    idx = lax.broadcasted_iota(jnp.int32, x.shape, 0) & (seg - 1)
    s = 1
    while s < seg:
        x = x + jnp.where(idx >= s, pltpu.roll(x, s, 0), 0.0)
        s *= 2
    return x


def _log_sigmoid(x):
    return jnp.minimum(x, 0.0) - jnp.log(1.0 + jnp.exp(-jnp.abs(x)))


def _split_bf16(x):
    hi = x.astype(BF16)
    r1 = x - hi.astype(F32)
    mid = r1.astype(BF16)
    lo = (r1 - mid.astype(F32)).astype(BF16)
    return hi, mid, lo


def _gla_kernel(*refs, reverse, final, tb):
    if final:
        q_ref, k_ref, v_ref, lr_ref, wg_ref, bg_ref, of_ref, gg_ref, gn_ref, o_ref, s_scr = refs
    else:
        q_ref, k_ref, v_ref, lr_ref, wg_ref, bg_ref, o_ref, s_scr = refs
    C = GLA_CHUNK
    nchunk = tb // C

    @pl.when(pl.program_id(2) == 0)
    def _():
        s_scr[...] = jnp.zeros_like(s_scr)

    x = jnp.dot(lr_ref[...], wg_ref[...], preferred_element_type=F32) + bg_ref[...]
    la_parts = _split_bf16(_log_sigmoid(x) * GLA_INV_NORMALIZER)

    row = lax.broadcasted_iota(jnp.int32, (C, C), 0)
    col = lax.broadcasted_iota(jnp.int32, (C, C), 1)
    mask = (col >= row) if reverse else (col <= row)
    tri = jnp.where(mask, 1.0, 0.0).astype(BF16)
    scale = GLA_DK ** -0.5

    order = [(nchunk - 1 - ci) if reverse else ci for ci in range(nchunk)]
    sl = [slice(c * C, (c + 1) * C) for c in range(nchunk)]
    qts, atts, incs, dls = {}, {}, {}, {}
    bs = {c: sum(jnp.dot(tri, part[sl[c]], preferred_element_type=F32) for part in la_parts) for c in order}
    for c in order:
        b = bs[c]
        blast = b[0:1, :] if reverse else b[C - 1:C, :]
        q = q_ref[sl[c], :].astype(F32)
        k = k_ref[sl[c], :].astype(F32)
        qts[c] = (q * jnp.exp(b) * scale).astype(BF16)
        kdec = k * jnp.exp(-b)
        dl = jnp.exp(blast)
        kt = kdec.astype(BF16)
        ke_t = (kdec * dl).T.astype(BF16)
        dls[c] = jnp.broadcast_to(dl, (8, GLA_DK)).T[:, 0:1]
        att = lax.dot_general(qts[c], kt, (((1,), (1,)), ((), ())), preferred_element_type=F32)
        atts[c] = jnp.where(mask, att, 0.0).astype(BF16)
        incs[c] = jnp.dot(ke_t, v_ref[sl[c], :], preferred_element_type=F32)
    state = s_scr[...]
    states = {}
    for c in order:
        states[c] = state.astype(BF16)
        state = dls[c] * state + incs[c]
    s_scr[...] = state
    for c in order:
        o = (jnp.dot(atts[c], v_ref[sl[c], :], preferred_element_type=F32)
             + jnp.dot(qts[c], states[c], preferred_element_type=F32))
        if final:
            y = _rms(o + of_ref[sl[c], :], gn_ref[...])
            gg = gg_ref[sl[c], :].astype(F32)
            o_ref[sl[c], :] = (y * _silu(gg)).astype(o_ref.dtype)
        else:
            o_ref[sl[c], :] = o


def gla_direction(z3, zl3, wg, bg, *, reverse, o_other=None, gn=None, tb=2048):
    B, T, _ = z3.shape
    nb = T // tb
    final = o_other is not None
    blk = (lambda j: nb - 1 - j) if reverse else (lambda j: j)
    qw, vw = GLA_DK, GLA_DV
    in_specs = [
        pl.BlockSpec((None, tb, qw), lambda b, h, j: (b, blk(j), Z_GQ // qw + h)),
        pl.BlockSpec((None, tb, qw), lambda b, h, j: (b, blk(j), Z_GK // qw + h)),
        pl.BlockSpec((None, tb, vw), lambda b, h, j: (b, blk(j), Z_GV // vw + h)),
        pl.BlockSpec((None, tb, LR_W), lambda b, h, j: (b, blk(j), 0)),
        pl.BlockSpec((LR_W, qw), lambda b, h, j: (0, h)),
        pl.BlockSpec((1, qw), lambda b, h, j: (0, h)),
    ]
    args = [z3, z3, z3, zl3, wg, bg]
    if final:
        in_specs += [
            pl.BlockSpec((None, tb, vw), lambda b, h, j: (b, blk(j), h)),
            pl.BlockSpec((None, tb, vw), lambda b, h, j: (b, blk(j), Z_GG // vw + h)),
            pl.BlockSpec((None, 1, vw), lambda b, h, j: (h, 0, 0)),
        ]
        args += [o_other, z3, gn]
    out_dtype = BF16 if final else F32
    est = tb * (2 * 2 * (2 * qw + vw + LR_W) + (2 * (4 + 2) * vw if final else 0) + 2 * 4 * vw + 10 * 4 * qw)
    return pl.pallas_call(
        functools.partial(_gla_kernel, reverse=reverse, final=final, tb=tb),
        grid=(B, GLA_H, nb),
        in_specs=in_specs,
        out_specs=pl.BlockSpec((None, tb, vw), lambda b, h, j: (b, blk(j), h)),
        out_shape=jax.ShapeDtypeStruct((B, T, GLA_VW), out_dtype),
        scratch_shapes=[pltpu.VMEM((GLA_DK, GLA_DV), F32)],
        compiler_params=pltpu.CompilerParams(
            dimension_semantics=("parallel", "parallel", "arbitrary"), vmem_limit_bytes=_vmem(est)),
        name="gla_bwd" if reverse else "gla_fwd",
    )(*args)


def _natten_kernel(q_ref, k_ref, v_ref, bias_ref, qg_ref, kg_ref, o_ref, kn_scr, *, rows, rq):
    T = rows * GRID_W
    KB = 2048
    WK = WIN_ROWS * GRID_W
    j = pl.program_id(2)

    @pl.when(j == 0)
    def _():
        def norm_block(i, carry):
            off = pl.multiple_of(i * KB, KB)
            kk = k_ref[pl.ds(off, KB), :].astype(F32)
            kn_scr[pl.ds(off, KB), :] = _rms(kk, kg_ref[...]).astype(BF16)
            return carry

        lax.fori_loop(0, T // KB, norm_block, 0)

    qn_blk = (_rms(q_ref[...].astype(F32), qg_ref[...]) * (NA_HD ** -0.5)).astype(BF16)
    koffs, probs, sums = [], [], []
    for i in range(rq):
        r = j * rq + i
        rs = jnp.clip(r - WIN_ROWS // 2, 0, rows - WIN_ROWS)
        dti = rs - r + (WIN_ROWS - 1)
        koffs.append(pl.multiple_of(rs * GRID_W, GRID_W))
        kw = kn_scr[pl.ds(koffs[i], WK), :]
        s = lax.dot_general(qn_blk[i * GRID_W:(i + 1) * GRID_W], kw, (((1,), (1,)), ((), ())),
                            preferred_element_type=F32)
        s = s + bias_ref[dti]
        p = jnp.exp(s - jnp.max(s, axis=-1, keepdims=True))
        sums.append(jnp.sum(p, axis=-1, keepdims=True))
        probs.append(p.astype(BF16))
    for i in range(rq):
        vw = v_ref[pl.ds(koffs[i], WK), :]
        o = jnp.dot(probs[i], vw, preferred_element_type=F32) / sums[i]
        o_ref[i * GRID_W:(i + 1) * GRID_W, :] = o.astype(o_ref.dtype)


def natten(z3, bias_tbl, qg, kg, *, rq=64):
    B, T, _ = z3.shape
    rows = T // GRID_W
    assert rows >= WIN_ROWS and rows % rq == 0
    tq = rq * GRID_W
    hd = NA_HD
    est = 2 * 2 * T * hd * 2 + T * hd * 2 + 2 * WIN_ROWS * GRID_W * WIN_ROWS * GRID_W * 4 + 4 * tq * hd * 2
    return pl.pallas_call(
        functools.partial(_natten_kernel, rows=rows, rq=rq),
        grid=(B, NA_H, rows // rq),
        in_specs=[
            pl.BlockSpec((None, tq, hd), lambda b, h, j: (b, j, Z_AQ // hd + h)),
            pl.BlockSpec((None, T, hd), lambda b, h, j: (b, 0, Z_AK // hd + h)),
            pl.BlockSpec((None, T, hd), lambda b, h, j: (b, 0, Z_AV // hd + h)),
            pl.BlockSpec((None, WIN_ROWS, GRID_W, WIN_ROWS * GRID_W), lambda b, h, j: (h, 0, 0, 0)),
            pl.BlockSpec((1, hd), lambda b, h, j: (0, 0)),
            pl.BlockSpec((1, hd), lambda b, h, j: (0, 0)),
        ],
        out_specs=pl.BlockSpec((None, tq, hd), lambda b, h, j: (b, j, h)),
        out_shape=jax.ShapeDtypeStruct((B, T, NA_W), BF16),
        scratch_shapes=[pltpu.VMEM((T, hd), BF16)],
        compiler_params=pltpu.CompilerParams(
            dimension_semantics=("parallel", "parallel", "arbitrary"), vmem_limit_bytes=_vmem(est)),
        name="natten",
    )(z3, z3, z3, bias_tbl, qg, kg)


def natten_bias_table(rpb):
    c = jnp.arange(GRID_W, dtype=jnp.int32)
    c_start = jnp.clip(c - WIN_COLS // 2, 0, GRID_W - WIN_COLS)
    valid = (c[None, :] >= c_start[:, None]) & (c[None, :] < c_start[:, None] + WIN_COLS)
    dc = jnp.clip(c[None, :] - c[:, None], -(WIN_COLS - 1), WIN_COLS - 1) + (WIN_COLS - 1)
    bias_c = jnp.where(valid, rpb.astype(F32)[:, :, dc], NEG_INF)
    dr = jnp.arange(WIN_ROWS)[:, None] + jnp.arange(WIN_ROWS)[None, :]
    tbl = bias_c[:, dr]
    return tbl.transpose(0, 1, 3, 2, 4).reshape(NA_H, WIN_ROWS, GRID_W, WIN_ROWS * GRID_W)


def _outproj_kernel(og_ref, on_ref, an_ref, w_ref, h_ref, o_ref):
    na = _rms(on_ref[...].astype(F32), an_ref[...]).astype(BF16)
    acc = jnp.dot(og_ref[...], w_ref[0:GLA_VW, :], preferred_element_type=F32)
    acc += jnp.dot(na, w_ref[GLA_VW:GLA_VW + NA_W, :], preferred_element_type=F32)
    o_ref[...] = h_ref[...] + acc


def outproj(o_gla, o_na, an, w_o, layer, h, *, tm=512):
    n = h.shape[0]
    est = 2 * 2 * tm * 1024 * 2 + 2 * D_MODEL * D_MODEL * 2 + 4 * tm * D_MODEL * 4 + tm * D_MODEL * 4
    return pl.pallas_call(
        _outproj_kernel,
        grid=(n // tm,),
        in_specs=[
            pl.BlockSpec((tm, GLA_VW), lambda i: (i, 0)),
            pl.BlockSpec((tm, NA_W), lambda i: (i, 0)),
            pl.BlockSpec((1, NA_W), lambda i: (0, 0)),
            pl.BlockSpec((None, D_MODEL, D_MODEL), lambda i: (layer, 0, 0)),
            pl.BlockSpec((tm, D_MODEL), lambda i: (i, 0)),
        ],
        out_specs=pl.BlockSpec((tm, D_MODEL), lambda i: (i, 0)),
        out_shape=jax.ShapeDtypeStruct((n, D_MODEL), F32),
        compiler_params=pltpu.CompilerParams(
            dimension_semantics=("parallel",), vmem_limit_bytes=_vmem(est)),
        name="outproj",
    )(o_gla, o_na, an, w_o, h)


FFN_TF = 512


def _ffn_kernel(h_ref, g_ref, wg_ref, wu_ref, wd_ref, o_ref, u_scr):
    @pl.when(pl.program_id(1) == 0)
    def _():
        h = h_ref[...]
        u_scr[...] = _rms(h, g_ref[...]).astype(BF16)
        o_ref[...] = h

    u = u_scr[...]
    a = jnp.dot(u, wg_ref[...], preferred_element_type=F32)
    b = jnp.dot(u, wu_ref[...], preferred_element_type=F32)
    hid = (_silu(a) * b).astype(BF16)
    o_ref[...] += jnp.dot(hid, wd_ref[...], preferred_element_type=F32)


def ffn_dense(h, g, wg, wu, wd, layer, *, tm=1024, tf=FFN_TF):
    n = h.shape[0]
    est = 4 * tm * D_MODEL * 4 + tm * D_MODEL * 2 + 2 * 3 * D_MODEL * tf * 2 + 3 * tm * tf * 4
    return pl.pallas_call(
        _ffn_kernel,
        grid=(n // tm, FF_DIM // tf),
        in_specs=[
            pl.BlockSpec((tm, D_MODEL), lambda i, f: (i, 0)),
            pl.BlockSpec((1, D_MODEL), lambda i, f: (0, 0)),
            pl.BlockSpec((None, D_MODEL, tf), lambda i, f: (layer, 0, f)),
            pl.BlockSpec((None, D_MODEL, tf), lambda i, f: (layer, 0, f)),
            pl.BlockSpec((None, tf, D_MODEL), lambda i, f: (layer, f, 0)),
        ],
        out_specs=pl.BlockSpec((tm, D_MODEL), lambda i, f: (i, 0)),
        out_shape=jax.ShapeDtypeStruct((n, D_MODEL), F32),
        scratch_shapes=[pltpu.VMEM((tm, D_MODEL), BF16)],
        compiler_params=pltpu.CompilerParams(
            dimension_semantics=("parallel", "arbitrary"), vmem_limit_bytes=_vmem(est)),
        name="ffn_dense",
    )(h, g, wg, wu, wd)


REC_E, REC_W, REC_RANK = 0, 2, 4
MOE_TOP_K = 2


def _route_kernel(h_ref, g_ref, wr_ref, rec_ref, cnt_ref, carry_scr):
    tm = h_ref.shape[0]
    lane = lax.broadcasted_iota(jnp.int32, (tm, LANES), 1)

    @pl.when(pl.program_id(0) == 0)
    def _():
        carry_scr[...] = jnp.zeros_like(carry_scr)

    u = _rms(h_ref[...], g_ref[...]).astype(BF16)
    logits = jnp.dot(u, wr_ref[...], preferred_element_type=F32)
    lg = jnp.where(lane < N_EXP, logits, -jnp.inf)
    v1 = jnp.max(lg, axis=-1, keepdims=True)
    i1 = jnp.min(jnp.where(lg == v1, lane, LANES), axis=-1, keepdims=True)
    lg2 = jnp.where(lane == i1, -jnp.inf, lg)
    v2 = jnp.max(lg2, axis=-1, keepdims=True)
    i2 = jnp.min(jnp.where(lg2 == v2, lane, LANES), axis=-1, keepdims=True)
    e2 = jnp.exp(v2 - v1)
    w1 = 1.0 / (1.0 + e2)
    w2 = e2 / (1.0 + e2)
    onehot = jnp.where(lane == i1, 1.0, 0.0) + jnp.where(lane == i2, 1.0, 0.0)
    incl = _cumsum_rows(onehot, tm)
    before = incl - onehot + carry_scr[0:1, :]
    r1 = jnp.sum(jnp.where(lane == i1, before, 0.0), axis=-1, keepdims=True)
    r2 = jnp.sum(jnp.where(lane == i2, before, 0.0), axis=-1, keepdims=True)
    rec = jnp.where(lane == REC_E, i1.astype(F32), 0.0)
    rec = jnp.where(lane == REC_E + 1, i2.astype(F32), rec)
    rec = jnp.where(lane == REC_W, w1, rec)
    rec = jnp.where(lane == REC_W + 1, w2, rec)
    rec = jnp.where(lane == REC_RANK, r1, rec)
    rec = jnp.where(lane == REC_RANK + 1, r2, rec)
    rec_ref[...] = rec
    carry_scr[...] = carry_scr[...] + incl[tm - 1:tm, :]
    cnt_ref[...] = carry_scr[...]


def moe_route(h, g, wr, *, tm=512):
    n = h.shape[0]
    est = 2 * tm * D_MODEL * 4 + D_MODEL * LANES * 4 + 8 * tm * LANES * 4
    return pl.pallas_call(
        _route_kernel,
        grid=(n // tm,),
        in_specs=[
            pl.BlockSpec((tm, D_MODEL), lambda i: (i, 0)),
            pl.BlockSpec((1, D_MODEL), lambda i: (0, 0)),
            pl.BlockSpec((D_MODEL, LANES), lambda i: (0, 0)),
        ],
        out_specs=[pl.BlockSpec((tm, LANES), lambda i: (i, 0)), pl.BlockSpec((8, LANES), lambda i: (0, 0))],
        out_shape=[jax.ShapeDtypeStruct((n, LANES), F32), jax.ShapeDtypeStruct((8, LANES), F32)],
        scratch_shapes=[pltpu.VMEM((8, LANES), F32)],
        compiler_params=pltpu.CompilerParams(
            dimension_semantics=("arbitrary",), vmem_limit_bytes=_vmem(est)),
        name="moe_route",
    )(h, g, wr)


def _static_slot(slot, fn):
    for s in range(2):
        pl.when(slot == s)(functools.partial(fn, s))


ROWS_PER_ISSUE = 8


def _idx_copy(idx_hbm, idx_smem, sem, step, s):
    n = idx_smem.shape[0] // 2
    return pltpu.make_async_copy(idx_hbm.at[pl.ds(pl.multiple_of(step * n, n), n)],
                                 idx_smem.at[pl.ds(s * n, n)], sem.at[s])


def _dispatch_kernel(pad_ref, h_ref, g_ref, idx_hbm, xs_ref, ubuf, zrow, idx_smem, row_sem, idx_sem, pad_sem):
    tm = h_ref.shape[0]
    i = pl.program_id(0)
    nsteps = pl.num_programs(0)
    n_idx = idx_smem.shape[0] // 2

    def row_copy(s, j, rr, dst_row):
        return pltpu.make_async_copy(ubuf.at[s, j, pl.ds(rr, 1)], xs_ref.at[pl.ds(dst_row, 1)], row_sem.at[s])

    def wait_rows(s):
        def body(r, c):
            for _ in range(MOE_TOP_K):
                row_copy(s, 0, 0, 0).wait()
            return c
        lax.fori_loop(0, tm, body, 0, unroll=16)

    def issue_rows(s):
        def body(j, c):
            base = s * n_idx + MOE_TOP_K * ROWS_PER_ISSUE * j
            for rr in range(ROWS_PER_ISSUE):
                for k in range(MOE_TOP_K):
                    row_copy(s, j, rr, idx_smem[base + MOE_TOP_K * rr + k]).start()
            return c
        lax.fori_loop(0, tm // ROWS_PER_ISSUE, body, 0)

    def zero_pad_rows():
        zrow[...] = jnp.zeros_like(zrow)
        for e in range(N_EXP):
            def pad_copy(j, e=e):
                return pltpu.make_async_copy(zrow.at[pl.ds(0, 1)], xs_ref.at[pl.ds(pad_ref[e] + j, 1)], pad_sem.at[0])

            def start(j, c, pad_copy=pad_copy):
                pad_copy(j).start()
                return c

            def wait(j, c, pad_copy=pad_copy):
                pad_copy(j).wait()
                return c
            lax.fori_loop(0, pad_ref[N_EXP + e], start, 0)
            lax.fori_loop(0, pad_ref[N_EXP + e], wait, 0)

        bm = zrow.shape[0]

        def tail_copy(b):
            return pltpu.make_async_copy(zrow, xs_ref.at[pl.ds(pl.multiple_of(b * bm, bm), bm)], pad_sem.at[0])

        def tail_start(b, c):
            tail_copy(b).start()
            return c

        def tail_wait(b, c):
            tail_copy(b).wait()
            return c
        lax.fori_loop(pad_ref[2 * N_EXP], xs_ref.shape[0] // bm, tail_start, 0)
        lax.fori_loop(pad_ref[2 * N_EXP], xs_ref.shape[0] // bm, tail_wait, 0)

    def step(s):
        @pl.when(i == 0)
        def _():
            _idx_copy(idx_hbm, idx_smem, idx_sem, 0, s).start()

        _idx_copy(idx_hbm, idx_smem, idx_sem, i, s).wait()

        @pl.when(i + 1 < nsteps)
        def _():
            _idx_copy(idx_hbm, idx_smem, idx_sem, i + 1, 1 - s).start()

        @pl.when(i >= 2)
        def _():
            wait_rows(s)

        ubuf[s] = _rms(h_ref[...], g_ref[...]).reshape(tm // ROWS_PER_ISSUE, ROWS_PER_ISSUE, D_MODEL)
        issue_rows(s)

        @pl.when(i == nsteps - 1)
        def _():
            zero_pad_rows()
            wait_rows(s)

            @pl.when(nsteps >= 2)
            def _():
                wait_rows(1 - s)

    _static_slot(i % 2, step)


def moe_dispatch(h, g, idx, padinfo, n_sorted, *, tm, bm):
    n = h.shape[0]
    est = 2 * tm * D_MODEL * 4 + 2 * tm * D_MODEL * 4 + bm * D_MODEL * 4
    return pl.pallas_call(
        _dispatch_kernel,
        grid_spec=pltpu.PrefetchScalarGridSpec(
            num_scalar_prefetch=1,
            grid=(n // tm,),
            in_specs=[
                pl.BlockSpec((tm, D_MODEL), lambda i, pad: (i, 0)),
                pl.BlockSpec((1, D_MODEL), lambda i, pad: (0, 0)),
                pl.BlockSpec(memory_space=pl.ANY),
            ],
            out_specs=pl.BlockSpec(memory_space=pl.ANY),
            scratch_shapes=[
                pltpu.VMEM((2, tm // ROWS_PER_ISSUE, ROWS_PER_ISSUE, D_MODEL), F32),
                pltpu.VMEM((bm, D_MODEL), F32),
                pltpu.SMEM((2 * MOE_TOP_K * tm,), jnp.int32),
                pltpu.SemaphoreType.DMA((2,)),
                pltpu.SemaphoreType.DMA((2,)),
                pltpu.SemaphoreType.DMA((1,)),
            ],
        ),
        out_shape=jax.ShapeDtypeStruct((n_sorted, D_MODEL), F32),
        compiler_params=pltpu.CompilerParams(
            dimension_semantics=("arbitrary",), vmem_limit_bytes=_vmem(est)),
        name="moe_dispatch",
    )(padinfo, h, g, idx)


def _experts_kernel(be_ref, nu_ref, x_ref, wg_ref, wu_ref, wd_ref, y_ref):
    del be_ref
    used = pl.program_id(0) < nu_ref[0]

    @pl.when(used)
    def _():
        x = x_ref[...].astype(BF16)
        a = jnp.dot(x, wg_ref[...], preferred_element_type=F32)
        b = jnp.dot(x, wu_ref[...], preferred_element_type=F32)
        hid = (_silu(a) * b).astype(BF16)
        y_ref[...] = jnp.dot(hid, wd_ref[...], preferred_element_type=F32)

    @pl.when(jnp.logical_not(used))
    def _():
        y_ref[...] = jnp.zeros_like(y_ref)


def moe_experts(block_expert, n_used, xs, wg, wu, wd, layer, *, bm):
    n_sorted = xs.shape[0]
    est = 4 * bm * D_MODEL * 4 + 2 * 3 * D_MODEL * EXP_DIM * 2 + 3 * bm * EXP_DIM * 4 + bm * D_MODEL * 2
    row_blk = lambda i, be, nu: (jnp.where(i < nu[0], i, 0), 0)
    return pl.pallas_call(
        _experts_kernel,
        grid_spec=pltpu.PrefetchScalarGridSpec(
            num_scalar_prefetch=2,
            grid=(n_sorted // bm,),
            in_specs=[
                pl.BlockSpec((bm, D_MODEL), row_blk),
                pl.BlockSpec((None, None, D_MODEL, EXP_DIM), lambda i, be, nu: (layer, be[i], 0, 0)),
                pl.BlockSpec((None, None, D_MODEL, EXP_DIM), lambda i, be, nu: (layer, be[i], 0, 0)),
                pl.BlockSpec((None, None, EXP_DIM, D_MODEL), lambda i, be, nu: (layer, be[i], 0, 0)),
            ],
            out_specs=pl.BlockSpec((bm, D_MODEL), lambda i, be, nu: (i, 0)),
        ),
        out_shape=jax.ShapeDtypeStruct((n_sorted, D_MODEL), F32),
        compiler_params=pltpu.CompilerParams(
            dimension_semantics=("arbitrary",), vmem_limit_bytes=_vmem(est)),
        name="moe_experts",
    )(block_expert, n_used, xs, wg, wu, wd)


def _combine_kernel(h_ref, rec_ref, idx_hbm, ys_ref, o_ref, ybuf, idx_smem, row_sem, idx_sem):
    tm = h_ref.shape[0]
    i = pl.program_id(0)
    nsteps = pl.num_programs(0)
    n_idx = idx_smem.shape[0] // 2

    def row_copy(s, k, j, rr, src_row):
        return pltpu.make_async_copy(ys_ref.at[pl.ds(src_row, 1)], ybuf.at[s, k, j, pl.ds(rr, 1)], row_sem.at[s])

    def issue_rows(s):
        def body(j, c):
            base = s * n_idx + MOE_TOP_K * ROWS_PER_ISSUE * j
            for rr in range(ROWS_PER_ISSUE):
                for k in range(MOE_TOP_K):
                    row_copy(s, k, j, rr, idx_smem[base + MOE_TOP_K * rr + k]).start()
            return c
        lax.fori_loop(0, tm // ROWS_PER_ISSUE, body, 0)

    def wait_rows(s):
        def body(r, c):
            for k in range(MOE_TOP_K):
                row_copy(s, k, 0, 0, 0).wait()
            return c
        lax.fori_loop(0, tm, body, 0, unroll=16)

    def step(s):
        @pl.when(i == 0)
        def _():
            _idx_copy(idx_hbm, idx_smem, idx_sem, 0, s).start()
            _idx_copy(idx_hbm, idx_smem, idx_sem, 0, s).wait()
            issue_rows(s)

            @pl.when(nsteps >= 2)
            def _():
                _idx_copy(idx_hbm, idx_smem, idx_sem, 1, 1 - s).start()

        @pl.when(i + 1 < nsteps)
        def _():
            _idx_copy(idx_hbm, idx_smem, idx_sem, i + 1, 1 - s).wait()
            issue_rows(1 - s)

        wait_rows(s)

        @pl.when(i + 2 < nsteps)
        def _():
            _idx_copy(idx_hbm, idx_smem, idx_sem, i + 2, s).start()

        rec = rec_ref[...]
        w1 = rec[:, REC_W:REC_W + 1]
        w2 = rec[:, REC_W + 1:REC_W + 2]
        y1 = ybuf[s, 0].reshape(tm, D_MODEL)
        y2 = ybuf[s, 1].reshape(tm, D_MODEL)
        o_ref[...] = h_ref[...] + w1 * y1 + w2 * y2

    _static_slot(i % 2, step)


def moe_combine(h, rec, idx, ys, *, tm):
    n = h.shape[0]
    est = 4 * tm * D_MODEL * 4 + 2 * tm * LANES * 4 + 2 * MOE_TOP_K * tm * D_MODEL * 4
    return pl.pallas_call(
        _combine_kernel,
        grid=(n // tm,),
        in_specs=[
            pl.BlockSpec((tm, D_MODEL), lambda i: (i, 0)),
            pl.BlockSpec((tm, LANES), lambda i: (i, 0)),
            pl.BlockSpec(memory_space=pl.ANY),
            pl.BlockSpec(memory_space=pl.ANY),
        ],
        out_specs=pl.BlockSpec((tm, D_MODEL), lambda i: (i, 0)),
        out_shape=jax.ShapeDtypeStruct((n, D_MODEL), F32),
        scratch_shapes=[
            pltpu.VMEM((2, MOE_TOP_K, tm // ROWS_PER_ISSUE, ROWS_PER_ISSUE, D_MODEL), F32),
            pltpu.SMEM((2 * MOE_TOP_K * tm,), jnp.int32),
            pltpu.SemaphoreType.DMA((2,)),
            pltpu.SemaphoreType.DMA((2,)),
        ],
        compiler_params=pltpu.CompilerParams(
            dimension_semantics=("arbitrary",), vmem_limit_bytes=_vmem(est)),
        name="moe_combine",
    )(h, rec, idx, ys)


def ffn_moe(h, g, wr, wg, wu, wd, layer, *, bm=512, tm=512):
    n = h.shape[0]
    n_sorted = MOE_TOP_K * n + N_EXP * bm
    rec, cnt = moe_route(h, g, wr)
    expert = rec[:, REC_E:REC_E + MOE_TOP_K].astype(jnp.int32)
    rank = rec[:, REC_RANK:REC_RANK + MOE_TOP_K].astype(jnp.int32)
    counts = cnt[0, :N_EXP].astype(jnp.int32)
    padded = (counts + bm - 1) // bm * bm
    pend = jnp.cumsum(padded)
    pstart = pend - padded
    dest = jnp.sum(jnp.where(expert[..., None] == jnp.arange(N_EXP), pstart, 0), axis=-1) + rank
    idx = dest.reshape(MOE_TOP_K * n)
    n_used = (pend[-1:] // bm).astype(jnp.int32)
    padinfo = jnp.concatenate([pstart + counts, padded - counts, n_used]).astype(jnp.int32)
    blk_start = jnp.arange(n_sorted // bm, dtype=jnp.int32) * bm
    block_expert = jnp.minimum(jnp.sum(blk_start[:, None] >= pend[None, :], axis=1), N_EXP - 1).astype(jnp.int32)
    xs = moe_dispatch(h, g, idx, padinfo, n_sorted, tm=tm, bm=bm)
    ys = moe_experts(block_expert, n_used, xs, wg, wu, wd, layer, bm=bm)
    return moe_combine(h, rec, idx, ys, tm=tm)


def _ple_kernel(h_ref, p_ref, gi_ref, wgate_ref, wp_ref, go_ref, o_ref):
    h = h_ref[...]
    hn = _rms(h, gi_ref[...]).astype(BF16)
    gate = jax.nn.sigmoid(jnp.dot(hn, wgate_ref[...], preferred_element_type=F32))
    e = jnp.dot(p_ref[...].astype(BF16), wp_ref[...], preferred_element_type=F32)
    o_ref[...] = h + gate * _rms(e, go_ref[...])


def ple(h, p_all, layer, gi, wgate, wp, go, *, tm=512):
    n = h.shape[0]
    est = 4 * tm * D_MODEL * 4 + 2 * D_MODEL * D_MODEL * 2 + 2 * PLE_DIM * D_MODEL * 2 + 3 * tm * D_MODEL * 4
    return pl.pallas_call(
        _ple_kernel,
        grid=(n // tm,),
        in_specs=[
            pl.BlockSpec((tm, D_MODEL), lambda i: (i, 0)),
            pl.BlockSpec((None, tm, PLE_DIM), lambda i: (layer, i, 0)),
            pl.BlockSpec((1, D_MODEL), lambda i: (0, 0)),
            pl.BlockSpec((None, D_MODEL, D_MODEL), lambda i: (layer, 0, 0)),
            pl.BlockSpec((None, PLE_DIM, D_MODEL), lambda i: (layer, 0, 0)),
            pl.BlockSpec((1, D_MODEL), lambda i: (0, 0)),
        ],
        out_specs=pl.BlockSpec((tm, D_MODEL), lambda i: (i, 0)),
        out_shape=jax.ShapeDtypeStruct((n, D_MODEL), F32),
        compiler_params=pltpu.CompilerParams(
            dimension_semantics=("parallel",), vmem_limit_bytes=_vmem(est)),
        name="ple",
    )(h, p_all, gi, wgate, wp, go)


def _prep_layer(i, P):
    lr0 = Z_AQ
    n_in = P["w_in"].shape[-1]

    def w_in_cols(a, b):
        return lax.slice(P["w_in"], (i, 0, a), (i + 1, D_MODEL, b)).reshape(D_MODEL, b - a).astype(BF16)

    w_in_r = jnp.concatenate([w_in_cols(0, lr0), w_in_cols(lr0 + 2 * GLA_RANK, n_in)], axis=1)
    w_lr = jnp.concatenate([w_in_cols(lr0, lr0 + 2 * GLA_RANK),
                            jnp.zeros((D_MODEL, LR_W - 2 * GLA_RANK), BF16)], axis=1)
    zpad = jnp.zeros((LR_W, GLA_KW), F32)
    wgf = zpad.at[0:GLA_RANK].set(P["w_gate_f"][i]).astype(BF16)
    wgb = zpad.at[GLA_RANK:2 * GLA_RANK].set(P["w_gate_b"][i]).astype(BF16)
    L = dict(
        g_mix=P["g_mix"][i][None], w_in=w_in_r, w_lr=w_lr, wgf=wgf, wgb=wgb,
        bgf=P["b_gate_f"][i][None], bgb=P["b_gate_b"][i][None],
        gla_norm=P["gla_norm"][i].reshape(GLA_H, 1, GLA_DV),
        q_norm=P["q_norm"][i][None], k_norm=P["k_norm"][i][None],
        bias_tbl=natten_bias_table(P["rpb"][i]), attn_norm=P["attn_norm"][i][None],
        g_ffn=P["g_ffn"][i][None], g_ple_in=P["g_ple_in"][i][None], g_ple_out=P["g_ple_out"][i][None],
    )
    if i % 2 == 1:
        L["wr"] = jnp.zeros((D_MODEL, LANES), F32).at[:, :N_EXP].set(P["w_router"][i // 2]).astype(BF16)
    return L


def _prep_stacked(P):
    names = ("w_o", "w_ple_gate", "w_ple", "w_ff_gate", "w_ff_up", "w_ff_down",
             "w_exp_gate", "w_exp_up", "w_exp_down")
    return {k: P[k].astype(BF16) for k in names}


def _trunk(x, p, layers, W):
    B, T, _ = x.shape
    n = B * T
    h = x.reshape(n, D_MODEL)
    p_all = p.reshape(DEPTH, n, PLE_DIM)
    for i, L in enumerate(layers):
        z, zl = inproj(h, L["g_mix"], L["w_in"], L["w_lr"])
        z3 = z.reshape(B, T, NZ)
        zl3 = zl.reshape(B, T, LR_W)
        o_f = gla_direction(z3, zl3, L["wgf"], L["bgf"], reverse=False)
        o_gla = gla_direction(z3, zl3, L["wgb"], L["bgb"], reverse=True, o_other=o_f, gn=L["gla_norm"])
        o_na = natten(z3, L["bias_tbl"], L["q_norm"], L["k_norm"])
        h = outproj(o_gla.reshape(n, GLA_VW), o_na.reshape(n, NA_W), L["attn_norm"], W["w_o"], i, h)
        if i % 2 == 0:
            h = ffn_dense(h, L["g_ffn"], W["w_ff_gate"], W["w_ff_up"], W["w_ff_down"], i // 2)
        else:
            h = ffn_moe(h, L["g_ffn"], L["wr"], W["w_exp_gate"], W["w_exp_up"], W["w_exp_down"], i // 2)
        h = ple(h, p_all, i, L["g_ple_in"], W["w_ple_gate"], W["w_ple"], L["g_ple_out"])
    return h.reshape(B, T, D_MODEL)


def kernel(x_prompt, x_sample, p_prompt, p_sample, g_mix, w_in, w_gate_f, b_gate_f, w_gate_b, b_gate_b, gla_norm, q_norm, k_norm, rpb, attn_norm, w_o, g_ffn, w_ff_gate, w_ff_up, w_ff_down, w_router, w_exp_gate, w_exp_up, w_exp_down, g_ple_in, w_ple_gate, w_ple, g_ple_out):
    P = dict(g_mix=g_mix, w_in=w_in, w_gate_f=w_gate_f, b_gate_f=b_gate_f, w_gate_b=w_gate_b,
             b_gate_b=b_gate_b, gla_norm=gla_norm, q_norm=q_norm, k_norm=k_norm, rpb=rpb,
             attn_norm=attn_norm, w_o=w_o, g_ffn=g_ffn, w_ff_gate=w_ff_gate, w_ff_up=w_ff_up,
             w_ff_down=w_ff_down, w_router=w_router, w_exp_gate=w_exp_gate, w_exp_up=w_exp_up,
             w_exp_down=w_exp_down, g_ple_in=g_ple_in, w_ple_gate=w_ple_gate, w_ple=w_ple,
             g_ple_out=g_ple_out)
    layers = [_prep_layer(i, P) for i in range(DEPTH)]
    W = _prep_stacked(P)
    return (_trunk(x_prompt, p_prompt, layers, W), _trunk(x_sample, p_sample, layers, W))
```

```python
import functools

import jax
import jax.numpy as jnp
from jax import lax
from jax.experimental import pallas as pl
from jax.experimental.pallas import tpu as pltpu

F32 = jnp.float32
BF16 = jnp.bfloat16

D_MODEL = 2048
DEPTH = 4
GRID_W = 64
GLA_H = 4
GLA_VW = 1024
GLA_KW = 512
GLA_DK = 128
GLA_DV = 256
GLA_RANK = 16
GLA_INV_NORMALIZER = 1.0 / 16.0
GLA_CHUNK = 64
NA_W = 1024
NA_H = 8
NA_HD = 128
WIN_ROWS = 8
WIN_COLS = 16
FF_DIM = 4096
N_EXP = 8
EXP_DIM = 1024
PLE_DIM = 256
EPS = 1e-6
NEG_INF = -1e30

Z_GQ, Z_GK, Z_GV, Z_GG = 0, 512, 1024, 2048
Z_AQ, Z_AK, Z_AV, NZ = 3072, 4096, 5120, 6144
LR_W = 256

V7X_VMEM_BYTES = 64 * 1024 * 1024
LANES = 128


def _vmem(nbytes):
    return int(min(nbytes * 3 // 2 + (4 << 20), V7X_VMEM_BYTES - (6 << 20)))


def _rms(x, g):
    ms = jnp.mean(x * x, axis=-1, keepdims=True)
    return x * lax.rsqrt(ms + EPS) * g


def _silu(x):
    return x * jax.nn.sigmoid(x)


def _inproj_kernel(h_ref, g_ref, w_ref, wl_ref, z_ref, zl_ref, u_scr):
    j = pl.program_id(1)

    @pl.when(j == 0)
    def _():
        u = _rms(h_ref[...], g_ref[...]).astype(BF16)
        u_scr[...] = u
        zl_ref[...] = jnp.dot(u, wl_ref[...], preferred_element_type=F32).astype(zl_ref.dtype)

    @pl.when(j > 0)
    def _():
        z_ref[...] = jnp.dot(u_scr[...], w_ref[...], preferred_element_type=F32).astype(z_ref.dtype)


def inproj(h, g, w, wl, *, tm=1024, tn=1536):
    n = h.shape[0]
    nmain = NZ // tn
    est = (2 * tm * D_MODEL * 4 + 2 * D_MODEL * tn * 2 + 2 * tm * tn * 2 + tm * D_MODEL * 2 + tm * tn * 4
           + 2 * D_MODEL * LR_W * 2 + 2 * tm * LR_W * 2)
    main = lambda j: jnp.maximum(j - 1, 0)
    return pl.pallas_call(
        _inproj_kernel,
        grid=(n // tm, nmain + 1),
        in_specs=[
            pl.BlockSpec((tm, D_MODEL), lambda i, j: (i, 0)),
            pl.BlockSpec((1, D_MODEL), lambda i, j: (0, 0)),
            pl.BlockSpec((D_MODEL, tn), lambda i, j: (0, main(j))),
            pl.BlockSpec((D_MODEL, LR_W), lambda i, j: (0, 0)),
        ],
        out_specs=[pl.BlockSpec((tm, tn), lambda i, j: (i, main(j))),
                   pl.BlockSpec((tm, LR_W), lambda i, j: (i, 0))],
        out_shape=[jax.ShapeDtypeStruct((n, NZ), BF16), jax.ShapeDtypeStruct((n, LR_W), BF16)],
        scratch_shapes=[pltpu.VMEM((tm, D_MODEL), BF16)],
        compiler_params=pltpu.CompilerParams(
            dimension_semantics=("parallel", "arbitrary"), vmem_limit_bytes=_vmem(est)),
        name="inproj",
    )(h, g, w, wl)


def _cumsum_rows(x, seg):
    idx = lax.broadcasted_iota(jnp.int32, x.shape, 0) & (seg - 1)
    s = 1
    while s < seg:
        x = x + jnp.where(idx >= s, pltpu.roll(x, s, 0), 0.0)
        s *= 2
    return x


def _log_sigmoid(x):
    return jnp.minimum(x, 0.0) - jnp.log(1.0 + jnp.exp(-jnp.abs(x)))


def _split_bf16(x):
    hi = x.astype(BF16)
    r1 = x - hi.astype(F32)
    mid = r1.astype(BF16)
    lo = (r1 - mid.astype(F32)).astype(BF16)
    return hi, mid, lo


def _gla_kernel(*refs, reverse, final, tb):
    if final:
        q_ref, k_ref, v_ref, lr_ref, wg_ref, bg_ref, of_ref, gg_ref, gn_ref, o_ref, s_scr = refs
    else:
        q_ref, k_ref, v_ref, lr_ref, wg_ref, bg_ref, o_ref, s_scr = refs
    C = GLA_CHUNK
    nchunk = tb // C

    @pl.when(pl.program_id(2) == 0)
    def _():
        s_scr[...] = jnp.zeros_like(s_scr)

    x = jnp.dot(lr_ref[...], wg_ref[...], preferred_element_type=F32) + bg_ref[...]
    la_parts = _split_bf16(_log_sigmoid(x) * GLA_INV_NORMALIZER)

    row = lax.broadcasted_iota(jnp.int32, (C, C), 0)
    col = lax.broadcasted_iota(jnp.int32, (C, C), 1)
    mask = (col >= row) if reverse else (col <= row)
    tri = jnp.where(mask, 1.0, 0.0).astype(BF16)
    scale = GLA_DK ** -0.5

    order = [(nchunk - 1 - ci) if reverse else ci for ci in range(nchunk)]
    sl = [slice(c * C, (c + 1) * C) for c in range(nchunk)]
    qts, atts, incs, dls = {}, {}, {}, {}
    bs = {c: sum(jnp.dot(tri, part[sl[c]], preferred_element_type=F32) for part in la_parts) for c in order}
    for c in order:
        b = bs[c]
        blast = b[0:1, :] if reverse else b[C - 1:C, :]
        q = q_ref[sl[c], :].astype(F32)
        k = k_ref[sl[c], :].astype(F32)
        qts[c] = (q * jnp.exp(b) * scale).astype(BF16)
        kdec = k * jnp.exp(-b)
        dl = jnp.exp(blast)
        kt = kdec.astype(BF16)
        ke_t = (kdec * dl).T.astype(BF16)
        dls[c] = jnp.broadcast_to(dl, (8, GLA_DK)).T[:, 0:1]
        att = lax.dot_general(qts[c], kt, (((1,), (1,)), ((), ())), preferred_element_type=F32)
        atts[c] = jnp.where(mask, att, 0.0).astype(BF16)
        incs[c] = jnp.dot(ke_t, v_ref[sl[c], :], preferred_element_type=F32)
    state = s_scr[...]
    states = {}
    for c in order:
        states[c] = state.astype(BF16)
        state = dls[c] * state + incs[c]
    s_scr[...] = state
    for c in order:
        o = (jnp.dot(atts[c], v_ref[sl[c], :], preferred_element_type=F32)
             + jnp.dot(qts[c], states[c], preferred_element_type=F32))
        if final:
            y = _rms(o + of_ref[sl[c], :], gn_ref[...])
            gg = gg_ref[sl[c], :].astype(F32)
            o_ref[sl[c], :] = (y * _silu(gg)).astype(o_ref.dtype)
        else:
            o_ref[sl[c], :] = o


def gla_direction(z3, zl3, wg, bg, *, reverse, o_other=None, gn=None, tb=2048):
    B, T, _ = z3.shape
    nb = T // tb
    final = o_other is not None
    blk = (lambda j: nb - 1 - j) if reverse else (lambda j: j)
    qw, vw = GLA_DK, GLA_DV
    in_specs = [
        pl.BlockSpec((None, tb, qw), lambda b, h, j: (b, blk(j), Z_GQ // qw + h)),
        pl.BlockSpec((None, tb, qw), lambda b, h, j: (b, blk(j), Z_GK // qw + h)),
        pl.BlockSpec((None, tb, vw), lambda b, h, j: (b, blk(j), Z_GV // vw + h)),
        pl.BlockSpec((None, tb, LR_W), lambda b, h, j: (b, blk(j), 0)),
        pl.BlockSpec((LR_W, qw), lambda b, h, j: (0, h)),
        pl.BlockSpec((1, qw), lambda b, h, j: (0, h)),
    ]
    args = [z3, z3, z3, zl3, wg, bg]
    if final:
        in_specs += [
            pl.BlockSpec((None, tb, vw), lambda b, h, j: (b, blk(j), h)),
            pl.BlockSpec((None, tb, vw), lambda b, h, j: (b, blk(j), Z_GG // vw + h)),
            pl.BlockSpec((None, 1, vw), lambda b, h, j: (h, 0, 0)),
        ]
        args += [o_other, z3, gn]
    out_dtype = BF16 if final else F32
    est = tb * (2 * 2 * (2 * qw + vw + LR_W) + (2 * (4 + 2) * vw if final else 0) + 2 * 4 * vw + 10 * 4 * qw)
    return pl.pallas_call(
        functools.partial(_gla_kernel, reverse=reverse, final=final, tb=tb),
        grid=(B, GLA_H, nb),
        in_specs=in_specs,
        out_specs=pl.BlockSpec((None, tb, vw), lambda b, h, j: (b, blk(j), h)),
        out_shape=jax.ShapeDtypeStruct((B, T, GLA_VW), out_dtype),
        scratch_shapes=[pltpu.VMEM((GLA_DK, GLA_DV), F32)],
        compiler_params=pltpu.CompilerParams(
            dimension_semantics=("parallel", "parallel", "arbitrary"), vmem_limit_bytes=_vmem(est)),
        name="gla_bwd" if reverse else "gla_fwd",
    )(*args)


def _natten_kernel(q_ref, k_ref, v_ref, bias_ref, qg_ref, kg_ref, o_ref, kn_scr, *, rows, rq):
    T = rows * GRID_W
    KB = 2048
    WK = WIN_ROWS * GRID_W
    j = pl.program_id(2)

    @pl.when(j == 0)
    def _():
        def norm_block(i, carry):
            off = pl.multiple_of(i * KB, KB)
            kk = k_ref[pl.ds(off, KB), :].astype(F32)
            kn_scr[pl.ds(off, KB), :] = _rms(kk, kg_ref[...]).astype(BF16)
            return carry

        lax.fori_loop(0, T // KB, norm_block, 0)

    qn_blk = (_rms(q_ref[...].astype(F32), qg_ref[...]) * (NA_HD ** -0.5)).astype(BF16)
    koffs, probs, sums = [], [], []
    for i in range(rq):
        r = j * rq + i
        rs = jnp.clip(r - WIN_ROWS // 2, 0, rows - WIN_ROWS)
        dti = rs - r + (WIN_ROWS - 1)
        koffs.append(pl.multiple_of(rs * GRID_W, GRID_W))
        kw = kn_scr[pl.ds(koffs[i], WK), :]
        s = lax.dot_general(qn_blk[i * GRID_W:(i + 1) * GRID_W], kw, (((1,), (1,)), ((), ())),
                            preferred_element_type=F32)
        s = s + bias_ref[dti]
        p = jnp.exp(s - jnp.max(s, axis=-1, keepdims=True))
        sums.append(jnp.sum(p, axis=-1, keepdims=True))
        probs.append(p.astype(BF16))
    for i in range(rq):
        vw = v_ref[pl.ds(koffs[i], WK), :]
        o = jnp.dot(probs[i], vw, preferred_element_type=F32) / sums[i]
        o_ref[i * GRID_W:(i + 1) * GRID_W, :] = o.astype(o_ref.dtype)


def natten(z3, bias_tbl, qg, kg, *, rq=64):
    B, T, _ = z3.shape
    rows = T // GRID_W
    assert rows >= WIN_ROWS and rows % rq == 0
    tq = rq * GRID_W
    hd = NA_HD
    est = 2 * 2 * T * hd * 2 + T * hd * 2 + 2 * WIN_ROWS * GRID_W * WIN_ROWS * GRID_W * 4 + 4 * tq * hd * 2
    return pl.pallas_call(
        functools.partial(_natten_kernel, rows=rows, rq=rq),
        grid=(B, NA_H, rows // rq),
        in_specs=[
            pl.BlockSpec((None, tq, hd), lambda b, h, j: (b, j, Z_AQ // hd + h)),
            pl.BlockSpec((None, T, hd), lambda b, h, j: (b, 0, Z_AK // hd + h)),
            pl.BlockSpec((None, T, hd), lambda b, h, j: (b, 0, Z_AV // hd + h)),
            pl.BlockSpec((None, WIN_ROWS, GRID_W, WIN_ROWS * GRID_W), lambda b, h, j: (h, 0, 0, 0)),
            pl.BlockSpec((1, hd), lambda b, h, j: (0, 0)),
            pl.BlockSpec((1, hd), lambda b, h, j: (0, 0)),
        ],
        out_specs=pl.BlockSpec((None, tq, hd), lambda b, h, j: (b, j, h)),
        out_shape=jax.ShapeDtypeStruct((B, T, NA_W), BF16),
        scratch_shapes=[pltpu.VMEM((T, hd), BF16)],
        compiler_params=pltpu.CompilerParams(
            dimension_semantics=("parallel", "parallel", "arbitrary"), vmem_limit_bytes=_vmem(est)),
        name="natten",
    )(z3, z3, z3, bias_tbl, qg, kg)


def natten_bias_table(rpb):
    c = jnp.arange(GRID_W, dtype=jnp.int32)
    c_start = jnp.clip(c - WIN_COLS // 2, 0, GRID_W - WIN_COLS)
    valid = (c[None, :] >= c_start[:, None]) & (c[None, :] < c_start[:, None] + WIN_COLS)
    dc = jnp.clip(c[None, :] - c[:, None], -(WIN_COLS - 1), WIN_COLS - 1) + (WIN_COLS - 1)
    bias_c = jnp.where(valid, rpb.astype(F32)[:, :, dc], NEG_INF)
    dr = jnp.arange(WIN_ROWS)[:, None] + jnp.arange(WIN_ROWS)[None, :]
    tbl = bias_c[:, dr]
    return tbl.transpose(0, 1, 3, 2, 4).reshape(NA_H, WIN_ROWS, GRID_W, WIN_ROWS * GRID_W)


def _outproj_kernel(og_ref, on_ref, an_ref, w_ref, h_ref, o_ref):
    na = _rms(on_ref[...].astype(F32), an_ref[...]).astype(BF16)
    acc = jnp.dot(og_ref[...], w_ref[0:GLA_VW, :], preferred_element_type=F32)
    acc += jnp.dot(na, w_ref[GLA_VW:GLA_VW + NA_W, :], preferred_element_type=F32)
    o_ref[...] = h_ref[...] + acc


def outproj(o_gla, o_na, an, w_o, layer, h, *, tm=512):
    n = h.shape[0]
    est = 2 * 2 * tm * 1024 * 2 + 2 * D_MODEL * D_MODEL * 2 + 4 * tm * D_MODEL * 4 + tm * D_MODEL * 4
    return pl.pallas_call(
        _outproj_kernel,
        grid=(n // tm,),
        in_specs=[
            pl.BlockSpec((tm, GLA_VW), lambda i: (i, 0)),
            pl.BlockSpec((tm, NA_W), lambda i: (i, 0)),
            pl.BlockSpec((1, NA_W), lambda i: (0, 0)),
            pl.BlockSpec((None, D_MODEL, D_MODEL), lambda i: (layer, 0, 0)),
            pl.BlockSpec((tm, D_MODEL), lambda i: (i, 0)),
        ],
        out_specs=pl.BlockSpec((tm, D_MODEL), lambda i: (i, 0)),
        out_shape=jax.ShapeDtypeStruct((n, D_MODEL), F32),
        compiler_params=pltpu.CompilerParams(
            dimension_semantics=("parallel",), vmem_limit_bytes=_vmem(est)),
        name="outproj",
    )(o_gla, o_na, an, w_o, h)


FFN_TF = 512


def _ffn_kernel(h_ref, g_ref, wg_ref, wu_ref, wd_ref, o_ref, u_scr):
    @pl.when(pl.program_id(1) == 0)
    def _():
        h = h_ref[...]
        u_scr[...] = _rms(h, g_ref[...]).astype(BF16)
        o_ref[...] = h

    u = u_scr[...]
    a = jnp.dot(u, wg_ref[...], preferred_element_type=F32)
    b = jnp.dot(u, wu_ref[...], preferred_element_type=F32)
    hid = (_silu(a) * b).astype(BF16)
    o_ref[...] += jnp.dot(hid, wd_ref[...], preferred_element_type=F32)


def ffn_dense(h, g, wg, wu, wd, layer, *, tm=1024, tf=FFN_TF):
    n = h.shape[0]
    est = 4 * tm * D_MODEL * 4 + tm * D_MODEL * 2 + 2 * 3 * D_MODEL * tf * 2 + 3 * tm * tf * 4
    return pl.pallas_call(
        _ffn_kernel,
        grid=(n // tm, FF_DIM // tf),
        in_specs=[
            pl.BlockSpec((tm, D_MODEL), lambda i, f: (i, 0)),
            pl.BlockSpec((1, D_MODEL), lambda i, f: (0, 0)),
            pl.BlockSpec((None, D_MODEL, tf), lambda i, f: (layer, 0, f)),
            pl.BlockSpec((None, D_MODEL, tf), lambda i, f: (layer, 0, f)),
            pl.BlockSpec((None, tf, D_MODEL), lambda i, f: (layer, f, 0)),
        ],
        out_specs=pl.BlockSpec((tm, D_MODEL), lambda i, f: (i, 0)),
        out_shape=jax.ShapeDtypeStruct((n, D_MODEL), F32),
        scratch_shapes=[pltpu.VMEM((tm, D_MODEL), BF16)],
        compiler_params=pltpu.CompilerParams(
            dimension_semantics=("parallel", "arbitrary"), vmem_limit_bytes=_vmem(est)),
        name="ffn_dense",
    )(h, g, wg, wu, wd)


REC_E, REC_W, REC_RANK = 0, 2, 4
MOE_TOP_K = 2


def _route_kernel(h_ref, g_ref, wr_ref, rec_ref, cnt_ref, carry_scr):
    tm = h_ref.shape[0]
    lane = lax.broadcasted_iota(jnp.int32, (tm, LANES), 1)

    @pl.when(pl.program_id(0) == 0)
    def _():
        carry_scr[...] = jnp.zeros_like(carry_scr)

    u = _rms(h_ref[...], g_ref[...]).astype(BF16)
    logits = jnp.dot(u, wr_ref[...], preferred_element_type=F32)
    lg = jnp.where(lane < N_EXP, logits, -jnp.inf)
    v1 = jnp.max(lg, axis=-1, keepdims=True)
    i1 = jnp.min(jnp.where(lg == v1, lane, LANES), axis=-1, keepdims=True)
    lg2 = jnp.where(lane == i1, -jnp.inf, lg)
    v2 = jnp.max(lg2, axis=-1, keepdims=True)
    i2 = jnp.min(jnp.where(lg2 == v2, lane, LANES), axis=-1, keepdims=True)
    e2 = jnp.exp(v2 - v1)
    w1 = 1.0 / (1.0 + e2)
    w2 = e2 / (1.0 + e2)
    onehot = jnp.where(lane == i1, 1.0, 0.0) + jnp.where(lane == i2, 1.0, 0.0)
    incl = _cumsum_rows(onehot, tm)
    before = incl - onehot + carry_scr[0:1, :]
    r1 = jnp.sum(jnp.where(lane == i1, before, 0.0), axis=-1, keepdims=True)
    r2 = jnp.sum(jnp.where(lane == i2, before, 0.0), axis=-1, keepdims=True)
    rec = jnp.where(lane == REC_E, i1.astype(F32), 0.0)
    rec = jnp.where(lane == REC_E + 1, i2.astype(F32), rec)
    rec = jnp.where(lane == REC_W, w1, rec)
    rec = jnp.where(lane == REC_W + 1, w2, rec)
    rec = jnp.where(lane == REC_RANK, r1, rec)
    rec = jnp.where(lane == REC_RANK + 1, r2, rec)
    rec_ref[...] = rec
    carry_scr[...] = carry_scr[...] + incl[tm - 1:tm, :]
    cnt_ref[...] = carry_scr[...]


def moe_route(h, g, wr, *, tm=512):
    n = h.shape[0]
    est = 2 * tm * D_MODEL * 4 + D_MODEL * LANES * 4 + 8 * tm * LANES * 4
    return pl.pallas_call(
        _route_kernel,
        grid=(n // tm,),
        in_specs=[
            pl.BlockSpec((tm, D_MODEL), lambda i: (i, 0)),
            pl.BlockSpec((1, D_MODEL), lambda i: (0, 0)),
            pl.BlockSpec((D_MODEL, LANES), lambda i: (0, 0)),
        ],
        out_specs=[pl.BlockSpec((tm, LANES), lambda i: (i, 0)), pl.BlockSpec((8, LANES), lambda i: (0, 0))],
        out_shape=[jax.ShapeDtypeStruct((n, LANES), F32), jax.ShapeDtypeStruct((8, LANES), F32)],
        scratch_shapes=[pltpu.VMEM((8, LANES), F32)],
        compiler_params=pltpu.CompilerParams(
            dimension_semantics=("arbitrary",), vmem_limit_bytes=_vmem(est)),
        name="moe_route",
    )(h, g, wr)


def _static_slot(slot, fn):
    for s in range(2):
        pl.when(slot == s)(functools.partial(fn, s))


ROWS_PER_ISSUE = 8


def _idx_copy(idx_hbm, idx_smem, sem, step, s):
    n = idx_smem.shape[0] // 2
    return pltpu.make_async_copy(idx_hbm.at[pl.ds(pl.multiple_of(step * n, n), n)],
                                 idx_smem.at[pl.ds(s * n, n)], sem.at[s])


def _dispatch_kernel(pad_ref, h_ref, g_ref, idx_hbm, xs_ref, ubuf, zrow, idx_smem, row_sem, idx_sem, pad_sem):
    tm = h_ref.shape[0]
    i = pl.program_id(0)
    nsteps = pl.num_programs(0)
    n_idx = idx_smem.shape[0] // 2

    def row_copy(s, j, rr, dst_row):
        return pltpu.make_async_copy(ubuf.at[s, j, pl.ds(rr, 1)], xs_ref.at[pl.ds(dst_row, 1)], row_sem.at[s])

    def wait_rows(s):
        def body(r, c):
            for _ in range(MOE_TOP_K):
                row_copy(s, 0, 0, 0).wait()
            return c
        lax.fori_loop(0, tm, body, 0, unroll=16)

    def issue_rows(s):
        def body(j, c):
            base = s * n_idx + MOE_TOP_K * ROWS_PER_ISSUE * j
            for rr in range(ROWS_PER_ISSUE):
                for k in range(MOE_TOP_K):
                    row_copy(s, j, rr, idx_smem[base + MOE_TOP_K * rr + k]).start(priority=k)
            return c
        lax.fori_loop(0, tm // ROWS_PER_ISSUE, body, 0)

    def zero_pad_rows():
        zrow[...] = jnp.zeros_like(zrow)
        for e in range(N_EXP):
            def pad_copy(j, e=e):
                return pltpu.make_async_copy(zrow.at[pl.ds(0, 1)], xs_ref.at[pl.ds(pad_ref[e] + j, 1)], pad_sem.at[0])

            def start(j, c, pad_copy=pad_copy):
                pad_copy(j).start()
                return c

            def wait(j, c, pad_copy=pad_copy):
                pad_copy(j).wait()
                return c
            lax.fori_loop(0, pad_ref[N_EXP + e], start, 0)
            lax.fori_loop(0, pad_ref[N_EXP + e], wait, 0)

        bm = zrow.shape[0]

        def tail_copy(b):
            return pltpu.make_async_copy(zrow, xs_ref.at[pl.ds(pl.multiple_of(b * bm, bm), bm)], pad_sem.at[0])

        def tail_start(b, c):
            tail_copy(b).start()
            return c

        def tail_wait(b, c):
            tail_copy(b).wait()
            return c
        lax.fori_loop(pad_ref[2 * N_EXP], xs_ref.shape[0] // bm, tail_start, 0)
        lax.fori_loop(pad_ref[2 * N_EXP], xs_ref.shape[0] // bm, tail_wait, 0)

    def step(s):
        @pl.when(i == 0)
        def _():
            _idx_copy(idx_hbm, idx_smem, idx_sem, 0, s).start()

        _idx_copy(idx_hbm, idx_smem, idx_sem, i, s).wait()

        @pl.when(i + 1 < nsteps)
        def _():
            _idx_copy(idx_hbm, idx_smem, idx_sem, i + 1, 1 - s).start()

        @pl.when(i >= 2)
        def _():
            wait_rows(s)

        ubuf[s] = _rms(h_ref[...], g_ref[...]).reshape(tm // ROWS_PER_ISSUE, ROWS_PER_ISSUE, D_MODEL)
        issue_rows(s)

        @pl.when(i == nsteps - 1)
        def _():
            zero_pad_rows()
            wait_rows(s)

            @pl.when(nsteps >= 2)
            def _():
                wait_rows(1 - s)

    _static_slot(i % 2, step)


def moe_dispatch(h, g, idx, padinfo, n_sorted, *, tm, bm):
    n = h.shape[0]
    est = 2 * tm * D_MODEL * 4 + 2 * tm * D_MODEL * 4 + bm * D_MODEL * 4
    return pl.pallas_call(
        _dispatch_kernel,
        grid_spec=pltpu.PrefetchScalarGridSpec(
            num_scalar_prefetch=1,
            grid=(n // tm,),
            in_specs=[
                pl.BlockSpec((tm, D_MODEL), lambda i, pad: (i, 0)),
                pl.BlockSpec((1, D_MODEL), lambda i, pad: (0, 0)),
                pl.BlockSpec(memory_space=pl.ANY),
            ],
            out_specs=pl.BlockSpec(memory_space=pl.ANY),
            scratch_shapes=[
                pltpu.VMEM((2, tm // ROWS_PER_ISSUE, ROWS_PER_ISSUE, D_MODEL), F32),
                pltpu.VMEM((bm, D_MODEL), F32),
                pltpu.SMEM((2 * MOE_TOP_K * tm,), jnp.int32),
                pltpu.SemaphoreType.DMA((2,)),
                pltpu.SemaphoreType.DMA((2,)),
                pltpu.SemaphoreType.DMA((1,)),
            ],
        ),
        out_shape=jax.ShapeDtypeStruct((n_sorted, D_MODEL), F32),
        compiler_params=pltpu.CompilerParams(
            dimension_semantics=("arbitrary",), vmem_limit_bytes=_vmem(est)),
        name="moe_dispatch",
    )(padinfo, h, g, idx)


def _experts_kernel(be_ref, nu_ref, x_ref, wg_ref, wu_ref, wd_ref, y_ref):
    del be_ref
    used = pl.program_id(0) < nu_ref[0]

    @pl.when(used)
    def _():
        x = x_ref[...].astype(BF16)
        a = jnp.dot(x, wg_ref[...], preferred_element_type=F32)
        b = jnp.dot(x, wu_ref[...], preferred_element_type=F32)
        hid = (_silu(a) * b).astype(BF16)
        y_ref[...] = jnp.dot(hid, wd_ref[...], preferred_element_type=F32)

    @pl.when(jnp.logical_not(used))
    def _():
        y_ref[...] = jnp.zeros_like(y_ref)


def moe_experts(block_expert, n_used, xs, wg, wu, wd, layer, *, bm):
    n_sorted = xs.shape[0]
    est = 4 * bm * D_MODEL * 4 + 2 * 3 * D_MODEL * EXP_DIM * 2 + 3 * bm * EXP_DIM * 4 + bm * D_MODEL * 2
    row_blk = lambda i, be, nu: (jnp.where(i < nu[0], i, 0), 0)
    return pl.pallas_call(
        _experts_kernel,
        grid_spec=pltpu.PrefetchScalarGridSpec(
            num_scalar_prefetch=2,
            grid=(n_sorted // bm,),
            in_specs=[
                pl.BlockSpec((bm, D_MODEL), row_blk),
                pl.BlockSpec((None, None, D_MODEL, EXP_DIM), lambda i, be, nu: (layer, be[i], 0, 0)),
                pl.BlockSpec((None, None, D_MODEL, EXP_DIM), lambda i, be, nu: (layer, be[i], 0, 0)),
                pl.BlockSpec((None, None, EXP_DIM, D_MODEL), lambda i, be, nu: (layer, be[i], 0, 0)),
            ],
            out_specs=pl.BlockSpec((bm, D_MODEL), lambda i, be, nu: (i, 0)),
        ),
        out_shape=jax.ShapeDtypeStruct((n_sorted, D_MODEL), F32),
        compiler_params=pltpu.CompilerParams(
            dimension_semantics=("arbitrary",), vmem_limit_bytes=_vmem(est)),
        name="moe_experts",
    )(block_expert, n_used, xs, wg, wu, wd)


def _combine_kernel(h_ref, rec_ref, idx_hbm, ys_ref, o_ref, ybuf, idx_smem, row_sem, idx_sem):
    tm = h_ref.shape[0]
    i = pl.program_id(0)
    nsteps = pl.num_programs(0)
    n_idx = idx_smem.shape[0] // 2

    def row_copy(s, k, j, rr, src_row):
        return pltpu.make_async_copy(ys_ref.at[pl.ds(src_row, 1)], ybuf.at[s, k, j, pl.ds(rr, 1)], row_sem.at[s])

    def issue_rows(s):
        def body(j, c):
            base = s * n_idx + MOE_TOP_K * ROWS_PER_ISSUE * j
            for rr in range(ROWS_PER_ISSUE):
                for k in range(MOE_TOP_K):
                    row_copy(s, k, j, rr, idx_smem[base + MOE_TOP_K * rr + k]).start(priority=k)
            return c
        lax.fori_loop(0, tm // ROWS_PER_ISSUE, body, 0)

    def wait_rows(s):
        def body(r, c):
            for k in range(MOE_TOP_K):
                row_copy(s, k, 0, 0, 0).wait()
            return c
        lax.fori_loop(0, tm, body, 0, unroll=16)

    def step(s):
        @pl.when(i == 0)
        def _():
            _idx_copy(idx_hbm, idx_smem, idx_sem, 0, s).start()
            _idx_copy(idx_hbm, idx_smem, idx_sem, 0, s).wait()
            issue_rows(s)

            @pl.when(nsteps >= 2)
            def _():
                _idx_copy(idx_hbm, idx_smem, idx_sem, 1, 1 - s).start()

        @pl.when(i + 1 < nsteps)
        def _():
            _idx_copy(idx_hbm, idx_smem, idx_sem, i + 1, 1 - s).wait()
            issue_rows(1 - s)

        wait_rows(s)

        @pl.when(i + 2 < nsteps)
        def _():
            _idx_copy(idx_hbm, idx_smem, idx_sem, i + 2, s).start()

        rec = rec_ref[...]
        w1 = rec[:, REC_W:REC_W + 1]
        w2 = rec[:, REC_W + 1:REC_W + 2]
        y1 = ybuf[s, 0].reshape(tm, D_MODEL)
        y2 = ybuf[s, 1].reshape(tm, D_MODEL)
        o_ref[...] = h_ref[...] + w1 * y1 + w2 * y2

    _static_slot(i % 2, step)


def moe_combine(h, rec, idx, ys, *, tm):
    n = h.shape[0]
    est = 4 * tm * D_MODEL * 4 + 2 * tm * LANES * 4 + 2 * MOE_TOP_K * tm * D_MODEL * 4
    return pl.pallas_call(
        _combine_kernel,
        grid=(n // tm,),
        in_specs=[
            pl.BlockSpec((tm, D_MODEL), lambda i: (i, 0)),
            pl.BlockSpec((tm, LANES), lambda i: (i, 0)),
            pl.BlockSpec(memory_space=pl.ANY),
            pl.BlockSpec(memory_space=pl.ANY),
        ],
        out_specs=pl.BlockSpec((tm, D_MODEL), lambda i: (i, 0)),
        out_shape=jax.ShapeDtypeStruct((n, D_MODEL), F32),
        scratch_shapes=[
            pltpu.VMEM((2, MOE_TOP_K, tm // ROWS_PER_ISSUE, ROWS_PER_ISSUE, D_MODEL), F32),
            pltpu.SMEM((2 * MOE_TOP_K * tm,), jnp.int32),
            pltpu.SemaphoreType.DMA((2,)),
            pltpu.SemaphoreType.DMA((2,)),
        ],
        compiler_params=pltpu.CompilerParams(
            dimension_semantics=("arbitrary",), vmem_limit_bytes=_vmem(est)),
        name="moe_combine",
    )(h, rec, idx, ys)


def ffn_moe(h, g, wr, wg, wu, wd, layer, *, bm=512, tm=512):
    n = h.shape[0]
    n_sorted = MOE_TOP_K * n + N_EXP * bm
    rec, cnt = moe_route(h, g, wr)
    expert = rec[:, REC_E:REC_E + MOE_TOP_K].astype(jnp.int32)
    rank = rec[:, REC_RANK:REC_RANK + MOE_TOP_K].astype(jnp.int32)
    counts = cnt[0, :N_EXP].astype(jnp.int32)
    padded = (counts + bm - 1) // bm * bm
    pend = jnp.cumsum(padded)
    pstart = pend - padded
    dest = jnp.sum(jnp.where(expert[..., None] == jnp.arange(N_EXP), pstart, 0), axis=-1) + rank
    idx = dest.reshape(MOE_TOP_K * n)
    n_used = (pend[-1:] // bm).astype(jnp.int32)
    padinfo = jnp.concatenate([pstart + counts, padded - counts, n_used]).astype(jnp.int32)
    blk_start = jnp.arange(n_sorted // bm, dtype=jnp.int32) * bm
    block_expert = jnp.minimum(jnp.sum(blk_start[:, None] >= pend[None, :], axis=1), N_EXP - 1).astype(jnp.int32)
    xs = moe_dispatch(h, g, idx, padinfo, n_sorted, tm=tm, bm=bm)
    ys = moe_experts(block_expert, n_used, xs, wg, wu, wd, layer, bm=bm)
    return moe_combine(h, rec, idx, ys, tm=tm)


def _ple_kernel(h_ref, p_ref, gi_ref, wgate_ref, wp_ref, go_ref, o_ref):
    h = h_ref[...]
    hn = _rms(h, gi_ref[...]).astype(BF16)
    gate = jax.nn.sigmoid(jnp.dot(hn, wgate_ref[...], preferred_element_type=F32))
    e = jnp.dot(p_ref[...].astype(BF16), wp_ref[...], preferred_element_type=F32)
    o_ref[...] = h + gate * _rms(e, go_ref[...])


def ple(h, p_all, layer, gi, wgate, wp, go, *, tm=512):
    n = h.shape[0]
    est = 4 * tm * D_MODEL * 4 + 2 * D_MODEL * D_MODEL * 2 + 2 * PLE_DIM * D_MODEL * 2 + 3 * tm * D_MODEL * 4
    return pl.pallas_call(
        _ple_kernel,
        grid=(n // tm,),
        in_specs=[
            pl.BlockSpec((tm, D_MODEL), lambda i: (i, 0)),
            pl.BlockSpec((None, tm, PLE_DIM), lambda i: (layer, i, 0)),
            pl.BlockSpec((1, D_MODEL), lambda i: (0, 0)),
            pl.BlockSpec((None, D_MODEL, D_MODEL), lambda i: (layer, 0, 0)),
            pl.BlockSpec((None, PLE_DIM, D_MODEL), lambda i: (layer, 0, 0)),
            pl.BlockSpec((1, D_MODEL), lambda i: (0, 0)),
        ],
        out_specs=pl.BlockSpec((tm, D_MODEL), lambda i: (i, 0)),
        out_shape=jax.ShapeDtypeStruct((n, D_MODEL), F32),
        compiler_params=pltpu.CompilerParams(
            dimension_semantics=("parallel",), vmem_limit_bytes=_vmem(est)),
        name="ple",
    )(h, p_all, gi, wgate, wp, go)


def _prep_layer(i, P):
    lr0 = Z_AQ
    n_in = P["w_in"].shape[-1]

    def w_in_cols(a, b):
        return lax.slice(P["w_in"], (i, 0, a), (i + 1, D_MODEL, b)).reshape(D_MODEL, b - a).astype(BF16)

    w_in_r = jnp.concatenate([w_in_cols(0, lr0), w_in_cols(lr0 + 2 * GLA_RANK, n_in)], axis=1)
    w_lr = jnp.concatenate([w_in_cols(lr0, lr0 + 2 * GLA_RANK),
                            jnp.zeros((D_MODEL, LR_W - 2 * GLA_RANK), BF16)], axis=1)
    zpad = jnp.zeros((LR_W, GLA_KW), F32)
    wgf = zpad.at[0:GLA_RANK].set(P["w_gate_f"][i]).astype(BF16)
    wgb = zpad.at[GLA_RANK:2 * GLA_RANK].set(P["w_gate_b"][i]).astype(BF16)
    L = dict(
        g_mix=P["g_mix"][i][None], w_in=w_in_r, w_lr=w_lr, wgf=wgf, wgb=wgb,
        bgf=P["b_gate_f"][i][None], bgb=P["b_gate_b"][i][None],
        gla_norm=P["gla_norm"][i].reshape(GLA_H, 1, GLA_DV),
        q_norm=P["q_norm"][i][None], k_norm=P["k_norm"][i][None],
        bias_tbl=natten_bias_table(P["rpb"][i]), attn_norm=P["attn_norm"][i][None],
        g_ffn=P["g_ffn"][i][None], g_ple_in=P["g_ple_in"][i][None], g_ple_out=P["g_ple_out"][i][None],
    )
    if i % 2 == 1:
        L["wr"] = jnp.zeros((D_MODEL, LANES), F32).at[:, :N_EXP].set(P["w_router"][i // 2]).astype(BF16)
    return L


def _prep_stacked(P):
    names = ("w_o", "w_ple_gate", "w_ple", "w_ff_gate", "w_ff_up", "w_ff_down",
             "w_exp_gate", "w_exp_up", "w_exp_down")
    return {k: P[k].astype(BF16) for k in names}


def _trunk(x, p, layers, W):
    B, T, _ = x.shape
    n = B * T
    h = x.reshape(n, D_MODEL)
    p_all = p.reshape(DEPTH, n, PLE_DIM)
    for i, L in enumerate(layers):
        z, zl = inproj(h, L["g_mix"], L["w_in"], L["w_lr"])
        z3 = z.reshape(B, T, NZ)
        zl3 = zl.reshape(B, T, LR_W)
        o_f = gla_direction(z3, zl3, L["wgf"], L["bgf"], reverse=False)
        o_gla = gla_direction(z3, zl3, L["wgb"], L["bgb"], reverse=True, o_other=o_f, gn=L["gla_norm"])
        o_na = natten(z3, L["bias_tbl"], L["q_norm"], L["k_norm"])
        h = outproj(o_gla.reshape(n, GLA_VW), o_na.reshape(n, NA_W), L["attn_norm"], W["w_o"], i, h)
        if i % 2 == 0:
            h = ffn_dense(h, L["g_ffn"], W["w_ff_gate"], W["w_ff_up"], W["w_ff_down"], i // 2)
        else:
            h = ffn_moe(h, L["g_ffn"], L["wr"], W["w_exp_gate"], W["w_exp_up"], W["w_exp_down"], i // 2)
        h = ple(h, p_all, i, L["g_ple_in"], W["w_ple_gate"], W["w_ple"], L["g_ple_out"])
    return h.reshape(B, T, D_MODEL)


def kernel(x_prompt, x_sample, p_prompt, p_sample, g_mix, w_in, w_gate_f, b_gate_f, w_gate_b, b_gate_b, gla_norm, q_norm, k_norm, rpb, attn_norm, w_o, g_ffn, w_ff_gate, w_ff_up, w_ff_down, w_router, w_exp_gate, w_exp_up, w_exp_down, g_ple_in, w_ple_gate, w_ple, g_ple_out):
    P = dict(g_mix=g_mix, w_in=w_in, w_gate_f=w_gate_f, b_gate_f=b_gate_f, w_gate_b=w_gate_b,
             b_gate_b=b_gate_b, gla_norm=gla_norm, q_norm=q_norm, k_norm=k_norm, rpb=rpb,
             attn_norm=attn_norm, w_o=w_o, g_ffn=g_ffn, w_ff_gate=w_ff_gate, w_ff_up=w_ff_up,
             w_ff_down=w_ff_down, w_router=w_router, w_exp_gate=w_exp_gate, w_exp_up=w_exp_up,
             w_exp_down=w_exp_down, g_ple_in=g_ple_in, w_ple_gate=w_ple_gate, w_ple=w_ple,
             g_ple_out=g_ple_out)
    layers = [_prep_layer(i, P) for i in range(DEPTH)]
    W = _prep_stacked(P)
    return (_trunk(x_prompt, p_prompt, layers, W), _trunk(x_sample, p_sample, layers, W))
```
